```python
import jax, jax.numpy as jnp
from jax import lax
import numpy as np

D_MODEL = 2048
BATCH = 4
SEQ = 2048
DEPTH = 4

CHUNK = 64
EPS = 1e-6
NEG_INF = -1e30
A_HEAD_DIM = 128
A_HEADS = (D_MODEL // 2) // A_HEAD_DIM
A_WIDTH = A_HEADS * A_HEAD_DIM
LEFT_CHUNKS = 8
BAND_CHUNKS = LEFT_CHUNKS + 1
REL_CLIP = 256
M_HEADS = 4
M_HEAD_DIM = (D_MODEL // 2) // M_HEADS
M_WIDTH = M_HEADS * M_HEAD_DIM
CONV_W = 4
IN_SIZES = (A_WIDTH, A_WIDTH, A_WIDTH, M_WIDTH, M_WIDTH, M_WIDTH, M_WIDTH, D_MODEL, D_MODEL, 2 * M_HEADS)
IN_COLS = sum(IN_SIZES)
N_GROUPS = 4
EXPERTS_PER_GROUP = 8
N_EXPERTS = N_GROUPS * EXPERTS_PER_GROUP
TOP_K = 2
D_EXPERT = (3 * D_MODEL) // 8
MOE_BLOCK = 128

kernel_name = "hybrid_chunkattn_mlstm_hiermoe_adaln"


def _rms(t, w):
    tf = t.astype(jnp.float32)
    y = tf * lax.rsqrt(jnp.mean(tf * tf, axis=-1, keepdims=True) + EPS)
    return (y * w.astype(jnp.float32)).astype(t.dtype)


def _split_cols(t, sizes):
    out, start = [], 0
    for sz in sizes:
        out.append(t[..., start:start + sz])
        start += sz
    return out


def _causal_conv(t, w):
    ch = t.shape[-1]
    return lax.conv_general_dilated(t, w[:, None, :].astype(t.dtype), window_strides=(1,),
                                    padding=[(CONV_W - 1, 0)], dimension_numbers=('NWC', 'WIO', 'NWC'),
                                    feature_group_count=ch)


def _chunk_attention(q, k, v, rel_table):
    b, s, nh, dh = q.shape
    nc = s // CHUNK
    q = q.reshape(b, nc, CHUNK, nh, dh)
    pad = ((0, 0), (LEFT_CHUNKS, 0), (0, 0), (0, 0), (0, 0))
    kp = jnp.pad(k.reshape(b, nc, CHUNK, nh, dh), pad)
    vp = jnp.pad(v.reshape(b, nc, CHUNK, nh, dh), pad)
    band = jnp.arange(nc)[:, None] + jnp.arange(BAND_CHUNKS)[None, :]
    band_len = BAND_CHUNKS * CHUNK
    kb = kp[:, band].reshape(b, nc, band_len, nh, dh)
    vb = vp[:, band].reshape(b, nc, band_len, nh, dh)
    key_ok = jnp.repeat(band >= LEFT_CHUNKS, CHUNK, axis=1)
    dist = jnp.arange(CHUNK)[:, None] + LEFT_CHUNKS * CHUNK - jnp.arange(band_len)[None, :]
    bias = rel_table[:, jnp.clip(dist, -REL_CLIP, REL_CLIP) + REL_CLIP].astype(jnp.float32)
    scores = jnp.einsum('bcqhd,bckhd->bhcqk', q, kb).astype(jnp.float32) * (dh ** -0.5) + bias[None, :, None]
    scores = jnp.where(key_ok[None, None, :, None, :], scores, NEG_INF)
    probs = jax.nn.softmax(scores, axis=-1).astype(v.dtype)
    out = jnp.einsum('bhcqk,bckhd->bcqhd', probs, vb)
    return out.reshape(b, s, nh * dh)


def _mlstm(q, k, v, ig, fg):
    b, s, nh, dh = q.shape
    nc = s // CHUNK
    f32 = jnp.float32
    to_c = lambda t: t.astype(f32).reshape(b, nc, CHUNK, nh, dh).transpose(1, 0, 3, 2, 4)
    to_cg = lambda t: t.astype(f32).reshape(b, nc, CHUNK, nh).transpose(1, 0, 3, 2)
    qc, kc, vc = to_c(q * (dh ** -0.5)), to_c(k), to_c(v)
    igc = to_cg(ig)
    lfc = to_cg(jax.nn.log_sigmoid(fg.astype(f32)))
    causal = jnp.tril(jnp.ones((CHUNK, CHUNK), dtype=bool))

    def step(carry, inp):
        C, n, m = carry
        qx, kx, vx, igx, lfx = inp
        bcum = jnp.cumsum(lfx, axis=-1)
        logd = bcum[..., :, None] - bcum[..., None, :] + igx[..., None, :]
        logd = jnp.where(causal, logd, NEG_INF)
        inter = bcum + m[..., None]
        m_s = jnp.maximum(jnp.max(logd, axis=-1), inter)
        w_intra = jnp.einsum('bhsd,bhrd->bhsr', qx, kx) * jnp.exp(logd - m_s[..., None])
        w_inter = jnp.exp(inter - m_s)
        num = jnp.einsum('bhsr,bhrd->bhsd', w_intra, vx) + w_inter[..., None] * jnp.einsum('bhvk,bhsk->bhsv', C, qx)
        den = jnp.sum(w_intra, axis=-1) + w_inter * jnp.einsum('bhk,bhsk->bhs', n, qx)
        hs = num / jnp.maximum(jnp.abs(den), jnp.exp(-m_s))[..., None]
        b_last = bcum[..., -1]
        log_wk = b_last[..., None] - bcum + igx
        m_new = jnp.maximum(b_last + m, jnp.max(log_wk, axis=-1))
        wk = jnp.exp(log_wk - m_new[..., None])
        decay = jnp.exp(b_last + m - m_new)
        C_new = decay[..., None, None] * C + jnp.einsum('bhr,bhrv,bhrk->bhvk', wk, vx, kx)
        n_new = decay[..., None] * n + jnp.einsum('bhr,bhrk->bhk', wk, kx)
        return (C_new, n_new, m_new), hs

    init = (jnp.zeros((b, nh, dh, dh), f32), jnp.zeros((b, nh, dh), f32), jnp.zeros((b, nh), f32))
    _, hs = lax.scan(step, init, (qc, kc, vc, igc, lfc))
    return hs.transpose(1, 0, 3, 2, 4).reshape(b, s, nh, dh)


def _mixer(h, w_in, conv_q, conv_k, igate_b, fgate_b, rel_table, mlstm_norm_w, w_ba, w_bm, w_out):
    b, s, _ = h.shape
    aq, ak, av, mq, mk, mv, mo, ga, gm, gates = _split_cols(h @ w_in, IN_SIZES)
    a_heads = lambda t: t.reshape(b, s, A_HEADS, A_HEAD_DIM)
    y_attn = _chunk_attention(a_heads(aq), a_heads(ak), a_heads(av), rel_table)
    m_heads = lambda t: t.reshape(b, s, M_HEADS, M_HEAD_DIM)
    mq = jax.nn.silu(_causal_conv(mq, conv_q))
    mk = jax.nn.silu(_causal_conv(mk, conv_k))
    ig = gates[..., :M_HEADS].astype(jnp.float32) + igate_b
    fg = gates[..., M_HEADS:].astype(jnp.float32) + fgate_b
    h_t = _mlstm(m_heads(mq), m_heads(mk), m_heads(mv), ig, fg)
    h_m = jax.nn.sigmoid(m_heads(mo).astype(jnp.float32)) * h_t
    h_m = _rms(h_m, mlstm_norm_w.reshape(M_HEADS, M_HEAD_DIM)).reshape(b, s, M_WIDTH).astype(h.dtype)
    merged = jax.nn.sigmoid(ga) * (y_attn @ w_ba) + jax.nn.sigmoid(gm) * (h_m @ w_bm)
    return merged @ w_out


def _hier_moe(h, w_coarse, b_coarse, w_fine, b_fine, w_gate, w_up, w_down):
    n_tok, d = h.shape
    rows = jnp.arange(n_tok)
    coarse_logits = (h @ w_coarse).astype(jnp.float32) + b_coarse
    grp = jnp.argmax(coarse_logits, axis=-1)
    p_grp = jax.nn.softmax(coarse_logits, axis=-1)[rows, grp]
    fine_logits = ((h @ w_fine).astype(jnp.float32) + b_fine).reshape(n_tok, N_GROUPS, EXPERTS_PER_GROUP)
    top_val, top_idx = lax.top_k(fine_logits[rows, grp], TOP_K)
    weights = (p_grp[:, None] * jax.nn.softmax(top_val, axis=-1)).astype(h.dtype)
    expert = grp[:, None] * EXPERTS_PER_GROUP + top_idx
    n_assign = n_tok * TOP_K
    e_flat = expert.reshape(-1).astype(jnp.int32)
    tok_flat = jnp.repeat(jnp.arange(n_tok, dtype=jnp.int32), TOP_K)
    w_flat = weights.reshape(-1)
    order = jnp.argsort(e_flat)
    e_sorted = e_flat[order]
    counts = jnp.bincount(e_flat, length=N_EXPERTS)
    starts = jnp.cumsum(counts) - counts
    padded = ((counts + MOE_BLOCK - 1) // MOE_BLOCK) * MOE_BLOCK
    pad_ends = jnp.cumsum(padded)
    pad_starts = pad_ends - padded
    dest = pad_starts[e_sorted] + (jnp.arange(n_assign) - starts[e_sorted])
    cap = n_assign + N_EXPERTS * MOE_BLOCK
    n_blocks = cap // MOE_BLOCK
    slot_tok = jnp.full((cap,), n_tok, jnp.int32).at[dest].set(tok_flat[order])
    slot_w = jnp.zeros((cap,), h.dtype).at[dest].set(w_flat[order])
    blk_expert = jnp.minimum(jnp.searchsorted(pad_ends, jnp.arange(n_blocks) * MOE_BLOCK, side='right'),
                             N_EXPERTS - 1)
    h_pad = jnp.concatenate([h, jnp.zeros((1, d), h.dtype)], axis=0)
    xb = h_pad[slot_tok].reshape(n_blocks, MOE_BLOCK, d)

    def expert_block(args):
        xblk, e = args
        return (jax.nn.silu(xblk @ w_gate[e]) * (xblk @ w_up[e])) @ w_down[e]

    yb = lax.map(expert_block, (xb, blk_expert)).reshape(cap, d)
    y = jax.ops.segment_sum(yb * slot_w[:, None], slot_tok, num_segments=n_tok + 1)
    return y[:n_tok]


def setup_inputs(seed: int = 0) -> dict:
    key = jax.random.key(seed)
    ks = jax.random.split(key, 24)
    f32 = jnp.float32
    nrm = lambda k, shape, scale: jax.random.normal(k, shape, f32) * scale
    return {
        'x': nrm(ks[0], (BATCH, SEQ, D_MODEL), 1.0),
        'c': nrm(ks[1], (BATCH, D_MODEL), 1.0),
        'ada_w': nrm(ks[2], (DEPTH, D_MODEL, 6 * D_MODEL), 0.5 * D_MODEL ** -0.5),
        'ada_b': nrm(ks[3], (DEPTH, 6 * D_MODEL), 0.02),
        'norm1_w': 1.0 + nrm(ks[4], (DEPTH, D_MODEL), 0.02),
        'norm2_w': 1.0 + nrm(ks[5], (DEPTH, D_MODEL), 0.02),
        'w_in': nrm(ks[6], (DEPTH, D_MODEL, IN_COLS), D_MODEL ** -0.5),
        'conv_q': nrm(ks[7], (DEPTH, CONV_W, M_WIDTH), CONV_W ** -0.5),
        'conv_k': nrm(ks[8], (DEPTH, CONV_W, M_WIDTH), CONV_W ** -0.5),
        'igate_b': nrm(ks[9], (DEPTH, M_HEADS), 0.1),
        'fgate_b': jnp.linspace(3.0, 6.0, M_HEADS, dtype=f32)[None, :] + nrm(ks[10], (DEPTH, M_HEADS), 0.1),
        'rel_bias': nrm(ks[11], (DEPTH, A_HEADS, 2 * REL_CLIP + 1), 0.5),
        'mlstm_norm_w': 1.0 + nrm(ks[12], (DEPTH, M_WIDTH), 0.02),
        'w_branch_attn': nrm(ks[13], (DEPTH, A_WIDTH, D_MODEL), A_WIDTH ** -0.5),
        'w_branch_mlstm': nrm(ks[14], (DEPTH, M_WIDTH, D_MODEL), M_WIDTH ** -0.5),
        'w_out': nrm(ks[15], (DEPTH, D_MODEL, D_MODEL), D_MODEL ** -0.5),
        'router_coarse_w': nrm(ks[16], (DEPTH, D_MODEL, N_GROUPS), D_MODEL ** -0.5),
        'router_coarse_b': nrm(ks[17], (DEPTH, N_GROUPS), 0.01),
        'router_fine_w': nrm(ks[18], (DEPTH, D_MODEL, N_EXPERTS), D_MODEL ** -0.5),
        'router_fine_b': nrm(ks[19], (DEPTH, N_EXPERTS), 0.01),
        'w_gate': nrm(ks[20], (DEPTH, N_EXPERTS, D_MODEL, D_EXPERT), D_MODEL ** -0.5),
        'w_up': nrm(ks[21], (DEPTH, N_EXPERTS, D_MODEL, D_EXPERT), D_MODEL ** -0.5),
        'w_down': nrm(ks[22], (DEPTH, N_EXPERTS, D_EXPERT, D_MODEL), D_EXPERT ** -0.5),
        'final_norm_w': 1.0 + nrm(ks[23], (D_MODEL,), 0.02),
    }


def reference(x, c, ada_w, ada_b, norm1_w, norm2_w, w_in, conv_q, conv_k, igate_b, fgate_b, rel_bias,
              mlstm_norm_w, w_branch_attn, w_branch_mlstm, w_out, router_coarse_w, router_coarse_b,
              router_fine_w, router_fine_b, w_gate, w_up, w_down, final_norm_w):
    b, s, d = x.shape
    for l in range(DEPTH):
        mod = (c @ ada_w[l] + ada_b[l])[:, None, :]
        sh1, sc1, g1, sh2, sc2, g2 = jnp.split(mod, 6, axis=-1)
        h = _rms(x, norm1_w[l]) * (1.0 + sc1) + sh1
        x = x + g1 * _mixer(h, w_in[l], conv_q[l], conv_k[l], igate_b[l], fgate_b[l], rel_bias[l],
                            mlstm_norm_w[l], w_branch_attn[l], w_branch_mlstm[l], w_out[l])
        h = _rms(x, norm2_w[l]) * (1.0 + sc2) + sh2
        y = _hier_moe(h.reshape(b * s, d), router_coarse_w[l], router_coarse_b[l], router_fine_w[l],
                      router_fine_b[l], w_gate[l], w_up[l], w_down[l])
        x = x + g2 * y.reshape(b, s, d)
    return _rms(x, final_norm_w)
```

```python
import functools

import jax
import jax.numpy as jnp
from jax import lax
from jax.experimental import pallas as pl
from jax.experimental.pallas import tpu as pltpu

F32 = jnp.float32
BF16 = jnp.bfloat16

EPS = 1e-6
NEG_INF = -1e30
CHUNK = 64
LEFT_CHUNKS = 8
REL_CLIP = 256
A_HEADS = 8
A_HEAD_DIM = 128
M_HEADS = 4
M_HEAD_DIM = 256
CONV_W = 4
N_GROUPS = 4
EXPERTS_PER_GROUP = 8
N_EXPERTS = N_GROUPS * EXPERTS_PER_GROUP
TOP_K = 2

LANES = 128
VMEM_LIMIT = 60 * 1024 * 1024

ATT_QBLK = 256
ATT_KBLKS = 3
MOE_BLK = 128


def _cparams(sem):
    return pltpu.CompilerParams(dimension_semantics=sem, vmem_limit_bytes=VMEM_LIMIT)


def _sigmoid(t):
    return 1.0 / (1.0 + jnp.exp(-t))


def _silu(t):
    return t * _sigmoid(t)


def _ada_kernel(c_ref, w_ref, b_ref, o_ref):
    w = w_ref[...].astype(BF16)
    r = jnp.dot(c_ref[...], w, preferred_element_type=F32)
    bp = o_ref.shape[0]
    o_ref[...] = r[:bp] + r[bp:] + b_ref[...]


def _ada_mod(c, ada_w, ada_b):
    depth, d, n6 = ada_w.shape
    b = c.shape[0]
    bp = 8
    c_pad = jnp.zeros((bp, d), F32).at[:b].set(c)
    c_hi = c_pad.astype(BF16)
    c_lo = (c_pad - c_hi.astype(F32)).astype(BF16)
    c2 = jnp.concatenate([c_hi, c_lo], axis=0)
    tn = 1024
    out = pl.pallas_call(
        _ada_kernel,
        grid=(depth, n6 // tn),
        in_specs=[
            pl.BlockSpec((2 * bp, d), lambda l, j: (0, 0)),
            pl.BlockSpec((None, d, tn), lambda l, j: (l, 0, j)),
            pl.BlockSpec((None, 1, tn), lambda l, j: (l, 0, j)),
        ],
        out_specs=pl.BlockSpec((None, bp, tn), lambda l, j: (l, 0, j)),
        out_shape=jax.ShapeDtypeStruct((depth, bp, n6), F32),
        name="ada_mod",
        compiler_params=_cparams(("arbitrary", "arbitrary")),
    )(c2, ada_w, ada_b.reshape(depth, 1, n6))
    return out[:, :b].reshape(depth, b, 6, d)


def _norm_kernel(x_ref, nw_ref, mod_ref, ws_ref, bs_ref, h_ref, s_ref, *, shift_row, scale_row, precise):
    x = x_ref[...]
    y = x * lax.rsqrt(jnp.mean(x * x, axis=-1, keepdims=True) + EPS)
    y = y * nw_ref[...]
    h = y * (1.0 + mod_ref[scale_row:scale_row + 1, :]) + mod_ref[shift_row:shift_row + 1, :]
    h_ref[...] = h.astype(h_ref.dtype)
    if precise:
        s = jnp.dot(h, ws_ref[...], preferred_element_type=F32, precision=lax.Precision.HIGHEST)
    else:
        s = jnp.dot(h.astype(BF16), ws_ref[...].astype(BF16), preferred_element_type=F32)
    s_ref[...] = s + bs_ref[...]


def _norm_mod(x2, norm_w, mod_l, w_side, b_side, *, seq, shift_row, scale_row, precise, tm=256):
    n, d = x2.shape
    blocks_per_batch = seq // tm
    kern = functools.partial(_norm_kernel, shift_row=shift_row, scale_row=scale_row, precise=precise)
    return pl.pallas_call(
        kern,
        grid=(n // tm,),
        in_specs=[
            pl.BlockSpec((tm, d), lambda i: (i, 0)),
            pl.BlockSpec((1, d), lambda i: (0, 0)),
            pl.BlockSpec((None, 6, d), lambda i: (i // blocks_per_batch, 0, 0)),
            pl.BlockSpec((d, LANES), lambda i: (0, 0)),
            pl.BlockSpec((1, LANES), lambda i: (0, 0)),
        ],
        out_specs=[
            pl.BlockSpec((tm, d), lambda i: (i, 0)),
            pl.BlockSpec((tm, LANES), lambda i: (i, 0)),
        ],
        out_shape=[jax.ShapeDtypeStruct((n, d), BF16), jax.ShapeDtypeStruct((n, LANES), F32)],
        name="norm_mod",
        compiler_params=_cparams(("arbitrary",)),
    )(x2, norm_w.reshape(1, d), mod_l, w_side, b_side)


def _final_norm_kernel(x_ref, nw_ref, o_ref):
    x = x_ref[...]
    y = x * lax.rsqrt(jnp.mean(x * x, axis=-1, keepdims=True) + EPS)
    o_ref[...] = y * nw_ref[...]


def _final_norm(x2, norm_w, tm=256):
    n, d = x2.shape
    return pl.pallas_call(
        _final_norm_kernel,
        grid=(n // tm,),
        in_specs=[pl.BlockSpec((tm, d), lambda i: (i, 0)), pl.BlockSpec((1, d), lambda i: (0, 0))],
        out_specs=pl.BlockSpec((tm, d), lambda i: (i, 0)),
        out_shape=jax.ShapeDtypeStruct((n, d), F32),
        name="final_norm",
        compiler_params=_cparams(("arbitrary",)),
    )(x2, norm_w.reshape(1, d))


def _proj_kernel(a_ref, w_ref, o_ref, wb_ref):
    @pl.when(pl.program_id(1) == 0)
    def _():
        wb_ref[...] = w_ref[...].astype(BF16)

    o_ref[...] = jnp.dot(a_ref[...], wb_ref[...], preferred_element_type=F32).astype(o_ref.dtype)


def _proj(a, w_stack, layer, n_cols, tm=1024, tn=1024):
    m, k = a.shape
    return pl.pallas_call(
        _proj_kernel,
        grid=(n_cols // tn, m // tm),
        in_specs=[
            pl.BlockSpec((tm, k), lambda j, i: (i, 0)),
            pl.BlockSpec((None, k, tn), lambda j, i: (layer, 0, j)),
        ],
        out_specs=pl.BlockSpec((tm, tn), lambda j, i: (i, j)),
        out_shape=jax.ShapeDtypeStruct((m, n_cols), BF16),
        scratch_shapes=[pltpu.VMEM((k, tn), BF16)],
        name="in_proj",
        compiler_params=_cparams(("arbitrary", "arbitrary")),
    )(a, w_stack)


def _attn_kernel(q_ref, k0_ref, k1_ref, k2_ref, v0_ref, v1_ref, v2_ref, bias_ref, o_ref):
    i = pl.program_id(1)
    k_refs = (k0_ref, k1_ref, k2_ref)
    v_refs = (v0_ref, v1_ref, v2_ref)
    qb = q_ref.shape[0]
    kw = ATT_KBLKS * qb
    col = lax.broadcasted_iota(jnp.int32, (qb, kw), 1)
    key_ok = col >= (ATT_KBLKS - 1 - i) * qb
    scale = A_HEAD_DIM ** -0.5
    for h in range(A_HEADS):
        sl = slice(h * A_HEAD_DIM, (h + 1) * A_HEAD_DIM)
        q = q_ref[:, sl]
        parts = [lax.dot_general(q, kr[:, sl], (((1,), (1,)), ((), ())), preferred_element_type=F32)
                 for kr in k_refs]
        s = jnp.concatenate(parts, axis=1) * scale + bias_ref[h]
        s = jnp.where(key_ok, s, NEG_INF)
        m = jnp.max(s, axis=-1, keepdims=True)
        e = jnp.exp(s - m)
        p = (e / jnp.sum(e, axis=-1, keepdims=True)).astype(BF16)
        acc = jnp.dot(p[:, :qb], v_refs[0][:, sl], preferred_element_type=F32)
        for j in range(1, ATT_KBLKS):
            acc = acc + jnp.dot(p[:, j * qb:(j + 1) * qb], v_refs[j][:, sl], preferred_element_type=F32)
        o_ref[:, sl] = acc.astype(o_ref.dtype)


def _attn_bias(rel_table):
    qb, kw = ATT_QBLK, ATT_KBLKS * ATT_QBLK
    qi = jnp.arange(qb)[:, None]
    kj = jnp.arange(kw)[None, :]
    dist = qi + (ATT_KBLKS - 1) * qb - kj
    bias = rel_table[:, jnp.clip(dist, -REL_CLIP, REL_CLIP) + REL_CLIP].astype(F32)
    qc = qi // CHUNK + (ATT_KBLKS - 1) * (qb // CHUNK)
    kc = kj // CHUNK
    band = (kc <= qc) & (kc >= qc - LEFT_CHUNKS)
    return jnp.where(band[None], bias, NEG_INF)


def _attention(p_all, bias, batch, seq):
    n = p_all.shape[0]
    width = A_HEADS * A_HEAD_DIM
    qb = ATT_QBLK
    nb = seq // qb

    def kv_spec(back, colblk):
        return pl.BlockSpec((qb, width), lambda b, i: (b * nb + jnp.maximum(i - back, 0), colblk))

    return pl.pallas_call(
        _attn_kernel,
        grid=(batch, nb),
        in_specs=[
            pl.BlockSpec((qb, width), lambda b, i: (b * nb + i, 0)),
            kv_spec(2, 1), kv_spec(1, 1), kv_spec(0, 1),
            kv_spec(2, 2), kv_spec(1, 2), kv_spec(0, 2),
            pl.BlockSpec((A_HEADS, qb, ATT_KBLKS * qb), lambda b, i: (0, 0, 0)),
        ],
        out_specs=pl.BlockSpec((qb, width), lambda b, i: (b * nb + i, 0)),
        out_shape=jax.ShapeDtypeStruct((n, width), BF16),
        name="chunk_attn",
        compiler_params=_cparams(("arbitrary", "arbitrary")),
    )(p_all, p_all, p_all, p_all, p_all, p_all, p_all, bias)


def _log_sigmoid(t):
    return jnp.minimum(t, 0.0) - jnp.log(1.0 + jnp.exp(-jnp.abs(t)))


def _mlstm_kernel(q_ref, k_ref, v_ref, o_ref, g_ref, cq_ref, ck_ref, nw_ref, out_ref,
                  qbuf, kbuf, ct_ref, n_ref, m_ref):
    c = pl.program_id(1)
    L, D = CHUNK, M_HEAD_DIM
    tail = 8

    @pl.when(c == 0)
    def _():
        qbuf[0:tail, :] = jnp.zeros((tail, qbuf.shape[1]), F32)
        kbuf[0:tail, :] = jnp.zeros((tail, kbuf.shape[1]), F32)
        ct_ref[...] = jnp.zeros(ct_ref.shape, F32)
        n_ref[...] = jnp.zeros(n_ref.shape, F32)
        m_ref[...] = jnp.zeros(m_ref.shape, F32)

    qbuf[tail:tail + L, :] = q_ref[...].astype(F32)
    kbuf[tail:tail + L, :] = k_ref[...].astype(F32)

    def conv(buf, w_ref):
        acc = None
        for j in range(CONV_W):
            off = tail - (CONV_W - 1) + j
            term = buf[off:off + L, :] * w_ref[j:j + 1, :]
            acc = term if acc is None else acc + term
        return acc

    q_all = _silu(conv(qbuf, cq_ref)) * (D ** -0.5)
    k_all = _silu(conv(kbuf, ck_ref))
    qbuf[0:tail, :] = qbuf[L:L + tail, :]
    kbuf[0:tail, :] = kbuf[L:L + tail, :]

    row = lax.broadcasted_iota(jnp.int32, (L, L), 0)
    colm = lax.broadcasted_iota(jnp.int32, (L, L), 1)
    causal = colm <= row
    eye = colm == row
    upper = (row <= colm).astype(F32)

    def to_col(r):
        return jnp.sum(jnp.where(eye, jnp.broadcast_to(r, (L, L)), 0.0), axis=1, keepdims=True)

    for h in range(M_HEADS):
        sl = slice(h * D, (h + 1) * D)
        q = q_all[:, sl]
        k = k_all[:, sl]
        qb16 = q.astype(BF16)
        kb16 = k.astype(BF16)
        vb16 = v_ref[:, sl]
        ig = g_ref[h:h + 1, :]
        lf = _log_sigmoid(g_ref[M_HEADS + h:M_HEADS + h + 1, :])
        bcum = jnp.dot(jnp.broadcast_to(lf, (8, L)), upper, preferred_element_type=F32,
                       precision=lax.Precision.HIGHEST)[0:1, :]
        bcum_c = to_col(bcum)
        m_prev = m_ref[h]
        logd = jnp.where(causal, bcum_c - bcum + ig, NEG_INF)
        inter = bcum_c + m_prev
        m_s = jnp.maximum(jnp.max(logd, axis=-1, keepdims=True), inter)
        s = lax.dot_general(qb16, kb16, (((1,), (1,)), ((), ())), preferred_element_type=F32)
        w_intra = s * jnp.exp(logd - m_s)
        w_inter = jnp.exp(inter - m_s)
        ct = ct_ref[h]
        n_row = n_ref[h]
        num = (jnp.dot(w_intra.astype(BF16), vb16, preferred_element_type=F32)
               + w_inter * jnp.dot(qb16, ct.astype(BF16), preferred_element_type=F32))
        den = (jnp.sum(w_intra, axis=-1, keepdims=True)
               + w_inter * jnp.sum(q * n_row, axis=-1, keepdims=True))
        hs = num / jnp.maximum(jnp.abs(den), jnp.exp(-m_s))
        b_last = bcum[:, L - 1:L]
        log_wk = b_last - bcum + ig
        m_new = jnp.maximum(b_last + m_prev, jnp.max(log_wk, axis=-1, keepdims=True))
        wk = jnp.exp(log_wk - m_new)
        decay = jnp.exp(b_last + m_prev - m_new)
        kw = k * to_col(wk)
        ct_ref[h] = decay * ct + lax.dot_general(kw.astype(BF16), vb16, (((0,), (0,)), ((), ())),
                                                 preferred_element_type=F32)
        n_ref[h] = decay * n_row + jnp.sum(kw, axis=0, keepdims=True)
        m_ref[h] = m_new
        hm = _sigmoid(o_ref[:, sl].astype(F32)) * hs
        y = hm * lax.rsqrt(jnp.mean(hm * hm, axis=-1, keepdims=True) + EPS) * nw_ref[:, sl]
        out_ref[:, sl] = y.astype(out_ref.dtype)


def _mlstm(p_all, gates_t, conv_q, conv_k, norm_w, layer, batch, seq):
    n = p_all.shape[0]
    width = M_HEADS * M_HEAD_DIM
    nc = seq // CHUNK
    L = CHUNK

    def p_spec(colblk):
        return pl.BlockSpec((L, width), lambda b, c: (b * nc + c, colblk))

    return pl.pallas_call(
        _mlstm_kernel,
        grid=(batch, nc),
        in_specs=[
            p_spec(3), p_spec(4), p_spec(5), p_spec(6),
            pl.BlockSpec((None, None, 2 * M_HEADS, L), lambda b, c: (b, c, 0, 0)),
            pl.BlockSpec((None, CONV_W, width), lambda b, c: (layer, 0, 0)),
            pl.BlockSpec((None, CONV_W, width), lambda b, c: (layer, 0, 0)),
            pl.BlockSpec((1, width), lambda b, c: (0, 0)),
        ],
        out_specs=pl.BlockSpec((L, width), lambda b, c: (b * nc + c, 0)),
        out_shape=jax.ShapeDtypeStruct((n, width), BF16),
        scratch_shapes=[
            pltpu.VMEM((L + 8, width), F32),
            pltpu.VMEM((L + 8, width), F32),
            pltpu.VMEM((M_HEADS, M_HEAD_DIM, M_HEAD_DIM), F32),
            pltpu.VMEM((M_HEADS, 1, M_HEAD_DIM), F32),
            pltpu.VMEM((M_HEADS, 1, 1), F32),
        ],
        name="mlstm",
        compiler_params=_cparams(("arbitrary", "arbitrary")),
    )(p_all, p_all, p_all, p_all, gates_t, conv_q, conv_k, norm_w.reshape(1, width))


def _merge_kernel(ya_ref, hm_ref, ga_ref, gm_ref, wa_ref, wm_ref, o_ref, wab_ref, wmb_ref):
    @pl.when(pl.program_id(1) == 0)
    def _():
        wab_ref[...] = wa_ref[...].astype(BF16)
        wmb_ref[...] = wm_ref[...].astype(BF16)

    a = jnp.dot(ya_ref[...], wab_ref[...], preferred_element_type=F32)
    m = jnp.dot(hm_ref[...], wmb_ref[...], preferred_element_type=F32)
    out = _sigmoid(ga_ref[...].astype(F32)) * a + _sigmoid(gm_ref[...].astype(F32)) * m
    o_ref[...] = out.astype(o_ref.dtype)


def _merge(y_attn, h_m, p_all, w_ba, w_bm, layer, ga_col0, gm_col0, tm=1024, tn=1024):
    n, ka = y_attn.shape
    km = h_m.shape[1]
    d = w_ba.shape[2]
    return pl.pallas_call(
        _merge_kernel,
        grid=(d // tn, n // tm),
        in_specs=[
            pl.BlockSpec((tm, ka), lambda j, i: (i, 0)),
            pl.BlockSpec((tm, km), lambda j, i: (i, 0)),
            pl.BlockSpec((tm, tn), lambda j, i: (i, ga_col0 // tn + j)),
            pl.BlockSpec((tm, tn), lambda j, i: (i, gm_col0 // tn + j)),
            pl.BlockSpec((None, ka, tn), lambda j, i: (layer, 0, j)),
            pl.BlockSpec((None, km, tn), lambda j, i: (layer, 0, j)),
        ],
        out_specs=pl.BlockSpec((tm, tn), lambda j, i: (i, j)),
        out_shape=jax.ShapeDtypeStruct((n, d), BF16),
        scratch_shapes=[pltpu.VMEM((ka, tn), BF16), pltpu.VMEM((km, tn), BF16)],
        name="branch_merge",
        compiler_params=_cparams(("arbitrary", "arbitrary")),
    )(y_attn, h_m, p_all, p_all, w_ba, w_bm)


def _out_kernel(a_ref, w_ref, x_ref, mod_ref, o_ref, wb_ref, *, gate_row):
    @pl.when(pl.program_id(1) == 0)
    def _():
        wb_ref[...] = w_ref[...].astype(BF16)

    y = jnp.dot(a_ref[...], wb_ref[...], preferred_element_type=F32)
    o_ref[...] = x_ref[...] + mod_ref[gate_row:gate_row + 1, :] * y


def _out_proj(a, w_stack, layer, x2, mod_l, seq, gate_row, tm=1024, tn=1024):
    n, k = a.shape
    d = x2.shape[1]
    tm = min(tm, seq)
    blocks_per_batch = seq // tm
    return pl.pallas_call(
        functools.partial(_out_kernel, gate_row=gate_row),
        grid=(d // tn, n // tm),
        in_specs=[
            pl.BlockSpec((tm, k), lambda j, i: (i, 0)),
            pl.BlockSpec((None, k, tn), lambda j, i: (layer, 0, j)),
            pl.BlockSpec((tm, tn), lambda j, i: (i, j)),
            pl.BlockSpec((None, 6, tn), lambda j, i: (i // blocks_per_batch, 0, j)),
        ],
        out_specs=pl.BlockSpec((tm, tn), lambda j, i: (i, j)),
        out_shape=jax.ShapeDtypeStruct((n, d), F32),
        scratch_shapes=[pltpu.VMEM((k, tn), BF16)],
        name="out_proj",
        compiler_params=_cparams(("arbitrary", "arbitrary")),
    )(a, w_stack, x2, mod_l)


def _moe_kernel(blk_e_ref, nact_ref, x_ref, sw_ref, wg_ref, wu_ref, wd_ref, o_ref):
    i = pl.program_id(0)

    @pl.when(i < nact_ref[0])
    def _():
        x = x_ref[...]
        g = jnp.dot(x, wg_ref[...].astype(BF16), preferred_element_type=F32)
        u = jnp.dot(x, wu_ref[...].astype(BF16), preferred_element_type=F32)
        a = (_silu(g) * u).astype(BF16)
        y = jnp.dot(a, wd_ref[...].astype(BF16), preferred_element_type=F32)
        o_ref[...] = y * sw_ref[...]

    @pl.when(i >= nact_ref[0])
    def _():
        o_ref[...] = jnp.zeros(o_ref.shape, o_ref.dtype)


def _moe_experts(xb, slot_w, blk_expert, n_active, w_gate, w_up, w_down, layer):
    cap, d = xb.shape
    f = w_gate.shape[3]
    n_blocks = cap // MOE_BLK
    grid_spec = pltpu.PrefetchScalarGridSpec(
        num_scalar_prefetch=2,
        grid=(n_blocks,),
        in_specs=[
            pl.BlockSpec((MOE_BLK, d), lambda i, be, na: (i, 0)),
            pl.BlockSpec((MOE_BLK, 1), lambda i, be, na: (i, 0)),
            pl.BlockSpec((None, None, d, f), lambda i, be, na: (layer, be[i], 0, 0)),
            pl.BlockSpec((None, None, d, f), lambda i, be, na: (layer, be[i], 0, 0)),
            pl.BlockSpec((None, None, f, d), lambda i, be, na: (layer, be[i], 0, 0)),
        ],
        out_specs=pl.BlockSpec((MOE_BLK, d), lambda i, be, na: (i, 0)),
    )
    return pl.pallas_call(
        _moe_kernel,
        grid_spec=grid_spec,
        out_shape=jax.ShapeDtypeStruct((cap, d), F32),
        name="moe_experts",
        compiler_params=_cparams(("arbitrary",)),
    )(blk_expert, n_active, xb, slot_w.reshape(cap, 1), w_gate, w_up, w_down)


def _route(logits):
    n_tok = logits.shape[0]
    rows = jnp.arange(n_tok)
    coarse = logits[:, :N_GROUPS]
    grp = jnp.argmax(coarse, axis=-1)
    p_grp = jax.nn.softmax(coarse, axis=-1)[rows, grp]
    fine = logits[:, N_GROUPS:N_GROUPS + N_EXPERTS].reshape(n_tok, N_GROUPS, EXPERTS_PER_GROUP)
    top_val, top_idx = lax.top_k(fine[rows, grp], TOP_K)
    weights = p_grp[:, None] * jax.nn.softmax(top_val, axis=-1)
    expert = grp[:, None] * EXPERTS_PER_GROUP + top_idx
    return expert.astype(jnp.int32), weights


def _dispatch(expert, weights):
    n_tok = expert.shape[0]
    n_assign = n_tok * TOP_K
    e_flat = expert.reshape(-1)
    tok_flat = jnp.repeat(jnp.arange(n_tok, dtype=jnp.int32), TOP_K)
    w_flat = weights.reshape(-1)
    order = jnp.argsort(e_flat)
    e_sorted = e_flat[order]
    counts = jnp.bincount(e_flat, length=N_EXPERTS)
    starts = jnp.cumsum(counts) - counts
    padded = ((counts + MOE_BLK - 1) // MOE_BLK) * MOE_BLK
    pad_ends = jnp.cumsum(padded)
    pad_starts = pad_ends - padded
    dest = (pad_starts[e_sorted] + (jnp.arange(n_assign) - starts[e_sorted])).astype(jnp.int32)
    cap = n_assign + N_EXPERTS * MOE_BLK
    n_blocks = cap // MOE_BLK
    slot_tok = jnp.full((cap,), n_tok, jnp.int32).at[dest].set(tok_flat[order])
    slot_w = jnp.zeros((cap,), F32).at[dest].set(w_flat[order])
    blk_expert = jnp.minimum(jnp.searchsorted(pad_ends, jnp.arange(n_blocks) * MOE_BLK, side='right'),
                             N_EXPERTS - 1).astype(jnp.int32)
    n_active = (pad_ends[-1] // MOE_BLK).astype(jnp.int32).reshape(1)
    pos = jnp.zeros((n_assign,), jnp.int32).at[order].set(dest).reshape(n_tok, TOP_K)
    return slot_tok, slot_w, blk_expert, n_active, pos


def kernel(x, c, ada_w, ada_b, norm1_w, norm2_w, w_in, conv_q, conv_k, igate_b, fgate_b, rel_bias,
           mlstm_norm_w, w_branch_attn, w_branch_mlstm, w_out, router_coarse_w, router_coarse_b,
           router_fine_w, router_fine_b, w_gate, w_up, w_down, final_norm_w):
    b, s, d = x.shape
    depth = ada_w.shape[0]
    n = b * s
    nc = s // CHUNK
    a_width = A_HEADS * A_HEAD_DIM
    m_width = M_HEADS * M_HEAD_DIM
    main_cols = 3 * a_width + 4 * m_width + 2 * d
    ga_col0 = 3 * a_width + 4 * m_width
    gm_col0 = ga_col0 + d

    mod = _ada_mod(c, ada_w, ada_b)
    x2 = x.reshape(n, d)

    for l in range(depth):
        w_g = jnp.zeros((d, LANES), F32).at[:, :2 * M_HEADS].set(w_in[l, :, main_cols:])
        b_g = jnp.zeros((1, LANES), F32).at[0, :M_HEADS].set(igate_b[l]).at[0, M_HEADS:2 * M_HEADS].set(fgate_b[l])
        w_r = (jnp.zeros((d, LANES), F32).at[:, :N_GROUPS].set(router_coarse_w[l])
               .at[:, N_GROUPS:N_GROUPS + N_EXPERTS].set(router_fine_w[l]))
        b_r = (jnp.zeros((1, LANES), F32).at[0, :N_GROUPS].set(router_coarse_b[l])
               .at[0, N_GROUPS:N_GROUPS + N_EXPERTS].set(router_fine_b[l]))

        h, gates = _norm_mod(x2, norm1_w[l], mod[l], w_g, b_g, seq=s, shift_row=0, scale_row=1, precise=False)
        p_all = _proj(h, w_in, l, main_cols)
        y_attn = _attention(p_all, _attn_bias(rel_bias[l]), b, s)
        gates_t = gates[:, :2 * M_HEADS].reshape(b, nc, CHUNK, 2 * M_HEADS).transpose(0, 1, 3, 2)
        h_m = _mlstm(p_all, gates_t, conv_q, conv_k, mlstm_norm_w[l], l, b, s)
        merged = _merge(y_attn, h_m, p_all, w_branch_attn, w_branch_mlstm, l, ga_col0, gm_col0)
        x2 = _out_proj(merged, w_out, l, x2, mod[l], s, gate_row=2)

        h2, logits = _norm_mod(x2, norm2_w[l], mod[l], w_r, b_r, seq=s, shift_row=3, scale_row=4, precise=True)
        expert, weights = _route(logits)
        slot_tok, slot_w, blk_expert, n_active, pos = _dispatch(expert, weights)
        h_pad = jnp.concatenate([h2, jnp.zeros((1, d), h2.dtype)], axis=0)
        xb = h_pad[slot_tok]
        yb = _moe_experts(xb, slot_w, blk_expert, n_active, w_gate, w_up, w_down, l)
        y = yb[pos[:, 0]] + yb[pos[:, 1]]
        g2 = jnp.repeat(mod[l, :, 5, :], s, axis=0)
        x2 = x2 + g2 * y

    return _final_norm(x2, final_norm_w).reshape(b, s, d)
```

```python
import functools

import jax
import jax.numpy as jnp
from jax import lax
from jax.experimental import pallas as pl
from jax.experimental.pallas import tpu as pltpu

F32 = jnp.float32
BF16 = jnp.bfloat16

EPS = 1e-6
NEG_INF = -1e30
LOG2E = 1.4426950408889634
CHUNK = 64
LEFT_CHUNKS = 8
REL_CLIP = 256
A_HEADS = 8
A_HEAD_DIM = 128
M_HEADS = 4
M_HEAD_DIM = 256
CONV_W = 4
N_GROUPS = 4
EXPERTS_PER_GROUP = 8
N_EXPERTS = N_GROUPS * EXPERTS_PER_GROUP
TOP_K = 2

LANES = 128
VMEM_LIMIT = 60 * 1024 * 1024

ATT_QBLK = 256
ATT_KBLKS = 3
MLSTM_BATCH = 2
MOE_BLK = 256


def _cparams(sem):
    return pltpu.CompilerParams(dimension_semantics=sem, vmem_limit_bytes=VMEM_LIMIT)


def _sigmoid(t):
    return 1.0 / (1.0 + jnp.exp(-t))


def _silu(t):
    return t * _sigmoid(t)


def _ada_kernel(c_ref, w_ref, b_ref, o_ref):
    w = w_ref[...].astype(BF16)
    r = jnp.dot(c_ref[...], w, preferred_element_type=F32)
    bp = o_ref.shape[0]
    o_ref[...] = r[:bp] + r[bp:] + b_ref[...]


def _ada_mod(c, ada_w, ada_b):
    depth, d, n6 = ada_w.shape
    b = c.shape[0]
    bp = 8
    c_pad = jnp.zeros((bp, d), F32).at[:b].set(c)
    c_hi = c_pad.astype(BF16)
    c_lo = (c_pad - c_hi.astype(F32)).astype(BF16)
    c2 = jnp.concatenate([c_hi, c_lo], axis=0)
    tn = 1024
    out = pl.pallas_call(
        _ada_kernel,
        grid=(depth, n6 // tn),
        in_specs=[
            pl.BlockSpec((2 * bp, d), lambda l, j: (0, 0)),
            pl.BlockSpec((None, d, tn), lambda l, j: (l, 0, j)),
            pl.BlockSpec((None, 1, tn), lambda l, j: (l, 0, j)),
        ],
        out_specs=pl.BlockSpec((None, bp, tn), lambda l, j: (l, 0, j)),
        out_shape=jax.ShapeDtypeStruct((depth, bp, n6), F32),
        name="ada_mod",
        compiler_params=_cparams(("arbitrary", "arbitrary")),
    )(c2, ada_w, ada_b.reshape(depth, 1, n6))
    return out[:, :b].reshape(depth, b, 6, d)


def _norm_kernel(x_ref, nw_ref, mod_ref, ws_ref, bs_ref, h_ref, s_ref, *, shift_row, scale_row, precise,
                 side_cols):
    x = x_ref[...]
    y = x * lax.rsqrt(jnp.mean(x * x, axis=-1, keepdims=True) + EPS)
    y = y * nw_ref[...]
    h = y * (1.0 + mod_ref[scale_row:scale_row + 1, :]) + mod_ref[shift_row:shift_row + 1, :]
    h_ref[...] = h.astype(h_ref.dtype)
    lane = lax.broadcasted_iota(jnp.int32, ws_ref.shape, 1)
    ws = jnp.where(lane < side_cols, ws_ref[...], 0.0)
    if precise:
        s = jnp.dot(h, ws, preferred_element_type=F32, precision=lax.Precision.HIGHEST)
    else:
        s = jnp.dot(h.astype(BF16), ws.astype(BF16), preferred_element_type=F32)
    s_ref[...] = s + bs_ref[...]


def _norm_mod(x2, norm_w, mod_l, w_side, side_spec, side_cols, b_side, *, seq, shift_row, scale_row, precise,
              out_dtype, tm=256):
    n, d = x2.shape
    blocks_per_batch = seq // tm
    kern = functools.partial(_norm_kernel, shift_row=shift_row, scale_row=scale_row, precise=precise,
                             side_cols=side_cols)
    return pl.pallas_call(
        kern,
        grid=(n // tm,),
        in_specs=[
            pl.BlockSpec((tm, d), lambda i: (i, 0)),
            pl.BlockSpec((1, d), lambda i: (0, 0)),
            pl.BlockSpec((None, 6, d), lambda i: (i // blocks_per_batch, 0, 0)),
            side_spec,
            pl.BlockSpec((1, LANES), lambda i: (0, 0)),
        ],
        out_specs=[
            pl.BlockSpec((tm, d), lambda i: (i, 0)),
            pl.BlockSpec((tm, LANES), lambda i: (i, 0)),
        ],
        out_shape=[jax.ShapeDtypeStruct((n, d), out_dtype), jax.ShapeDtypeStruct((n, LANES), F32)],
        name="norm_mod",
        compiler_params=_cparams(("arbitrary",)),
    )(x2, norm_w.reshape(1, d), mod_l, w_side, b_side)


def _final_norm_kernel(x_ref, nw_ref, o_ref):
    x = x_ref[...]
    y = x * lax.rsqrt(jnp.mean(x * x, axis=-1, keepdims=True) + EPS)
    o_ref[...] = y * nw_ref[...]


def _final_norm(x2, norm_w, tm=256):
    n, d = x2.shape
    return pl.pallas_call(
        _final_norm_kernel,
        grid=(n // tm,),
        in_specs=[pl.BlockSpec((tm, d), lambda i: (i, 0)), pl.BlockSpec((1, d), lambda i: (0, 0))],
        out_specs=pl.BlockSpec((tm, d), lambda i: (i, 0)),
        out_shape=jax.ShapeDtypeStruct((n, d), F32),
        name="final_norm",
        compiler_params=_cparams(("arbitrary",)),
    )(x2, norm_w.reshape(1, d))


def _proj_kernel(a_ref, w_ref, o_ref, wb_ref):
    @pl.when(pl.program_id(1) == 0)
    def _():
        wb_ref[...] = w_ref[...].astype(BF16)

    o_ref[...] = jnp.dot(a_ref[...], wb_ref[...], preferred_element_type=F32).astype(o_ref.dtype)


def _proj(a, w_stack, layer, n_cols, tm=1024, tn=1024):
    m, k = a.shape
    return pl.pallas_call(
        _proj_kernel,
        grid=(n_cols // tn, m // tm),
        in_specs=[
            pl.BlockSpec((tm, k), lambda j, i: (i, 0)),
            pl.BlockSpec((None, k, tn), lambda j, i: (layer, 0, j)),
        ],
        out_specs=pl.BlockSpec((tm, tn), lambda j, i: (i, j)),
        out_shape=jax.ShapeDtypeStruct((m, n_cols), BF16),
        scratch_shapes=[pltpu.VMEM((k, tn), BF16)],
        name="in_proj",
        compiler_params=_cparams(("arbitrary", "arbitrary")),
    )(a, w_stack)


def _attn_kernel(q_ref, k0_ref, k1_ref, k2_ref, v0_ref, v1_ref, v2_ref, bias_ref, o_ref):
    k_refs = (k0_ref, k1_ref, k2_ref)
    v_refs = (v0_ref, v1_ref, v2_ref)
    qb = q_ref.shape[0]
    half = qb // 2
    scale2 = (A_HEAD_DIM ** -0.5) * LOG2E
    for h in range(A_HEADS):
        sl = slice(h * A_HEAD_DIM, (h + 1) * A_HEAD_DIM)
        for part in range(2):
            r0 = part * half
            c0 = part * half
            c1 = c0 + ATT_KBLKS * qb - half
            q = q_ref[r0:r0 + half, sl]
            pieces = []
            for j in range(ATT_KBLKS):
                lo, hi = max(c0, j * qb), min(c1, (j + 1) * qb)
                kblk = k_refs[j][lo - j * qb:hi - j * qb, sl]
                pieces.append(lax.dot_general(q, kblk, (((1,), (1,)), ((), ())), preferred_element_type=F32))
            s = jnp.concatenate(pieces, axis=1) * scale2 + bias_ref[h, r0:r0 + half, c0:c1]
            m = jnp.max(s, axis=-1, keepdims=True)
            e = jnp.exp2(s - m)
            denom = jnp.sum(e, axis=-1, keepdims=True)
            p = e.astype(BF16)
            acc = None
            off = 0
            for j in range(ATT_KBLKS):
                lo, hi = max(c0, j * qb), min(c1, (j + 1) * qb)
                term = jnp.dot(p[:, off:off + hi - lo], v_refs[j][lo - j * qb:hi - j * qb, sl],
                               preferred_element_type=F32)
                acc = term if acc is None else acc + term
                off += hi - lo
            o_ref[r0:r0 + half, sl] = (acc / denom).astype(o_ref.dtype)


def _attn_bias(rel_table):
    qb, kw = ATT_QBLK, ATT_KBLKS * ATT_QBLK
    qi = jnp.arange(qb)[:, None]
    kj = jnp.arange(kw)[None, :]
    dist = qi + (ATT_KBLKS - 1) * qb - kj
    bias = rel_table[:, jnp.clip(dist, -REL_CLIP, REL_CLIP) + REL_CLIP].astype(F32) * LOG2E
    qc = qi // CHUNK + (ATT_KBLKS - 1) * (qb // CHUNK)
    kc = kj // CHUNK
    band = (kc <= qc) & (kc >= qc - LEFT_CHUNKS)
    tables = []
    for t in range(ATT_KBLKS):
        ok = band & (kj >= (ATT_KBLKS - 1 - t) * qb)
        tables.append(jnp.where(ok[None], bias, NEG_INF))
    return jnp.stack(tables)


def _attention(p_all, bias, batch, seq):
    n = p_all.shape[0]
    width = A_HEADS * A_HEAD_DIM
    qb = ATT_QBLK
    nb = seq // qb

    def kv_spec(back, colblk):
        return pl.BlockSpec((qb, width), lambda i, b: (b * nb + jnp.maximum(i - back, 0), colblk))

    return pl.pallas_call(
        _attn_kernel,
        grid=(nb, batch),
        in_specs=[
            pl.BlockSpec((qb, width), lambda i, b: (b * nb + i, 0)),
            kv_spec(2, 1), kv_spec(1, 1), kv_spec(0, 1),
            kv_spec(2, 2), kv_spec(1, 2), kv_spec(0, 2),
            pl.BlockSpec((None, A_HEADS, qb, ATT_KBLKS * qb), lambda i, b: (jnp.minimum(i, ATT_KBLKS - 1), 0, 0, 0)),
        ],
        out_specs=pl.BlockSpec((qb, width), lambda i, b: (b * nb + i, 0)),
        out_shape=jax.ShapeDtypeStruct((n, width), BF16),
        name="chunk_attn",
        compiler_params=_cparams(("arbitrary", "arbitrary")),
    )(p_all, p_all, p_all, p_all, p_all, p_all, p_all, bias)


def _log_sigmoid(t):
    return jnp.minimum(t, 0.0) - jnp.log(1.0 + jnp.exp(-jnp.abs(t)))


def _mlstm_kernel(q_ref, k_ref, v_ref, o_ref, g_ref, cq_ref, ck_ref, nw_ref, out_ref,
                  qbuf, kbuf, ct_ref, n_ref, m_ref):
    c = pl.program_id(1)
    L, D = CHUNK, M_HEAD_DIM
    nb = q_ref.shape[0]
    tail = 8

    @pl.when(c == 0)
    def _():
        qbuf[:, 0:tail, :] = jnp.zeros((nb, tail, qbuf.shape[2]), F32)
        kbuf[:, 0:tail, :] = jnp.zeros((nb, tail, kbuf.shape[2]), F32)
        ct_ref[...] = jnp.zeros(ct_ref.shape, F32)
        n_ref[...] = jnp.zeros(n_ref.shape, F32)
        m_ref[...] = jnp.zeros(m_ref.shape, F32)

    row = lax.broadcasted_iota(jnp.int32, (L, L), 0)
    colm = lax.broadcasted_iota(jnp.int32, (L, L), 1)
    causal = colm <= row
    eye = colm == row
    upper = (row <= colm).astype(F32)

    def to_col(r):
        return jnp.sum(jnp.where(eye, jnp.broadcast_to(r, (L, L)), 0.0), axis=1, keepdims=True)

    def conv(buf, w_ref):
        acc = None
        for j in range(CONV_W):
            off = tail - (CONV_W - 1) + j
            term = buf[off:off + L, :] * w_ref[j:j + 1, :]
            acc = term if acc is None else acc + term
        return acc

    for bi in range(nb):
        qb_ref, kb_ref = qbuf.at[bi], kbuf.at[bi]
        qb_ref[tail:tail + L, :] = q_ref[bi].astype(F32)
        kb_ref[tail:tail + L, :] = k_ref[bi].astype(F32)
        q_all = _silu(conv(qb_ref, cq_ref)) * (D ** -0.5)
        k_all = _silu(conv(kb_ref, ck_ref))
        qb_ref[0:tail, :] = qb_ref[L:L + tail, :]
        kb_ref[0:tail, :] = kb_ref[L:L + tail, :]

        for h in range(M_HEADS):
            sl = slice(h * D, (h + 1) * D)
            q = q_all[:, sl]
            k = k_all[:, sl]
            qb16 = q.astype(BF16)
            kb16 = k.astype(BF16)
            vb16 = v_ref[bi, :, sl]
            ig = g_ref[bi, h:h + 1, :]
            lf = _log_sigmoid(g_ref[bi, M_HEADS + h:M_HEADS + h + 1, :])
            bcum = jnp.dot(jnp.broadcast_to(lf, (8, L)), upper, preferred_element_type=F32,
                           precision=lax.Precision.HIGHEST)[0:1, :]
            bcum_c = to_col(bcum)
            m_prev = m_ref[bi, h]
            logd = jnp.where(causal, bcum_c - bcum + ig, NEG_INF)
            inter = bcum_c + m_prev
            m_s = jnp.maximum(jnp.max(logd, axis=-1, keepdims=True), inter)
            s = lax.dot_general(qb16, kb16, (((1,), (1,)), ((), ())), preferred_element_type=F32)
            w_intra = s * jnp.exp(logd - m_s)
            w_inter = jnp.exp(inter - m_s)
            ct = ct_ref[bi, h]
            n_row = n_ref[bi, h]
            num = (jnp.dot(w_intra.astype(BF16), vb16, preferred_element_type=F32)
                   + w_inter * jnp.dot(qb16, ct.astype(BF16), preferred_element_type=F32))
            den = (jnp.sum(w_intra, axis=-1, keepdims=True)
                   + w_inter * jnp.sum(q * n_row, axis=-1, keepdims=True))
            hs = num / jnp.maximum(jnp.abs(den), jnp.exp(-m_s))
            b_last = bcum[:, L - 1:L]
            log_wk = b_last - bcum + ig
            m_new = jnp.maximum(b_last + m_prev, jnp.max(log_wk, axis=-1, keepdims=True))
            wk = jnp.exp(log_wk - m_new)
            decay = jnp.exp(b_last + m_prev - m_new)
            kw = k * to_col(wk)
            ct_ref[bi, h] = decay * ct + lax.dot_general(kw.astype(BF16), vb16, (((0,), (0,)), ((), ())),
                                                         preferred_element_type=F32)
            n_ref[bi, h] = decay * n_row + jnp.sum(kw, axis=0, keepdims=True)
            m_ref[bi, h] = m_new
            hm = _sigmoid(o_ref[bi, :, sl].astype(F32)) * hs
            y = hm * lax.rsqrt(jnp.mean(hm * hm, axis=-1, keepdims=True) + EPS) * nw_ref[:, sl]
            out_ref[bi, :, sl] = y.astype(out_ref.dtype)


def _mlstm(p_all, gates_t, conv_q, conv_k, norm_w, layer, batch, seq):
    n, cols = p_all.shape
    width = M_HEADS * M_HEAD_DIM
    nc = seq // CHUNK
    L = CHUNK
    nb = MLSTM_BATCH
    p3 = p_all.reshape(batch, seq, cols)

    def p_spec(colblk):
        return pl.BlockSpec((nb, L, width), lambda g, c: (g, c, colblk))

    out = pl.pallas_call(
        _mlstm_kernel,
        grid=(batch // nb, nc),
        in_specs=[
            p_spec(3), p_spec(4), p_spec(5), p_spec(6),
            pl.BlockSpec((nb, None, 2 * M_HEADS, L), lambda g, c: (g, c, 0, 0)),
            pl.BlockSpec((None, CONV_W, width), lambda g, c: (layer, 0, 0)),
            pl.BlockSpec((None, CONV_W, width), lambda g, c: (layer, 0, 0)),
            pl.BlockSpec((1, width), lambda g, c: (0, 0)),
        ],
        out_specs=pl.BlockSpec((nb, L, width), lambda g, c: (g, c, 0)),
        out_shape=jax.ShapeDtypeStruct((batch, seq, width), BF16),
        scratch_shapes=[
            pltpu.VMEM((nb, L + 8, width), F32),
            pltpu.VMEM((nb, L + 8, width), F32),
            pltpu.VMEM((nb, M_HEADS, M_HEAD_DIM, M_HEAD_DIM), F32),
            pltpu.VMEM((nb, M_HEADS, 1, M_HEAD_DIM), F32),
            pltpu.VMEM((nb, M_HEADS, 1, 1), F32),
        ],
        name="mlstm",
        compiler_params=_cparams(("arbitrary", "arbitrary")),
    )(p3, p3, p3, p3, gates_t, conv_q, conv_k, norm_w.reshape(1, width))
    return out.reshape(n, width)


def _merge_kernel(ya_ref, hm_ref, ga_ref, gm_ref, wa_ref, wm_ref, o_ref, wab_ref, wmb_ref):
    @pl.when(pl.program_id(1) == 0)
    def _():
        wab_ref[...] = wa_ref[...].astype(BF16)
        wmb_ref[...] = wm_ref[...].astype(BF16)

    a = jnp.dot(ya_ref[...], wab_ref[...], preferred_element_type=F32)
    m = jnp.dot(hm_ref[...], wmb_ref[...], preferred_element_type=F32)
    out = _sigmoid(ga_ref[...].astype(F32)) * a + _sigmoid(gm_ref[...].astype(F32)) * m
    o_ref[...] = out.astype(o_ref.dtype)


def _merge(y_attn, h_m, p_all, w_ba, w_bm, layer, ga_col0, gm_col0, tm=1024, tn=1024):
    n, ka = y_attn.shape
    km = h_m.shape[1]
    d = w_ba.shape[2]
    return pl.pallas_call(
        _merge_kernel,
        grid=(d // tn, n // tm),
        in_specs=[
            pl.BlockSpec((tm, ka), lambda j, i: (i, 0)),
            pl.BlockSpec((tm, km), lambda j, i: (i, 0)),
            pl.BlockSpec((tm, tn), lambda j, i: (i, ga_col0 // tn + j)),
            pl.BlockSpec((tm, tn), lambda j, i: (i, gm_col0 // tn + j)),
            pl.BlockSpec((None, ka, tn), lambda j, i: (layer, 0, j)),
            pl.BlockSpec((None, km, tn), lambda j, i: (layer, 0, j)),
        ],
        out_specs=pl.BlockSpec((tm, tn), lambda j, i: (i, j)),
        out_shape=jax.ShapeDtypeStruct((n, d), BF16),
        scratch_shapes=[pltpu.VMEM((ka, tn), BF16), pltpu.VMEM((km, tn), BF16)],
        name="branch_merge",
        compiler_params=_cparams(("arbitrary", "arbitrary")),
    )(y_attn, h_m, p_all, p_all, w_ba, w_bm)


def _out_kernel(a_ref, w_ref, x_ref, mod_ref, o_ref, wb_ref, *, gate_row):
    @pl.when(pl.program_id(1) == 0)
    def _():
        wb_ref[...] = w_ref[...].astype(BF16)

    y = jnp.dot(a_ref[...], wb_ref[...], preferred_element_type=F32)
    o_ref[...] = x_ref[...] + mod_ref[gate_row:gate_row + 1, :] * y


def _out_proj(a, w_stack, layer, x2, mod_l, seq, gate_row, tm=1024, tn=1024):
    n, k = a.shape
    d = x2.shape[1]
    tm = min(tm, seq)
    blocks_per_batch = seq // tm
    return pl.pallas_call(
        functools.partial(_out_kernel, gate_row=gate_row),
        grid=(d // tn, n // tm),
        in_specs=[
            pl.BlockSpec((tm, k), lambda j, i: (i, 0)),
            pl.BlockSpec((None, k, tn), lambda j, i: (layer, 0, j)),
            pl.BlockSpec((tm, tn), lambda j, i: (i, j)),
            pl.BlockSpec((None, 6, tn), lambda j, i: (i // blocks_per_batch, 0, j)),
        ],
        out_specs=pl.BlockSpec((tm, tn), lambda j, i: (i, j)),
        out_shape=jax.ShapeDtypeStruct((n, d), F32),
        scratch_shapes=[pltpu.VMEM((k, tn), BF16)],
        name="out_proj",
        compiler_params=_cparams(("arbitrary", "arbitrary")),
    )(a, w_stack, x2, mod_l)


def _moe_kernel(tok_ref, be_ref, first_ref, nxt_ref, nact_ref, h_hbm, wg_hbm, wu_hbm, wd_hbm, o_ref,
                xbuf, wg_st, wu_st, wd_st, wg_b, wu_b, wd_b, xsem, wsem, *, layer):
    i = pl.program_id(0)
    nact = nact_ref[0]
    blk = xbuf.shape[1]
    stages = ((wg_hbm, wg_st, wg_b), (wu_hbm, wu_st, wu_b), (wd_hbm, wd_st, wd_b))

    def weight_copy(k, e):
        return pltpu.make_async_copy(stages[k][0].at[layer, e], stages[k][1], wsem.at[k])

    def start_gather(j, slot):
        def body(r, carry):
            tok = tok_ref[j * blk + r]
            pltpu.make_async_copy(h_hbm.at[pl.ds(tok, 1)], xbuf.at[slot, pl.ds(r, 1)], xsem.at[slot]).start()
            return carry
        lax.fori_loop(0, blk, body, 0, unroll=8)

    def wait_gather(slot):
        pltpu.make_async_copy(h_hbm.at[pl.ds(0, blk)], xbuf.at[slot], xsem.at[slot]).wait()

    def cast_stage(st, dst):
        rows = st.shape[0]
        step = 256

        def body(c, carry):
            r0 = pl.multiple_of(c * step, step)
            dst[pl.ds(r0, step), :] = st[pl.ds(r0, step), :].astype(BF16)
            return carry
        lax.fori_loop(0, rows // step, body, 0)

    @pl.when(i == 0)
    def _():
        for k in range(3):
            weight_copy(k, be_ref[0]).start()
        start_gather(0, 0)

    @pl.when(i < nact)
    def _():
        slot = i % 2

        @pl.when(i + 1 < nact)
        def _():
            start_gather(i + 1, 1 - slot)

        @pl.when(first_ref[i] == 1)
        def _():
            e_next = nxt_ref[i]
            for k in range(3):
                weight_copy(k, be_ref[i]).wait()
                cast_stage(stages[k][1], stages[k][2])

                @pl.when(e_next >= 0)
                def _(k=k):
                    weight_copy(k, e_next).start()

        wait_gather(slot)
        x = xbuf[slot].astype(BF16)
        g = jnp.dot(x, wg_b[...], preferred_element_type=F32)
        u = jnp.dot(x, wu_b[...], preferred_element_type=F32)
        a = (_silu(g) * u).astype(BF16)
        o_ref[...] = jnp.dot(a, wd_b[...], preferred_element_type=F32)

    @pl.when(i >= nact)
    def _():
        o_ref[...] = jnp.zeros(o_ref.shape, o_ref.dtype)


def _moe_experts(h2, slot_tok, blk_expert, blk_first, blk_next, n_active, w_gate, w_up, w_down, layer):
    n, d = h2.shape
    f = w_gate.shape[3]
    cap = slot_tok.shape[0]
    n_blocks = cap // MOE_BLK
    any_spec = pl.BlockSpec(memory_space=pl.ANY)
    grid_spec = pltpu.PrefetchScalarGridSpec(
        num_scalar_prefetch=5,
        grid=(n_blocks,),
        in_specs=[any_spec, any_spec, any_spec, any_spec],
        out_specs=pl.BlockSpec((MOE_BLK, d), lambda i, *_: (i, 0)),
        scratch_shapes=[
            pltpu.VMEM((2, MOE_BLK, d), F32),
            pltpu.VMEM((d, f), F32), pltpu.VMEM((d, f), F32), pltpu.VMEM((f, d), F32),
            pltpu.VMEM((d, f), BF16), pltpu.VMEM((d, f), BF16), pltpu.VMEM((f, d), BF16),
            pltpu.SemaphoreType.DMA((2,)),
            pltpu.SemaphoreType.DMA((3,)),
        ],
    )
    return pl.pallas_call(
        functools.partial(_moe_kernel, layer=layer),
        grid_spec=grid_spec,
        out_shape=jax.ShapeDtypeStruct((cap, d), F32),
        name="moe_experts",
        compiler_params=_cparams(("arbitrary",)),
    )(slot_tok, blk_expert, blk_first, blk_next, n_active, h2, w_gate, w_up, w_down)


def _combine_kernel(pos_ref, yb_hbm, x_ref, w_ref, mod_ref, o_ref, buf, sem, *, gate_row):
    i = pl.program_id(0)
    nsteps = pl.num_programs(0)
    t = x_ref.shape[0]

    def start(j, slot):
        def body(r, carry):
            for k in range(TOP_K):
                p = pos_ref[(j * t + r) * TOP_K + k]
                pltpu.make_async_copy(yb_hbm.at[pl.ds(p, 1)], buf.at[slot, k, pl.ds(r, 1)], sem.at[slot]).start()
            return carry
        lax.fori_loop(0, t, body, 0, unroll=4)

    def wait(slot):
        for k in range(TOP_K):
            pltpu.make_async_copy(yb_hbm.at[pl.ds(0, t)], buf.at[slot, k], sem.at[slot]).wait()

    @pl.when(i == 0)
    def _():
        start(0, 0)

    slot = i % 2

    @pl.when(i + 1 < nsteps)
    def _():
        start(i + 1, 1 - slot)

    wait(slot)
    w = w_ref[...]
    y = w[:, 0:1] * buf[slot, 0] + w[:, 1:2] * buf[slot, 1]
    o_ref[...] = x_ref[...] + mod_ref[gate_row:gate_row + 1, :] * y


def _combine(yb, pos, weights, x2, mod_l, seq, gate_row, tm=256):
    n, d = x2.shape
    blocks_per_batch = seq // tm
    grid_spec = pltpu.PrefetchScalarGridSpec(
        num_scalar_prefetch=1,
        grid=(n // tm,),
        in_specs=[
            pl.BlockSpec(memory_space=pl.ANY),
            pl.BlockSpec((tm, d), lambda i, *_: (i, 0)),
            pl.BlockSpec((tm, TOP_K), lambda i, *_: (i, 0)),
            pl.BlockSpec((None, 6, d), lambda i, *_: (i // blocks_per_batch, 0, 0)),
        ],
        out_specs=pl.BlockSpec((tm, d), lambda i, *_: (i, 0)),
        scratch_shapes=[pltpu.VMEM((2, TOP_K, tm, d), F32), pltpu.SemaphoreType.DMA((2,))],
    )
    return pl.pallas_call(
        functools.partial(_combine_kernel, gate_row=gate_row),
        grid_spec=grid_spec,
        out_shape=jax.ShapeDtypeStruct((n, d), F32),
        name="moe_combine",
        compiler_params=_cparams(("arbitrary",)),
    )(pos.reshape(-1), yb, x2, weights, mod_l)


def _route(logits):
    n_tok = logits.shape[0]
    rows = jnp.arange(n_tok)
    coarse = logits[:, :N_GROUPS]
    grp = jnp.argmax(coarse, axis=-1)
    p_grp = jax.nn.softmax(coarse, axis=-1)[rows, grp]
    fine = logits[:, N_GROUPS:N_GROUPS + N_EXPERTS].reshape(n_tok, N_GROUPS, EXPERTS_PER_GROUP)
    top_val, top_idx = lax.top_k(fine[rows, grp], TOP_K)
    weights = p_grp[:, None] * jax.nn.softmax(top_val, axis=-1)
    expert = grp[:, None] * EXPERTS_PER_GROUP + top_idx
    return expert.astype(jnp.int32), weights


def _dispatch(expert):
    n_tok = expert.shape[0]
    n_assign = n_tok * TOP_K
    cap = n_assign + N_EXPERTS * MOE_BLK
    n_blocks = cap // MOE_BLK
    e_flat = expert.reshape(-1)
    onehot = (e_flat[:, None] == jnp.arange(N_EXPERTS, dtype=jnp.int32)[None, :]).astype(jnp.int32)
    cum = jnp.cumsum(onehot, axis=0)
    counts = cum[-1]
    rank = jnp.sum(onehot * (cum - 1), axis=1)
    padded = ((counts + MOE_BLK - 1) // MOE_BLK) * MOE_BLK
    pad_ends = jnp.cumsum(padded)
    pad_starts = pad_ends - padded
    dest = (jnp.sum(onehot * pad_starts[None, :], axis=1) + rank).astype(jnp.int32)
    tok_flat = jnp.arange(n_assign, dtype=jnp.int32) // TOP_K
    slot_tok = jnp.zeros((cap,), jnp.int32).at[dest].set(tok_flat)
    blk_start = jnp.arange(n_blocks, dtype=jnp.int32) * MOE_BLK
    blk_expert = jnp.minimum(jnp.sum((pad_ends[None, :] <= blk_start[:, None]).astype(jnp.int32), axis=1),
                             N_EXPERTS - 1).astype(jnp.int32)
    n_active = (pad_ends[-1] // MOE_BLK).astype(jnp.int32)
    prev = jnp.concatenate([jnp.full((1,), -1, jnp.int32), blk_expert[:-1]])
    blk_first = (blk_expert != prev).astype(jnp.int32)
    run_end = pad_ends[blk_expert] // MOE_BLK
    blk_next = jnp.where(run_end < n_active, blk_expert[jnp.minimum(run_end, n_blocks - 1)], -1).astype(jnp.int32)
    return slot_tok, blk_expert, blk_first, blk_next, n_active.reshape(1), dest.reshape(n_tok, TOP_K)


def kernel(x, c, ada_w, ada_b, norm1_w, norm2_w, w_in, conv_q, conv_k, igate_b, fgate_b, rel_bias,
           mlstm_norm_w, w_branch_attn, w_branch_mlstm, w_out, router_coarse_w, router_coarse_b,
           router_fine_w, router_fine_b, w_gate, w_up, w_down, final_norm_w):
    b, s, d = x.shape
    depth = ada_w.shape[0]
    n = b * s
    nc = s // CHUNK
    a_width = A_HEADS * A_HEAD_DIM
    m_width = M_HEADS * M_HEAD_DIM
    main_cols = 3 * a_width + 4 * m_width + 2 * d
    ga_col0 = 3 * a_width + 4 * m_width
    gm_col0 = ga_col0 + d

    mod = _ada_mod(c, ada_w, ada_b)
    x2 = x.reshape(n, d)

    def gate_spec(l):
        return pl.BlockSpec((None, d, LANES), lambda i: (l, 0, main_cols // LANES))

    router_spec = pl.BlockSpec((d, LANES), lambda i: (0, 0))

    for l in range(depth):
        b_g = jnp.zeros((1, LANES), F32).at[0, :M_HEADS].set(igate_b[l]).at[0, M_HEADS:2 * M_HEADS].set(fgate_b[l])
        w_r = (jnp.zeros((d, LANES), F32).at[:, :N_GROUPS].set(router_coarse_w[l])
               .at[:, N_GROUPS:N_GROUPS + N_EXPERTS].set(router_fine_w[l]))
        b_r = (jnp.zeros((1, LANES), F32).at[0, :N_GROUPS].set(router_coarse_b[l])
               .at[0, N_GROUPS:N_GROUPS + N_EXPERTS].set(router_fine_b[l]))

        h, gates = _norm_mod(x2, norm1_w[l], mod[l], w_in, gate_spec(l), 2 * M_HEADS, b_g, seq=s,
                             shift_row=0, scale_row=1, precise=False, out_dtype=BF16)
        p_all = _proj(h, w_in, l, main_cols)
        y_attn = _attention(p_all, _attn_bias(rel_bias[l]), b, s)
        gates_t = gates[:, :2 * M_HEADS].reshape(b, nc, CHUNK, 2 * M_HEADS).transpose(0, 1, 3, 2)
        h_m = _mlstm(p_all, gates_t, conv_q, conv_k, mlstm_norm_w[l], l, b, s)
        merged = _merge(y_attn, h_m, p_all, w_branch_attn, w_branch_mlstm, l, ga_col0, gm_col0)
        x2 = _out_proj(merged, w_out, l, x2, mod[l], s, gate_row=2)

        h2, logits = _norm_mod(x2, norm2_w[l], mod[l], w_r, router_spec, N_GROUPS + N_EXPERTS, b_r, seq=s,
                               shift_row=3, scale_row=4, precise=True, out_dtype=F32)
        expert, weights = _route(logits)
        slot_tok, blk_expert, blk_first, blk_next, n_active, pos = _dispatch(expert)
        yb = _moe_experts(h2, slot_tok, blk_expert, blk_first, blk_next, n_active, w_gate, w_up, w_down, l)
        x2 = _combine(yb, pos, weights, x2, mod[l], s, gate_row=5)

    return _final_norm(x2, final_norm_w).reshape(b, s, d)
```

```python
import functools

import jax
import jax.numpy as jnp
from jax import lax
from jax.experimental import pallas as pl
from jax.experimental.pallas import tpu as pltpu

F32 = jnp.float32
BF16 = jnp.bfloat16

EPS = 1e-6
NEG_INF = -1e30
LOG2E = 1.4426950408889634
CHUNK = 64
LEFT_CHUNKS = 8
REL_CLIP = 256
A_HEADS = 8
A_HEAD_DIM = 128
M_HEADS = 4
M_HEAD_DIM = 256
CONV_W = 4
N_GROUPS = 4
EXPERTS_PER_GROUP = 8
N_EXPERTS = N_GROUPS * EXPERTS_PER_GROUP
TOP_K = 2

LANES = 128
VMEM_LIMIT = 60 * 1024 * 1024

ATT_QBLK = 256
ATT_KBLKS = 3
MLSTM_BATCH = 2
MOE_BLK = 256


def _cparams(sem):
    return pltpu.CompilerParams(dimension_semantics=sem, vmem_limit_bytes=VMEM_LIMIT)


def _sigmoid(t):
    return 1.0 / (1.0 + jnp.exp(-t))


def _silu(t):
    return t * _sigmoid(t)


def _ada_kernel(c_ref, w_ref, b_ref, o_ref):
    w = w_ref[...].astype(BF16)
    r = jnp.dot(c_ref[...], w, preferred_element_type=F32)
    bp = o_ref.shape[0]
    o_ref[...] = r[:bp] + r[bp:] + b_ref[...]


def _ada_mod(c, ada_w, ada_b):
    depth, d, n6 = ada_w.shape
    b = c.shape[0]
    bp = 8
    c_pad = jnp.zeros((bp, d), F32).at[:b].set(c)
    c_hi = c_pad.astype(BF16)
    c_lo = (c_pad - c_hi.astype(F32)).astype(BF16)
    c2 = jnp.concatenate([c_hi, c_lo], axis=0)
    tn = 1024
    out = pl.pallas_call(
        _ada_kernel,
        grid=(depth, n6 // tn),
        in_specs=[
            pl.BlockSpec((2 * bp, d), lambda l, j: (0, 0)),
            pl.BlockSpec((None, d, tn), lambda l, j: (l, 0, j)),
            pl.BlockSpec((None, 1, tn), lambda l, j: (l, 0, j)),
        ],
        out_specs=pl.BlockSpec((None, bp, tn), lambda l, j: (l, 0, j)),
        out_shape=jax.ShapeDtypeStruct((depth, bp, n6), F32),
        name="ada_mod",
        compiler_params=_cparams(("arbitrary", "arbitrary")),
    )(c2, ada_w, ada_b.reshape(depth, 1, n6))
    return out[:, :b].reshape(depth, b, 6, d)


def _norm_kernel(x_ref, nw_ref, mod_ref, ws_ref, bs_ref, h_ref, s_ref, *, shift_row, scale_row, precise,
                 side_cols):
    x = x_ref[...]
    y = x * lax.rsqrt(jnp.mean(x * x, axis=-1, keepdims=True) + EPS)
    y = y * nw_ref[...]
    h = y * (1.0 + mod_ref[scale_row:scale_row + 1, :]) + mod_ref[shift_row:shift_row + 1, :]
    h_ref[...] = h.astype(h_ref.dtype)
    wrow = lax.broadcasted_iota(jnp.int32, ws_ref.shape, 0)
    ws = jnp.where(wrow < side_cols, ws_ref[...], 0.0)
    nt = (((1,), (1,)), ((), ()))
    if precise:
        s = lax.dot_general(h, ws, nt, preferred_element_type=F32, precision=lax.Precision.HIGHEST)
    else:
        s = lax.dot_general(h.astype(BF16), ws.astype(BF16), nt, preferred_element_type=F32)
    s_ref[...] = s + bs_ref[...]


def _norm_mod(x2, norm_w, mod_l, w_side, side_spec, side_cols, b_side, *, seq, shift_row, scale_row, precise,
              out_dtype, tm=256):
    n, d = x2.shape
    blocks_per_batch = seq // tm
    kern = functools.partial(_norm_kernel, shift_row=shift_row, scale_row=scale_row, precise=precise,
                             side_cols=side_cols)
    return pl.pallas_call(
        kern,
        grid=(n // tm,),
        in_specs=[
            pl.BlockSpec((tm, d), lambda i: (i, 0)),
            pl.BlockSpec((1, d), lambda i: (0, 0)),
            pl.BlockSpec((None, 6, d), lambda i: (i // blocks_per_batch, 0, 0)),
            side_spec,
            pl.BlockSpec((1, LANES), lambda i: (0, 0)),
        ],
        out_specs=[
            pl.BlockSpec((tm, d), lambda i: (i, 0)),
            pl.BlockSpec((tm, LANES), lambda i: (i, 0)),
        ],
        out_shape=[jax.ShapeDtypeStruct((n, d), out_dtype), jax.ShapeDtypeStruct((n, LANES), F32)],
        name="norm_mod",
        compiler_params=_cparams(("arbitrary",)),
    )(x2, norm_w.reshape(1, d), mod_l, w_side, b_side)


def _final_norm_kernel(x_ref, nw_ref, o_ref):
    x = x_ref[...]
    y = x * lax.rsqrt(jnp.mean(x * x, axis=-1, keepdims=True) + EPS)
    o_ref[...] = y * nw_ref[...]


def _final_norm(x2, norm_w, tm=256):
    n, d = x2.shape
    return pl.pallas_call(
        _final_norm_kernel,
        grid=(n // tm,),
        in_specs=[pl.BlockSpec((tm, d), lambda i: (i, 0)), pl.BlockSpec((1, d), lambda i: (0, 0))],
        out_specs=pl.BlockSpec((tm, d), lambda i: (i, 0)),
        out_shape=jax.ShapeDtypeStruct((n, d), F32),
        name="final_norm",
        compiler_params=_cparams(("arbitrary",)),
    )(x2, norm_w.reshape(1, d))


def _proj_kernel(a_ref, wt_ref, o_ref, wb_ref):
    @pl.when(pl.program_id(1) == 0)
    def _():
        wb_ref[...] = wt_ref[...].astype(BF16)

    o_ref[...] = lax.dot_general(a_ref[...], wb_ref[...], (((1,), (1,)), ((), ())),
                                 preferred_element_type=F32).astype(o_ref.dtype)


def _proj(a, wt_stack, layer, n_cols, tm=1024, tn=1024):
    m, k = a.shape
    return pl.pallas_call(
        _proj_kernel,
        grid=(n_cols // tn, m // tm),
        in_specs=[
            pl.BlockSpec((tm, k), lambda j, i: (i, 0)),
            pl.BlockSpec((None, tn, k), lambda j, i: (layer, j, 0)),
        ],
        out_specs=pl.BlockSpec((tm, tn), lambda j, i: (i, j)),
        out_shape=jax.ShapeDtypeStruct((m, n_cols), BF16),
        scratch_shapes=[pltpu.VMEM((tn, k), BF16)],
        name="in_proj",
        compiler_params=_cparams(("arbitrary", "arbitrary")),
    )(a, wt_stack)


def _attn_kernel(q_ref, k0_ref, k1_ref, k2_ref, v0_ref, v1_ref, v2_ref, bias_ref, o_ref):
    k_refs = (k0_ref, k1_ref, k2_ref)
    v_refs = (v0_ref, v1_ref, v2_ref)
    qb = q_ref.shape[0]
    half = qb // 2
    scale2 = (A_HEAD_DIM ** -0.5) * LOG2E
    for h in range(A_HEADS):
        sl = slice(h * A_HEAD_DIM, (h + 1) * A_HEAD_DIM)
        for part in range(2):
            r0 = part * half
            c0 = part * half
            c1 = c0 + ATT_KBLKS * qb - half
            q = q_ref[r0:r0 + half, sl]
            pieces = []
            for j in range(ATT_KBLKS):
                lo, hi = max(c0, j * qb), min(c1, (j + 1) * qb)
                kblk = k_refs[j][lo - j * qb:hi - j * qb, sl]
                pieces.append(lax.dot_general(q, kblk, (((1,), (1,)), ((), ())), preferred_element_type=F32))
            s = jnp.concatenate(pieces, axis=1) * scale2 + bias_ref[h, r0:r0 + half, c0:c1]
            m = jnp.max(s, axis=-1, keepdims=True)
            e = jnp.exp2(s - m)
            denom = jnp.sum(e, axis=-1, keepdims=True)
            p = e.astype(BF16)
            acc = None
            off = 0
            for j in range(ATT_KBLKS):
                lo, hi = max(c0, j * qb), min(c1, (j + 1) * qb)
                term = jnp.dot(p[:, off:off + hi - lo], v_refs[j][lo - j * qb:hi - j * qb, sl],
                               preferred_element_type=F32)
                acc = term if acc is None else acc + term
                off += hi - lo
            o_ref[r0:r0 + half, sl] = (acc / denom).astype(o_ref.dtype)


def _attn_bias(rel_table):
    qb, kw = ATT_QBLK, ATT_KBLKS * ATT_QBLK
    nh = rel_table.shape[0]
    qi = jnp.arange(qb)[:, None]
    kj = jnp.arange(kw)[None, :]
    off = kw - 1 - (ATT_KBLKS - 1) * qb
    glen = qb + kw
    n_lo = max(0, min(glen, off - REL_CLIP))
    n_lin = max(0, min(glen, off + REL_CLIP + 1) - n_lo)
    n_hi = glen - n_lo - n_lin
    lin0 = n_lo - off + REL_CLIP
    g = jnp.concatenate([jnp.broadcast_to(rel_table[:, :1], (nh, n_lo)),
                         rel_table[:, lin0:lin0 + n_lin],
                         jnp.broadcast_to(rel_table[:, 2 * REL_CLIP:], (nh, n_hi))], axis=1).astype(F32)
    hank = jnp.tile(g, (1, qb + 1))[:, :qb * (glen + 1)].reshape(nh, qb, glen + 1)[:, :, :kw]
    bias = hank[:, :, ::-1] * LOG2E
    qc = qi // CHUNK + (ATT_KBLKS - 1) * (qb // CHUNK)
    kc = kj // CHUNK
    band = (kc <= qc) & (kc >= qc - LEFT_CHUNKS)
    tables = []
    for t in range(ATT_KBLKS):
        ok = band & (kj >= (ATT_KBLKS - 1 - t) * qb)
        tables.append(jnp.where(ok[None], bias, NEG_INF))
    return jnp.stack(tables)


def _attention(p_all, bias, batch, seq):
    n = p_all.shape[0]
    width = A_HEADS * A_HEAD_DIM
    qb = ATT_QBLK
    nb = seq // qb

    def kv_spec(back, colblk):
        return pl.BlockSpec((qb, width), lambda i, b: (b * nb + jnp.maximum(i - back, 0), colblk))

    return pl.pallas_call(
        _attn_kernel,
        grid=(nb, batch),
        in_specs=[
            pl.BlockSpec((qb, width), lambda i, b: (b * nb + i, 0)),
            kv_spec(2, 1), kv_spec(1, 1), kv_spec(0, 1),
            kv_spec(2, 2), kv_spec(1, 2), kv_spec(0, 2),
            pl.BlockSpec((None, A_HEADS, qb, ATT_KBLKS * qb), lambda i, b: (jnp.minimum(i, ATT_KBLKS - 1), 0, 0, 0)),
        ],
        out_specs=pl.BlockSpec((qb, width), lambda i, b: (b * nb + i, 0)),
        out_shape=jax.ShapeDtypeStruct((n, width), BF16),
        name="chunk_attn",
        compiler_params=_cparams(("arbitrary", "arbitrary")),
    )(p_all, p_all, p_all, p_all, p_all, p_all, p_all, bias)


def _log_sigmoid(t):
    return jnp.minimum(t, 0.0) - jnp.log(1.0 + jnp.exp(-jnp.abs(t)))


def _mlstm_kernel(q_ref, k_ref, v_ref, o_ref, g_ref, cq_ref, ck_ref, nw_ref, out_ref,
                  qbuf, kbuf, ct_ref, n_ref, m_ref):
    c = pl.program_id(1)
    L, D = CHUNK, M_HEAD_DIM
    nb = q_ref.shape[0]
    tail = 8

    @pl.when(c == 0)
    def _():
        qbuf[:, 0:tail, :] = jnp.zeros((nb, tail, qbuf.shape[2]), F32)
        kbuf[:, 0:tail, :] = jnp.zeros((nb, tail, kbuf.shape[2]), F32)
        ct_ref[...] = jnp.zeros(ct_ref.shape, F32)
        n_ref[...] = jnp.zeros(n_ref.shape, F32)
        m_ref[...] = jnp.zeros(m_ref.shape, F32)

    row = lax.broadcasted_iota(jnp.int32, (L, L), 0)
    colm = lax.broadcasted_iota(jnp.int32, (L, L), 1)
    causal = colm <= row
    eye = colm == row
    upper = (row <= colm).astype(F32)

    def to_col(r):
        return jnp.sum(jnp.where(eye, jnp.broadcast_to(r, (L, L)), 0.0), axis=1, keepdims=True)

    def conv(buf, w_ref):
        acc = None
        for j in range(CONV_W):
            off = tail - (CONV_W - 1) + j
            term = buf[off:off + L, :] * w_ref[j:j + 1, :]
            acc = term if acc is None else acc + term
        return acc

    for bi in range(nb):
        qb_ref, kb_ref = qbuf.at[bi], kbuf.at[bi]
        qb_ref[tail:tail + L, :] = q_ref[bi].astype(F32)
        kb_ref[tail:tail + L, :] = k_ref[bi].astype(F32)
        q_all = _silu(conv(qb_ref, cq_ref)) * (D ** -0.5)
        k_all = _silu(conv(kb_ref, ck_ref))
        qb_ref[0:tail, :] = qb_ref[L:L + tail, :]
        kb_ref[0:tail, :] = kb_ref[L:L + tail, :]

        for h in range(M_HEADS):
            sl = slice(h * D, (h + 1) * D)
            q = q_all[:, sl]
            k = k_all[:, sl]
            qb16 = q.astype(BF16)
            kb16 = k.astype(BF16)
            vb16 = v_ref[bi, :, sl]
            ig = g_ref[bi, h:h + 1, :]
            lf = _log_sigmoid(g_ref[bi, M_HEADS + h:M_HEADS + h + 1, :])
            bcum = jnp.dot(jnp.broadcast_to(lf, (8, L)), upper, preferred_element_type=F32,
                           precision=lax.Precision.HIGHEST)[0:1, :]
            bcum_c = to_col(bcum)
            m_prev = m_ref[bi, h]
            logd = jnp.where(causal, bcum_c - bcum + ig, NEG_INF)
            inter = bcum_c + m_prev
            m_s = jnp.maximum(jnp.max(logd, axis=-1, keepdims=True), inter)
            s = lax.dot_general(qb16, kb16, (((1,), (1,)), ((), ())), preferred_element_type=F32)
            w_intra = s * jnp.exp(logd - m_s)
            w_inter = jnp.exp(inter - m_s)
            ct = ct_ref[bi, h]
            n_row = n_ref[bi, h]
            num = (jnp.dot(w_intra.astype(BF16), vb16, preferred_element_type=F32)
                   + w_inter * jnp.dot(qb16, ct.astype(BF16), preferred_element_type=F32))
            den = (jnp.sum(w_intra, axis=-1, keepdims=True)
                   + w_inter * jnp.sum(q * n_row, axis=-1, keepdims=True))
            hs = num / jnp.maximum(jnp.abs(den), jnp.exp(-m_s))
            b_last = bcum[:, L - 1:L]
            log_wk = b_last - bcum + ig
            m_new = jnp.maximum(b_last + m_prev, jnp.max(log_wk, axis=-1, keepdims=True))
            wk = jnp.exp(log_wk - m_new)
            decay = jnp.exp(b_last + m_prev - m_new)
            kw = k * to_col(wk)
            ct_ref[bi, h] = decay * ct + lax.dot_general(kw.astype(BF16), vb16, (((0,), (0,)), ((), ())),
                                                         preferred_element_type=F32)
            n_ref[bi, h] = decay * n_row + jnp.sum(kw, axis=0, keepdims=True)
            m_ref[bi, h] = m_new
            hm = _sigmoid(o_ref[bi, :, sl].astype(F32)) * hs
            y = hm * lax.rsqrt(jnp.mean(hm * hm, axis=-1, keepdims=True) + EPS) * nw_ref[:, sl]
            out_ref[bi, :, sl] = y.astype(out_ref.dtype)


def _mlstm(p_all, gates_t, conv_q, conv_k, norm_w, layer, batch, seq):
    n, cols = p_all.shape
    width = M_HEADS * M_HEAD_DIM
    nc = seq // CHUNK
    L = CHUNK
    nb = MLSTM_BATCH
    p3 = p_all.reshape(batch, seq, cols)

    def p_spec(colblk):
        return pl.BlockSpec((nb, L, width), lambda g, c: (g, c, colblk))

    out = pl.pallas_call(
        _mlstm_kernel,
        grid=(batch // nb, nc),
        in_specs=[
            p_spec(3), p_spec(4), p_spec(5), p_spec(6),
            pl.BlockSpec((nb, None, 2 * M_HEADS, L), lambda g, c: (g, c, 0, 0)),
            pl.BlockSpec((None, CONV_W, width), lambda g, c: (layer, 0, 0)),
            pl.BlockSpec((None, CONV_W, width), lambda g, c: (layer, 0, 0)),
            pl.BlockSpec((1, width), lambda g, c: (0, 0)),
        ],
        out_specs=pl.BlockSpec((nb, L, width), lambda g, c: (g, c, 0)),
        out_shape=jax.ShapeDtypeStruct((batch, seq, width), BF16),
        scratch_shapes=[
            pltpu.VMEM((nb, L + 8, width), F32),
            pltpu.VMEM((nb, L + 8, width), F32),
            pltpu.VMEM((nb, M_HEADS, M_HEAD_DIM, M_HEAD_DIM), F32),
            pltpu.VMEM((nb, M_HEADS, 1, M_HEAD_DIM), F32),
            pltpu.VMEM((nb, M_HEADS, 1, 1), F32),
        ],
        name="mlstm",
        compiler_params=_cparams(("arbitrary", "arbitrary")),
    )(p3, p3, p3, p3, gates_t, conv_q, conv_k, norm_w.reshape(1, width))
    return out.reshape(n, width)


def _merge_kernel(ya_ref, hm_ref, ga_ref, gm_ref, wa_ref, wm_ref, o_ref, wab_ref, wmb_ref):
    @pl.when(pl.program_id(1) == 0)
    def _():
        wab_ref[...] = wa_ref[...].astype(BF16)
        wmb_ref[...] = wm_ref[...].astype(BF16)

    a = jnp.dot(ya_ref[...], wab_ref[...], preferred_element_type=F32)
    m = jnp.dot(hm_ref[...], wmb_ref[...], preferred_element_type=F32)
    out = _sigmoid(ga_ref[...].astype(F32)) * a + _sigmoid(gm_ref[...].astype(F32)) * m
    o_ref[...] = out.astype(o_ref.dtype)


def _merge(y_attn, h_m, p_all, w_ba, w_bm, layer, ga_col0, gm_col0, tm=1024, tn=1024):
    n, ka = y_attn.shape
    km = h_m.shape[1]
    d = w_ba.shape[2]
    return pl.pallas_call(
        _merge_kernel,
        grid=(d // tn, n // tm),
        in_specs=[
            pl.BlockSpec((tm, ka), lambda j, i: (i, 0)),
            pl.BlockSpec((tm, km), lambda j, i: (i, 0)),
            pl.BlockSpec((tm, tn), lambda j, i: (i, ga_col0 // tn + j)),
            pl.BlockSpec((tm, tn), lambda j, i: (i, gm_col0 // tn + j)),
            pl.BlockSpec((None, ka, tn), lambda j, i: (layer, 0, j)),
            pl.BlockSpec((None, km, tn), lambda j, i: (layer, 0, j)),
        ],
        out_specs=pl.BlockSpec((tm, tn), lambda j, i: (i, j)),
        out_shape=jax.ShapeDtypeStruct((n, d), BF16),
        scratch_shapes=[pltpu.VMEM((ka, tn), BF16), pltpu.VMEM((km, tn), BF16)],
        name="branch_merge",
        compiler_params=_cparams(("arbitrary", "arbitrary")),
    )(y_attn, h_m, p_all, p_all, w_ba, w_bm)


def _out_kernel(a_ref, w_ref, x_ref, mod_ref, o_ref, wb_ref, *, gate_row):
    @pl.when(pl.program_id(1) == 0)
    def _():
        wb_ref[...] = w_ref[...].astype(BF16)

    y = jnp.dot(a_ref[...], wb_ref[...], preferred_element_type=F32)
    o_ref[...] = x_ref[...] + mod_ref[gate_row:gate_row + 1, :] * y


def _out_proj(a, w_stack, layer, x2, mod_l, seq, gate_row, tm=1024, tn=1024):
    n, k = a.shape
    d = x2.shape[1]
    tm = min(tm, seq)
    blocks_per_batch = seq // tm
    return pl.pallas_call(
        functools.partial(_out_kernel, gate_row=gate_row),
        grid=(d // tn, n // tm),
        in_specs=[
            pl.BlockSpec((tm, k), lambda j, i: (i, 0)),
            pl.BlockSpec((None, k, tn), lambda j, i: (layer, 0, j)),
            pl.BlockSpec((tm, tn), lambda j, i: (i, j)),
            pl.BlockSpec((None, 6, tn), lambda j, i: (i // blocks_per_batch, 0, j)),
        ],
        out_specs=pl.BlockSpec((tm, tn), lambda j, i: (i, j)),
        out_shape=jax.ShapeDtypeStruct((n, d), F32),
        scratch_shapes=[pltpu.VMEM((k, tn), BF16)],
        name="out_proj",
        compiler_params=_cparams(("arbitrary", "arbitrary")),
    )(a, w_stack, x2, mod_l)


def _moe_kernel(tok_ref, be_ref, first_ref, nxt_ref, nact_ref, h_hbm, wg_hbm, wu_hbm, wd_hbm, o_ref,
                xbuf, xb16, wg_st, wu_st, wd_st, wg_b, wu_b, wd_b, xsem, wsem, *, layer):
    i = pl.program_id(0)
    nact = nact_ref[0]
    blk = xbuf.shape[0]
    stages = ((wg_hbm, wg_st, wg_b), (wu_hbm, wu_st, wu_b), (wd_hbm, wd_st, wd_b))

    def weight_copy(k, e):
        return pltpu.make_async_copy(stages[k][0].at[layer, e], stages[k][1], wsem.at[k])

    def start_gather(j):
        base = j * blk
        for r in range(blk):
            tok = tok_ref[base + r]
            pltpu.make_async_copy(h_hbm.at[pl.ds(tok, 1)], xbuf.at[pl.ds(r, 1)], xsem.at[0]).start()

    def wait_gather():
        pltpu.make_async_copy(h_hbm.at[pl.ds(0, blk)], xbuf, xsem.at[0]).wait()

    def cast_stage(st, dst):
        rows = st.shape[0]
        step = 256

        def body(c, carry):
            r0 = pl.multiple_of(c * step, step)
            dst[pl.ds(r0, step), :] = st[pl.ds(r0, step), :].astype(BF16)
            return carry
        lax.fori_loop(0, rows // step, body, 0)

    @pl.when(i == 0)
    def _():
        for k in range(3):
            weight_copy(k, be_ref[0]).start()
        start_gather(0)

    @pl.when(i < nact)
    def _():
        @pl.when(first_ref[i] == 1)
        def _():
            e_next = nxt_ref[i]
            for k in range(3):
                weight_copy(k, be_ref[i]).wait()
                cast_stage(stages[k][1], stages[k][2])

                @pl.when(e_next >= 0)
                def _(k=k):
                    weight_copy(k, e_next).start()

        wait_gather()
        xb16[...] = xbuf[...].astype(BF16)
        start_gather(i + 1)
        x = xb16[...]
        g = jnp.dot(x, wg_b[...], preferred_element_type=F32)
        u = jnp.dot(x, wu_b[...], preferred_element_type=F32)
        a = (_silu(g) * u).astype(BF16)
        o_ref[...] = jnp.dot(a, wd_b[...], preferred_element_type=F32)

    @pl.when(i >= nact)
    def _():
        @pl.when(i == nact)
        def _():
            wait_gather()

        o_ref[...] = jnp.zeros(o_ref.shape, o_ref.dtype)


def _moe_experts(h2, slot_tok, blk_expert, blk_first, blk_next, n_active, w_gate, w_up, w_down, layer):
    n, d = h2.shape
    f = w_gate.shape[3]
    cap = slot_tok.shape[0]
    n_blocks = cap // MOE_BLK
    any_spec = pl.BlockSpec(memory_space=pl.ANY)
    grid_spec = pltpu.PrefetchScalarGridSpec(
        num_scalar_prefetch=5,
        grid=(n_blocks,),
        in_specs=[any_spec, any_spec, any_spec, any_spec],
        out_specs=pl.BlockSpec((MOE_BLK, d), lambda i, *_: (i, 0)),
        scratch_shapes=[
            pltpu.VMEM((MOE_BLK, d), F32),
            pltpu.VMEM((MOE_BLK, d), BF16),
            pltpu.VMEM((d, f), F32), pltpu.VMEM((d, f), F32), pltpu.VMEM((f, d), F32),
            pltpu.VMEM((d, f), BF16), pltpu.VMEM((d, f), BF16), pltpu.VMEM((f, d), BF16),
            pltpu.SemaphoreType.DMA((1,)),
            pltpu.SemaphoreType.DMA((3,)),
        ],
    )
    return pl.pallas_call(
        functools.partial(_moe_kernel, layer=layer),
        grid_spec=grid_spec,
        out_shape=jax.ShapeDtypeStruct((cap, d), F32),
        name="moe_experts",
        compiler_params=_cparams(("arbitrary",)),
    )(slot_tok, blk_expert, blk_first, blk_next, n_active, h2, w_gate, w_up, w_down)


def _combine_kernel(pos_ref, yb_hbm, x_ref, w_ref, mod_ref, o_ref, buf, sem, *, gate_row):
    i = pl.program_id(0)
    nsteps = pl.num_programs(0)
    t = x_ref.shape[0]

    def start(j, slot):
        base = j * (t * TOP_K)
        dst = buf.at[slot]
        for r in range(t):
            for k in range(TOP_K):
                p = pos_ref[base + r * TOP_K + k]
                pltpu.make_async_copy(yb_hbm.at[pl.ds(p, 1)], dst.at[k, pl.ds(r, 1)], sem.at[slot]).start()

    def wait(slot):
        for k in range(TOP_K):
            pltpu.make_async_copy(yb_hbm.at[pl.ds(0, t)], buf.at[slot, k], sem.at[slot]).wait()

    @pl.when(i == 0)
    def _():
        start(0, 0)

    slot = i % 2

    @pl.when(i + 1 < nsteps)
    def _():
        start(i + 1, 1 - slot)

    wait(slot)
    w = w_ref[...]
    y = w[:, 0:1] * buf[slot, 0] + w[:, 1:2] * buf[slot, 1]
    o_ref[...] = x_ref[...] + mod_ref[gate_row:gate_row + 1, :] * y


def _combine(yb, pos, weights, x2, mod_l, seq, gate_row, tm=256):
    n, d = x2.shape
    blocks_per_batch = seq // tm
    grid_spec = pltpu.PrefetchScalarGridSpec(
        num_scalar_prefetch=1,
        grid=(n // tm,),
        in_specs=[
            pl.BlockSpec(memory_space=pl.ANY),
            pl.BlockSpec((tm, d), lambda i, *_: (i, 0)),
            pl.BlockSpec((tm, TOP_K), lambda i, *_: (i, 0)),
            pl.BlockSpec((None, 6, d), lambda i, *_: (i // blocks_per_batch, 0, 0)),
        ],
        out_specs=pl.BlockSpec((tm, d), lambda i, *_: (i, 0)),
        scratch_shapes=[pltpu.VMEM((2, TOP_K, tm, d), F32), pltpu.SemaphoreType.DMA((2,))],
    )
    return pl.pallas_call(
        functools.partial(_combine_kernel, gate_row=gate_row),
        grid_spec=grid_spec,
        out_shape=jax.ShapeDtypeStruct((n, d), F32),
        name="moe_combine",
        compiler_params=_cparams(("arbitrary",)),
    )(pos.reshape(-1), yb, x2, weights, mod_l)


def _route(logits):
    n_tok = logits.shape[0]
    rows = jnp.arange(n_tok)
    coarse = logits[:, :N_GROUPS]
    grp = jnp.argmax(coarse, axis=-1)
    p_grp = jax.nn.softmax(coarse, axis=-1)[rows, grp]
    fine = logits[:, N_GROUPS:N_GROUPS + N_EXPERTS].reshape(n_tok, N_GROUPS, EXPERTS_PER_GROUP)
    top_val, top_idx = lax.top_k(fine[rows, grp], TOP_K)
    weights = p_grp[:, None] * jax.nn.softmax(top_val, axis=-1)
    expert = grp[:, None] * EXPERTS_PER_GROUP + top_idx
    return expert.astype(jnp.int32), weights


def _dispatch(expert):
    n_tok = expert.shape[0]
    n_assign = n_tok * TOP_K
    cap = n_assign + N_EXPERTS * MOE_BLK
    n_blocks = cap // MOE_BLK
    e_flat = expert.reshape(-1)
    onehot = (e_flat[:, None] == jnp.arange(N_EXPERTS, dtype=jnp.int32)[None, :]).astype(jnp.int32)
    cum = jnp.cumsum(onehot, axis=0)
    counts = cum[-1]
    rank = jnp.sum(onehot * (cum - 1), axis=1)
    padded = ((counts + MOE_BLK - 1) // MOE_BLK) * MOE_BLK
    pad_ends = jnp.cumsum(padded)
    pad_starts = pad_ends - padded
    dest = (jnp.sum(onehot * pad_starts[None, :], axis=1) + rank).astype(jnp.int32)
    tok_flat = jnp.arange(n_assign, dtype=jnp.int32) // TOP_K
    slot_tok = jnp.zeros((cap,), jnp.int32).at[dest].set(tok_flat)
    blk_start = jnp.arange(n_blocks, dtype=jnp.int32) * MOE_BLK
    blk_expert = jnp.minimum(jnp.sum((pad_ends[None, :] <= blk_start[:, None]).astype(jnp.int32), axis=1),
                             N_EXPERTS - 1).astype(jnp.int32)
    n_active = (pad_ends[-1] // MOE_BLK).astype(jnp.int32)
    prev = jnp.concatenate([jnp.full((1,), -1, jnp.int32), blk_expert[:-1]])
    blk_first = (blk_expert != prev).astype(jnp.int32)
    run_end = pad_ends[blk_expert] // MOE_BLK
    blk_next = jnp.where(run_end < n_active, blk_expert[jnp.minimum(run_end, n_blocks - 1)], -1).astype(jnp.int32)
    return slot_tok, blk_expert, blk_first, blk_next, n_active.reshape(1), dest.reshape(n_tok, TOP_K)


def kernel(x, c, ada_w, ada_b, norm1_w, norm2_w, w_in, conv_q, conv_k, igate_b, fgate_b, rel_bias,
           mlstm_norm_w, w_branch_attn, w_branch_mlstm, w_out, router_coarse_w, router_coarse_b,
           router_fine_w, router_fine_b, w_gate, w_up, w_down, final_norm_w):
    b, s, d = x.shape
    depth = ada_w.shape[0]
    n = b * s
    nc = s // CHUNK
    a_width = A_HEADS * A_HEAD_DIM
    m_width = M_HEADS * M_HEAD_DIM
    main_cols = 3 * a_width + 4 * m_width + 2 * d
    ga_col0 = 3 * a_width + 4 * m_width
    gm_col0 = ga_col0 + d

    mod = _ada_mod(c, ada_w, ada_b)
    x2 = x.reshape(n, d)

    w_in_t = jnp.swapaxes(w_in, 1, 2)

    def gate_spec(l):
        return pl.BlockSpec((None, LANES, d), lambda i: (l, main_cols // LANES, 0))

    router_spec = pl.BlockSpec((LANES, d), lambda i: (0, 0))

    for l in range(depth):
        b_g = jnp.zeros((1, LANES), F32).at[0, :M_HEADS].set(igate_b[l]).at[0, M_HEADS:2 * M_HEADS].set(fgate_b[l])
        w_r = (jnp.zeros((LANES, d), F32).at[:N_GROUPS].set(router_coarse_w[l].T)
               .at[N_GROUPS:N_GROUPS + N_EXPERTS].set(router_fine_w[l].T))
        b_r = (jnp.zeros((1, LANES), F32).at[0, :N_GROUPS].set(router_coarse_b[l])
               .at[0, N_GROUPS:N_GROUPS + N_EXPERTS].set(router_fine_b[l]))

        h, gates = _norm_mod(x2, norm1_w[l], mod[l], w_in_t, gate_spec(l), 2 * M_HEADS, b_g, seq=s,
                             shift_row=0, scale_row=1, precise=False, out_dtype=BF16)
        p_all = _proj(h, w_in_t, l, main_cols)
        y_attn = _attention(p_all, _attn_bias(rel_bias[l]), b, s)
        gates_t = gates[:, :2 * M_HEADS].reshape(b, nc, CHUNK, 2 * M_HEADS).transpose(0, 1, 3, 2)
        h_m = _mlstm(p_all, gates_t, conv_q, conv_k, mlstm_norm_w[l], l, b, s)
        merged = _merge(y_attn, h_m, p_all, w_branch_attn, w_branch_mlstm, l, ga_col0, gm_col0)
        x2 = _out_proj(merged, w_out, l, x2, mod[l], s, gate_row=2)

        h2, logits = _norm_mod(x2, norm2_w[l], mod[l], w_r, router_spec, N_GROUPS + N_EXPERTS, b_r, seq=s,
                               shift_row=3, scale_row=4, precise=True, out_dtype=F32)
        expert, weights = _route(logits)
        slot_tok, blk_expert, blk_first, blk_next, n_active, pos = _dispatch(expert)
        yb = _moe_experts(h2, slot_tok, blk_expert, blk_first, blk_next, n_active, w_gate, w_up, w_down, l)
        x2 = _combine(yb, pos, weights, x2, mod[l], s, gate_row=5)

    return _final_norm(x2, final_norm_w).reshape(b, s, d)
```

```python
import functools

import jax
import jax.numpy as jnp
from jax import lax
from jax.experimental import pallas as pl
from jax.experimental.pallas import tpu as pltpu

F32 = jnp.float32
BF16 = jnp.bfloat16

EPS = 1e-6
NEG_INF = -1e30
LOG2E = 1.4426950408889634
CHUNK = 64
LEFT_CHUNKS = 8
REL_CLIP = 256
A_HEADS = 8
A_HEAD_DIM = 128
M_HEADS = 4
M_HEAD_DIM = 256
CONV_W = 4
N_GROUPS = 4
EXPERTS_PER_GROUP = 8
N_EXPERTS = N_GROUPS * EXPERTS_PER_GROUP
TOP_K = 2

LANES = 128
VMEM_LIMIT = 60 * 1024 * 1024

ATT_QBLK = 256
ATT_KBLKS = 3
MLSTM_BATCH = 4
MOE_BLK = 256


def _cparams(sem):
    return pltpu.CompilerParams(dimension_semantics=sem, vmem_limit_bytes=VMEM_LIMIT)


def _sigmoid(t):
    return 1.0 / (1.0 + jnp.exp(-t))


def _silu(t):
    return t * _sigmoid(t)


def _ada_kernel(c_ref, w_ref, b_ref, o_ref):
    w = w_ref[...].astype(BF16)
    r = jnp.dot(c_ref[...], w, preferred_element_type=F32)
    bp = o_ref.shape[0]
    o_ref[...] = r[:bp] + r[bp:] + b_ref[...]


def _ada_mod(c, ada_w, ada_b):
    depth, d, n6 = ada_w.shape
    b = c.shape[0]
    bp = 8
    c_pad = jnp.zeros((bp, d), F32).at[:b].set(c)
    c_hi = c_pad.astype(BF16)
    c_lo = (c_pad - c_hi.astype(F32)).astype(BF16)
    c2 = jnp.concatenate([c_hi, c_lo], axis=0)
    tn = 1024
    out = pl.pallas_call(
        _ada_kernel,
        grid=(depth, n6 // tn),
        in_specs=[
            pl.BlockSpec((2 * bp, d), lambda l, j: (0, 0)),
            pl.BlockSpec((None, d, tn), lambda l, j: (l, 0, j)),
            pl.BlockSpec((None, 1, tn), lambda l, j: (l, 0, j)),
        ],
        out_specs=pl.BlockSpec((None, bp, tn), lambda l, j: (l, 0, j)),
        out_shape=jax.ShapeDtypeStruct((depth, bp, n6), F32),
        name="ada_mod",
        compiler_params=_cparams(("arbitrary", "arbitrary")),
    )(c2, ada_w, ada_b.reshape(depth, 1, n6))
    return out[:, :b].reshape(depth, b, 6, d)


def _norm_kernel(x_ref, nw_ref, mod_ref, ws_ref, bs_ref, h_ref, s_ref, *, shift_row, scale_row, precise,
                 side_cols):
    x = x_ref[...]
    y = x * lax.rsqrt(jnp.mean(x * x, axis=-1, keepdims=True) + EPS)
    y = y * nw_ref[...]
    h = y * (1.0 + mod_ref[scale_row:scale_row + 1, :]) + mod_ref[shift_row:shift_row + 1, :]
    h_ref[...] = h.astype(h_ref.dtype)
    wrow = lax.broadcasted_iota(jnp.int32, ws_ref.shape, 0)
    ws = jnp.where(wrow < side_cols, ws_ref[...], 0.0)
    nt = (((1,), (1,)), ((), ()))
    if precise:
        h_hi = h.astype(BF16)
        h_lo = (h - h_hi.astype(F32)).astype(BF16)
        w_hi = ws.astype(BF16)
        w_lo = (ws - w_hi.astype(F32)).astype(BF16)
        s = (lax.dot_general(h_hi, w_hi, nt, preferred_element_type=F32)
             + lax.dot_general(h_hi, w_lo, nt, preferred_element_type=F32)
             + lax.dot_general(h_lo, w_hi, nt, preferred_element_type=F32))
    else:
        s = lax.dot_general(h.astype(BF16), ws.astype(BF16), nt, preferred_element_type=F32)
    s_ref[...] = s + bs_ref[...]


def _norm_mod(x2, norm_w, mod_l, w_side, side_spec, side_cols, b_side, *, seq, shift_row, scale_row, precise,
              out_dtype, tm=256):
    n, d = x2.shape
    blocks_per_batch = seq // tm
    kern = functools.partial(_norm_kernel, shift_row=shift_row, scale_row=scale_row, precise=precise,
                             side_cols=side_cols)
    return pl.pallas_call(
        kern,
        grid=(n // tm,),
        in_specs=[
            pl.BlockSpec((tm, d), lambda i: (i, 0)),
            pl.BlockSpec((1, d), lambda i: (0, 0)),
            pl.BlockSpec((None, 6, d), lambda i: (i // blocks_per_batch, 0, 0)),
            side_spec,
            pl.BlockSpec((1, LANES), lambda i: (0, 0)),
        ],
        out_specs=[
            pl.BlockSpec((tm, d), lambda i: (i, 0)),
            pl.BlockSpec((tm, LANES), lambda i: (i, 0)),
        ],
        out_shape=[jax.ShapeDtypeStruct((n, d), out_dtype), jax.ShapeDtypeStruct((n, LANES), F32)],
        name="norm_mod",
        compiler_params=_cparams(("arbitrary",)),
    )(x2, norm_w.reshape(1, d), mod_l, w_side, b_side)


def _final_norm_kernel(x_ref, nw_ref, o_ref):
    x = x_ref[...]
    y = x * lax.rsqrt(jnp.mean(x * x, axis=-1, keepdims=True) + EPS)
    o_ref[...] = y * nw_ref[...]


def _final_norm(x2, norm_w, tm=256):
    n, d = x2.shape
    return pl.pallas_call(
        _final_norm_kernel,
        grid=(n // tm,),
        in_specs=[pl.BlockSpec((tm, d), lambda i: (i, 0)), pl.BlockSpec((1, d), lambda i: (0, 0))],
        out_specs=pl.BlockSpec((tm, d), lambda i: (i, 0)),
        out_shape=jax.ShapeDtypeStruct((n, d), F32),
        name="final_norm",
        compiler_params=_cparams(("arbitrary",)),
    )(x2, norm_w.reshape(1, d))


def _proj_kernel(a_ref, wt_ref, o_ref, wb_ref):
    @pl.when(pl.program_id(1) == 0)
    def _():
        wb_ref[...] = wt_ref[...].astype(BF16)

    o_ref[...] = lax.dot_general(a_ref[...], wb_ref[...], (((1,), (1,)), ((), ())),
                                 preferred_element_type=F32).astype(o_ref.dtype)


def _proj(a, wt_stack, layer, n_cols, tm=1024, tn=1024):
    m, k = a.shape
    return pl.pallas_call(
        _proj_kernel,
        grid=(n_cols // tn, m // tm),
        in_specs=[
            pl.BlockSpec((tm, k), lambda j, i: (i, 0)),
            pl.BlockSpec((None, tn, k), lambda j, i: (layer, j, 0)),
        ],
        out_specs=pl.BlockSpec((tm, tn), lambda j, i: (i, j)),
        out_shape=jax.ShapeDtypeStruct((m, n_cols), BF16),
        scratch_shapes=[pltpu.VMEM((tn, k), BF16)],
        name="in_proj",
        compiler_params=_cparams(("arbitrary", "arbitrary")),
    )(a, wt_stack)


def _attn_kernel(q_ref, k0_ref, k1_ref, k2_ref, v0_ref, v1_ref, v2_ref, bias_ref, o_ref):
    k_refs = (k0_ref, k1_ref, k2_ref)
    v_refs = (v0_ref, v1_ref, v2_ref)
    qb = q_ref.shape[0]
    half = qb // 2
    scale2 = (A_HEAD_DIM ** -0.5) * LOG2E
    for h in range(A_HEADS):
        sl = slice(h * A_HEAD_DIM, (h + 1) * A_HEAD_DIM)
        for part in range(2):
            r0 = part * half
            c0 = part * half
            c1 = c0 + ATT_KBLKS * qb - half
            q = q_ref[r0:r0 + half, sl]
            pieces = []
            for j in range(ATT_KBLKS):
                lo, hi = max(c0, j * qb), min(c1, (j + 1) * qb)
                kblk = k_refs[j][lo - j * qb:hi - j * qb, sl]
                pieces.append(lax.dot_general(q, kblk, (((1,), (1,)), ((), ())), preferred_element_type=F32))
            s = jnp.concatenate(pieces, axis=1) * scale2 + bias_ref[h, r0:r0 + half, c0:c1]
            m = jnp.max(s, axis=-1, keepdims=True)
            e = jnp.exp2(s - m)
            denom = jnp.sum(e, axis=-1, keepdims=True)
            p = e.astype(BF16)
            acc = None
            off = 0
            for j in range(ATT_KBLKS):
                lo, hi = max(c0, j * qb), min(c1, (j + 1) * qb)
                term = jnp.dot(p[:, off:off + hi - lo], v_refs[j][lo - j * qb:hi - j * qb, sl],
                               preferred_element_type=F32)
                acc = term if acc is None else acc + term
                off += hi - lo
            o_ref[r0:r0 + half, sl] = (acc / denom).astype(o_ref.dtype)


def _attn_bias(rel_table):
    qb, kw = ATT_QBLK, ATT_KBLKS * ATT_QBLK
    nh = rel_table.shape[0]
    qi = jnp.arange(qb)[:, None]
    kj = jnp.arange(kw)[None, :]
    off = kw - 1 - (ATT_KBLKS - 1) * qb
    glen = qb + kw
    n_lo = max(0, min(glen, off - REL_CLIP))
    n_lin = max(0, min(glen, off + REL_CLIP + 1) - n_lo)
    n_hi = glen - n_lo - n_lin
    lin0 = n_lo - off + REL_CLIP
    gr = jnp.concatenate([jnp.broadcast_to(rel_table[:, 2 * REL_CLIP:], (nh, n_hi)),
                          rel_table[:, lin0:lin0 + n_lin][:, ::-1],
                          jnp.broadcast_to(rel_table[:, :1], (nh, n_lo))], axis=1).astype(F32) * LOG2E
    c0 = glen - kw
    bias = jnp.tile(gr, (1, qb + 1))[:, c0:c0 + qb * (glen - 1)].reshape(nh, qb, glen - 1)[:, :, :kw]
    qc = qi // CHUNK + (ATT_KBLKS - 1) * (qb // CHUNK)
    kc = kj // CHUNK
    band = (kc <= qc) & (kc >= qc - LEFT_CHUNKS)
    tables = []
    for t in range(ATT_KBLKS):
        ok = band & (kj >= (ATT_KBLKS - 1 - t) * qb)
        tables.append(jnp.where(ok[None], bias, NEG_INF))
    return jnp.stack(tables)


def _attention(p_all, bias, batch, seq):
    n = p_all.shape[0]
    width = A_HEADS * A_HEAD_DIM
    qb = ATT_QBLK
    nb = seq // qb

    def kv_spec(back, colblk):
        return pl.BlockSpec((qb, width), lambda i, b: (b * nb + jnp.maximum(i - back, 0), colblk))

    return pl.pallas_call(
        _attn_kernel,
        grid=(nb, batch),
        in_specs=[
            pl.BlockSpec((qb, width), lambda i, b: (b * nb + i, 0)),
            kv_spec(2, 1), kv_spec(1, 1), kv_spec(0, 1),
            kv_spec(2, 2), kv_spec(1, 2), kv_spec(0, 2),
            pl.BlockSpec((None, A_HEADS, qb, ATT_KBLKS * qb), lambda i, b: (jnp.minimum(i, ATT_KBLKS - 1), 0, 0, 0)),
        ],
        out_specs=pl.BlockSpec((qb, width), lambda i, b: (b * nb + i, 0)),
        out_shape=jax.ShapeDtypeStruct((n, width), BF16),
        name="chunk_attn",
        compiler_params=_cparams(("arbitrary", "arbitrary")),
    )(p_all, p_all, p_all, p_all, p_all, p_all, p_all, bias)


def _log_sigmoid(t):
    return jnp.minimum(t, 0.0) - jnp.log(1.0 + jnp.exp(-jnp.abs(t)))


def _mlstm_kernel(q_ref, k_ref, v_ref, o_ref, g_ref, cq_ref, ck_ref, nw_ref, out_ref,
                  qbuf, kbuf, ct_ref, n_ref, m_ref):
    c = pl.program_id(1)
    L, D = CHUNK, M_HEAD_DIM
    nb = q_ref.shape[0]
    tail = 16

    @pl.when(c == 0)
    def _():
        qbuf[:, 0:tail, :] = jnp.zeros((nb, tail, qbuf.shape[2]), BF16)
        kbuf[:, 0:tail, :] = jnp.zeros((nb, tail, kbuf.shape[2]), BF16)
        ct_ref[...] = jnp.zeros(ct_ref.shape, F32)
        n_ref[...] = jnp.zeros(n_ref.shape, F32)
        m_ref[...] = jnp.zeros(m_ref.shape, F32)

    row = lax.broadcasted_iota(jnp.int32, (L, L), 0)
    colm = lax.broadcasted_iota(jnp.int32, (L, L), 1)
    causal = colm <= row
    eye = colm == row
    upper = (row <= colm).astype(F32)

    def to_col(r):
        return jnp.sum(jnp.where(eye, jnp.broadcast_to(r, (L, L)), 0.0), axis=1, keepdims=True)

    srow = lax.broadcasted_iota(jnp.int32, ((CONV_W - 1) * L, tail + L), 0)
    scol = lax.broadcasted_iota(jnp.int32, ((CONV_W - 1) * L, tail + L), 1)
    stap = srow // L
    shifts = (scol == srow - stap * L + stap + (tail - (CONV_W - 1))).astype(BF16)

    def conv(buf, x_cur, w_ref):
        buf[tail:tail + L, :] = x_cur
        shifted = jnp.dot(shifts, buf[...], preferred_element_type=F32)
        acc = x_cur.astype(F32) * w_ref[CONV_W - 1:CONV_W, :]
        for j in range(CONV_W - 1):
            acc = acc + shifted[j * L:(j + 1) * L, :] * w_ref[j:j + 1, :]
        buf[0:tail, :] = buf[L:L + tail, :]
        return acc

    for bi in range(nb):
        q_all = _silu(conv(qbuf.at[bi], q_ref[bi], cq_ref)) * (D ** -0.5)
        k_all = _silu(conv(kbuf.at[bi], k_ref[bi], ck_ref))

        for h in range(M_HEADS):
            sl = slice(h * D, (h + 1) * D)
            q = q_all[:, sl]
            k = k_all[:, sl]
            qb16 = q.astype(BF16)
            kb16 = k.astype(BF16)
            vb16 = v_ref[bi, :, sl]
            ig = g_ref[bi, h:h + 1, :]
            lf = _log_sigmoid(g_ref[bi, M_HEADS + h:M_HEADS + h + 1, :])
            bcum = jnp.dot(jnp.broadcast_to(lf, (8, L)), upper, preferred_element_type=F32,
                           precision=lax.Precision.HIGHEST)[0:1, :]
            bcum_c = to_col(bcum)
            m_prev = m_ref[bi, h]
            logd = jnp.where(causal, bcum_c - bcum + ig, NEG_INF)
            inter = bcum_c + m_prev
            m_s = jnp.maximum(jnp.max(logd, axis=-1, keepdims=True), inter)
            s = lax.dot_general(qb16, kb16, (((1,), (1,)), ((), ())), preferred_element_type=F32)
            w_intra = s * jnp.exp(logd - m_s)
            w_inter = jnp.exp(inter - m_s)
            ct = ct_ref[bi, h]
            n_row = n_ref[bi, h]
            num = (jnp.dot(w_intra.astype(BF16), vb16, preferred_element_type=F32)
                   + w_inter * jnp.dot(qb16, ct.astype(BF16), preferred_element_type=F32))
            den = (jnp.sum(w_intra, axis=-1, keepdims=True)
                   + w_inter * jnp.sum(q * n_row, axis=-1, keepdims=True))
            hs = num / jnp.maximum(jnp.abs(den), jnp.exp(-m_s))
            b_last = bcum[:, L - 1:L]
            log_wk = b_last - bcum + ig
            m_new = jnp.maximum(b_last + m_prev, jnp.max(log_wk, axis=-1, keepdims=True))
            wk = jnp.exp(log_wk - m_new)
            decay = jnp.exp(b_last + m_prev - m_new)
            kw = k * to_col(wk)
            ct_ref[bi, h] = decay * ct + lax.dot_general(kw.astype(BF16), vb16, (((0,), (0,)), ((), ())),
                                                         preferred_element_type=F32)
            n_ref[bi, h] = decay * n_row + jnp.sum(kw, axis=0, keepdims=True)
            m_ref[bi, h] = m_new
            hm = _sigmoid(o_ref[bi, :, sl].astype(F32)) * hs
            y = hm * lax.rsqrt(jnp.mean(hm * hm, axis=-1, keepdims=True) + EPS) * nw_ref[:, sl]
            out_ref[bi, :, sl] = y.astype(out_ref.dtype)


def _mlstm(p_all, gates_t, conv_q, conv_k, norm_w, layer, batch, seq):
    n, cols = p_all.shape
    width = M_HEADS * M_HEAD_DIM
    nc = seq // CHUNK
    L = CHUNK
    nb = MLSTM_BATCH
    p3 = p_all.reshape(batch, seq, cols)

    def p_spec(colblk):
        return pl.BlockSpec((nb, L, width), lambda g, c: (g, c, colblk))

    out = pl.pallas_call(
        _mlstm_kernel,
        grid=(batch // nb, nc),
        in_specs=[
            p_spec(3), p_spec(4), p_spec(5), p_spec(6),
            pl.BlockSpec((nb, None, 2 * M_HEADS, L), lambda g, c: (g, c, 0, 0)),
            pl.BlockSpec((None, CONV_W, width), lambda g, c: (layer, 0, 0)),
            pl.BlockSpec((None, CONV_W, width), lambda g, c: (layer, 0, 0)),
            pl.BlockSpec((1, width), lambda g, c: (0, 0)),
        ],
        out_specs=pl.BlockSpec((nb, L, width), lambda g, c: (g, c, 0)),
        out_shape=jax.ShapeDtypeStruct((batch, seq, width), BF16),
        scratch_shapes=[
            pltpu.VMEM((nb, L + 16, width), BF16),
            pltpu.VMEM((nb, L + 16, width), BF16),
            pltpu.VMEM((nb, M_HEADS, M_HEAD_DIM, M_HEAD_DIM), F32),
            pltpu.VMEM((nb, M_HEADS, 1, M_HEAD_DIM), F32),
            pltpu.VMEM((nb, M_HEADS, 1, 1), F32),
        ],
        name="mlstm",
        compiler_params=_cparams(("arbitrary", "arbitrary")),
    )(p3, p3, p3, p3, gates_t, conv_q, conv_k, norm_w.reshape(1, width))
    return out.reshape(n, width)


def _merge_kernel(ya_ref, hm_ref, ga_ref, gm_ref, wa_ref, wm_ref, o_ref, wab_ref, wmb_ref):
    @pl.when(pl.program_id(1) == 0)
    def _():
        wab_ref[...] = wa_ref[...].astype(BF16)
        wmb_ref[...] = wm_ref[...].astype(BF16)

    a = jnp.dot(ya_ref[...], wab_ref[...], preferred_element_type=F32)
    m = jnp.dot(hm_ref[...], wmb_ref[...], preferred_element_type=F32)
    out = _sigmoid(ga_ref[...].astype(F32)) * a + _sigmoid(gm_ref[...].astype(F32)) * m
    o_ref[...] = out.astype(o_ref.dtype)


def _merge(y_attn, h_m, p_all, w_ba, w_bm, layer, ga_col0, gm_col0, tm=1024, tn=1024):
    n, ka = y_attn.shape
    km = h_m.shape[1]
    d = w_ba.shape[2]
    return pl.pallas_call(
        _merge_kernel,
        grid=(d // tn, n // tm),
        in_specs=[
            pl.BlockSpec((tm, ka), lambda j, i: (i, 0)),
            pl.BlockSpec((tm, km), lambda j, i: (i, 0)),
            pl.BlockSpec((tm, tn), lambda j, i: (i, ga_col0 // tn + j)),
            pl.BlockSpec((tm, tn), lambda j, i: (i, gm_col0 // tn + j)),
            pl.BlockSpec((None, ka, tn), lambda j, i: (layer, 0, j)),
            pl.BlockSpec((None, km, tn), lambda j, i: (layer, 0, j)),
        ],
        out_specs=pl.BlockSpec((tm, tn), lambda j, i: (i, j)),
        out_shape=jax.ShapeDtypeStruct((n, d), BF16),
        scratch_shapes=[pltpu.VMEM((ka, tn), BF16), pltpu.VMEM((km, tn), BF16)],
        name="branch_merge",
        compiler_params=_cparams(("arbitrary", "arbitrary")),
    )(y_attn, h_m, p_all, p_all, w_ba, w_bm)


def _out_kernel(a_ref, w_ref, x_ref, mod_ref, o_ref, wb_ref, *, gate_row):
    @pl.when(pl.program_id(1) == 0)
    def _():
        wb_ref[...] = w_ref[...].astype(BF16)

    y = jnp.dot(a_ref[...], wb_ref[...], preferred_element_type=F32)
    o_ref[...] = x_ref[...] + mod_ref[gate_row:gate_row + 1, :] * y


def _out_proj(a, w_stack, layer, x2, mod_l, seq, gate_row, tm=1024, tn=1024):
    n, k = a.shape
    d = x2.shape[1]
    tm = min(tm, seq)
    blocks_per_batch = seq // tm
    return pl.pallas_call(
        functools.partial(_out_kernel, gate_row=gate_row),
        grid=(d // tn, n // tm),
        in_specs=[
            pl.BlockSpec((tm, k), lambda j, i: (i, 0)),
            pl.BlockSpec((None, k, tn), lambda j, i: (layer, 0, j)),
            pl.BlockSpec((tm, tn), lambda j, i: (i, j)),
            pl.BlockSpec((None, 6, tn), lambda j, i: (i // blocks_per_batch, 0, j)),
        ],
        out_specs=pl.BlockSpec((tm, tn), lambda j, i: (i, j)),
        out_shape=jax.ShapeDtypeStruct((n, d), F32),
        scratch_shapes=[pltpu.VMEM((k, tn), BF16)],
        name="out_proj",
        compiler_params=_cparams(("arbitrary", "arbitrary")),
    )(a, w_stack, x2, mod_l)


def _moe_kernel(tok_ref, src_ref, be_ref, first_ref, nxt_ref, nact_ref, h_hbm, wg_hbm, wu_hbm, wd_hbm, o_ref,
                xbuf, xb16, wg_st, wu_st, wd_st, wg_b, wu_b, wd_b, xsem, wsem, *, layer):
    i = pl.program_id(0)
    nact = nact_ref[0]
    blk = xbuf.shape[0]
    stages = ((wg_hbm, wg_st, wg_b), (wu_hbm, wu_st, wu_b), (wd_hbm, wd_st, wd_b))

    def weight_copy(k, e):
        return pltpu.make_async_copy(stages[k][0].at[layer, e], stages[k][1], wsem.at[k])

    weight_queue = 1

    def start_gather(j):
        base = src_ref[j]
        for r in range(blk):
            tok = tok_ref[base + r]
            pltpu.make_async_copy(h_hbm.at[pl.ds(tok, 1)], xbuf.at[pl.ds(r, 1)], xsem.at[0]).start()

    def wait_gather():
        pltpu.make_async_copy(h_hbm.at[pl.ds(0, blk)], xbuf, xsem.at[0]).wait()

    def cast_stage(st, dst):
        rows = st.shape[0]
        step = 256

        def body(c, carry):
            r0 = pl.multiple_of(c * step, step)
            dst[pl.ds(r0, step), :] = st[pl.ds(r0, step), :].astype(BF16)
            return carry
        lax.fori_loop(0, rows // step, body, 0)

    @pl.when(i == 0)
    def _():
        for k in range(3):
            weight_copy(k, be_ref[0]).start(priority=weight_queue)
        start_gather(0)

    @pl.when(i < nact)
    def _():
        @pl.when(first_ref[i] == 1)
        def _():
            e_next = nxt_ref[i]
            for k in range(3):
                weight_copy(k, be_ref[i]).wait()
                cast_stage(stages[k][1], stages[k][2])

                @pl.when(e_next >= 0)
                def _(k=k):
                    weight_copy(k, e_next).start(priority=weight_queue)

        wait_gather()
        xb16[...] = xbuf[...].astype(BF16)
        start_gather(i + 1)
        x = xb16[...]
        g = jnp.dot(x, wg_b[...], preferred_element_type=F32)
        u = jnp.dot(x, wu_b[...], preferred_element_type=F32)
        a = (_silu(g) * u).astype(BF16)
        o_ref[...] = jnp.dot(a, wd_b[...], preferred_element_type=F32)

    @pl.when(i >= nact)
    def _():
        @pl.when(i == nact)
        def _():
            wait_gather()

        o_ref[...] = jnp.zeros(o_ref.shape, o_ref.dtype)


def _moe_experts(h2, tok_src, blk_src, blk_expert, blk_first, blk_next, n_active, w_gate, w_up, w_down, layer):
    n, d = h2.shape
    f = w_gate.shape[3]
    n_blocks = blk_src.shape[0]
    cap = n_blocks * MOE_BLK
    any_spec = pl.BlockSpec(memory_space=pl.ANY)
    grid_spec = pltpu.PrefetchScalarGridSpec(
        num_scalar_prefetch=6,
        grid=(n_blocks,),
        in_specs=[any_spec, any_spec, any_spec, any_spec],
        out_specs=pl.BlockSpec((MOE_BLK, d), lambda i, *_: (i, 0)),
        scratch_shapes=[
            pltpu.VMEM((MOE_BLK, d), F32),
            pltpu.VMEM((MOE_BLK, d), BF16),
            pltpu.VMEM((d, f), F32), pltpu.VMEM((d, f), F32), pltpu.VMEM((f, d), F32),
            pltpu.VMEM((d, f), BF16), pltpu.VMEM((d, f), BF16), pltpu.VMEM((f, d), BF16),
            pltpu.SemaphoreType.DMA((1,)),
            pltpu.SemaphoreType.DMA((3,)),
        ],
    )
    return pl.pallas_call(
        functools.partial(_moe_kernel, layer=layer),
        grid_spec=grid_spec,
        out_shape=jax.ShapeDtypeStruct((cap, d), F32),
        name="moe_experts",
        compiler_params=_cparams(("arbitrary",)),
    )(tok_src, blk_src, blk_expert, blk_first, blk_next, n_active, h2, w_gate, w_up, w_down)


def _combine_kernel(pos_ref, yb_hbm, x_ref, w_ref, mod_ref, o_ref, buf, sem, *, gate_row):
    i = pl.program_id(0)
    nsteps = pl.num_programs(0)
    t = x_ref.shape[0]

    def start(j, slot):
        base = j * (t * TOP_K)
        dst = buf.at[slot]
        for r in range(t):
            for k in range(TOP_K):
                p = pos_ref[base + r * TOP_K + k]
                pltpu.make_async_copy(yb_hbm.at[pl.ds(p, 1)], dst.at[k, pl.ds(r, 1)], sem.at[slot]).start()

    def wait(slot):
        for k in range(TOP_K):
            pltpu.make_async_copy(yb_hbm.at[pl.ds(0, t)], buf.at[slot, k], sem.at[slot]).wait()

    @pl.when(i == 0)
    def _():
        start(0, 0)

    slot = i % 2

    @pl.when(i + 1 < nsteps)
    def _():
        start(i + 1, 1 - slot)

    wait(slot)
    w = w_ref[...]
    y = w[:, 0:1] * buf[slot, 0] + w[:, 1:2] * buf[slot, 1]
    o_ref[...] = x_ref[...] + mod_ref[gate_row:gate_row + 1, :] * y


def _combine(yb, pos, weights, x2, mod_l, seq, gate_row, tm=256):
    n, d = x2.shape
    blocks_per_batch = seq // tm
    grid_spec = pltpu.PrefetchScalarGridSpec(
        num_scalar_prefetch=1,
        grid=(n // tm,),
        in_specs=[
            pl.BlockSpec(memory_space=pl.ANY),
            pl.BlockSpec((tm, d), lambda i, *_: (i, 0)),
            pl.BlockSpec((tm, TOP_K), lambda i, *_: (i, 0)),
            pl.BlockSpec((None, 6, d), lambda i, *_: (i // blocks_per_batch, 0, 0)),
        ],
        out_specs=pl.BlockSpec((tm, d), lambda i, *_: (i, 0)),
        scratch_shapes=[pltpu.VMEM((2, TOP_K, tm, d), F32), pltpu.SemaphoreType.DMA((2,))],
    )
    return pl.pallas_call(
        functools.partial(_combine_kernel, gate_row=gate_row),
        grid_spec=grid_spec,
        out_shape=jax.ShapeDtypeStruct((n, d), F32),
        name="moe_combine",
        compiler_params=_cparams(("arbitrary",)),
    )(pos.reshape(-1), yb, x2, weights, mod_l)


def _route(logits):
    n_tok = logits.shape[0]
    rows = jnp.arange(n_tok)
    coarse = logits[:, :N_GROUPS]
    grp = jnp.argmax(coarse, axis=-1)
    p_grp = jax.nn.softmax(coarse, axis=-1)[rows, grp]
    fine = logits[:, N_GROUPS:N_GROUPS + N_EXPERTS].reshape(n_tok, N_GROUPS, EXPERTS_PER_GROUP)
    top_val, top_idx = lax.top_k(fine[rows, grp], TOP_K)
    weights = p_grp[:, None] * jax.nn.softmax(top_val, axis=-1)
    expert = grp[:, None] * EXPERTS_PER_GROUP + top_idx
    return expert.astype(jnp.int32), weights


def _dispatch(expert):
    n_tok = expert.shape[0]
    n_assign = n_tok * TOP_K
    cap = n_assign + N_EXPERTS * MOE_BLK
    n_blocks = cap // MOE_BLK
    e_flat = expert.reshape(-1)
    onehot = (e_flat[:, None] == jnp.arange(N_EXPERTS, dtype=jnp.int32)[None, :]).astype(jnp.int32)
    cum = jnp.cumsum(onehot, axis=0)
    counts = cum[-1]
    rank = jnp.sum(onehot * (cum - 1), axis=1)
    padded = ((counts + MOE_BLK - 1) // MOE_BLK) * MOE_BLK
    pad_ends = jnp.cumsum(padded)
    pad_starts = pad_ends - padded
    dest = (jnp.sum(onehot * pad_starts[None, :], axis=1) + rank).astype(jnp.int32)
    order = jnp.argsort(e_flat, stable=True)
    tok_src = jnp.concatenate([(order // TOP_K).astype(jnp.int32), jnp.zeros((MOE_BLK,), jnp.int32)])
    starts = jnp.cumsum(counts) - counts
    blk_start = jnp.arange(n_blocks, dtype=jnp.int32) * MOE_BLK
    blk_expert = jnp.minimum(jnp.sum((pad_ends[None, :] <= blk_start[:, None]).astype(jnp.int32), axis=1),
                             N_EXPERTS - 1).astype(jnp.int32)
    blk_src = jnp.clip(blk_start - (pad_starts - starts)[blk_expert], 0, n_assign).astype(jnp.int32)
    n_active = (pad_ends[-1] // MOE_BLK).astype(jnp.int32)
    prev = jnp.concatenate([jnp.full((1,), -1, jnp.int32), blk_expert[:-1]])
    blk_first = (blk_expert != prev).astype(jnp.int32)
    run_end = pad_ends[blk_expert] // MOE_BLK
    blk_next = jnp.where(run_end < n_active, blk_expert[jnp.minimum(run_end, n_blocks - 1)], -1).astype(jnp.int32)
    return tok_src, blk_src, blk_expert, blk_first, blk_next, n_active.reshape(1), dest.reshape(n_tok, TOP_K)


def kernel(x, c, ada_w, ada_b, norm1_w, norm2_w, w_in, conv_q, conv_k, igate_b, fgate_b, rel_bias,
           mlstm_norm_w, w_branch_attn, w_branch_mlstm, w_out, router_coarse_w, router_coarse_b,
           router_fine_w, router_fine_b, w_gate, w_up, w_down, final_norm_w):
    b, s, d = x.shape
    depth = ada_w.shape[0]
    n = b * s
    nc = s // CHUNK
    a_width = A_HEADS * A_HEAD_DIM
    m_width = M_HEADS * M_HEAD_DIM
    main_cols = 3 * a_width + 4 * m_width + 2 * d
    ga_col0 = 3 * a_width + 4 * m_width
    gm_col0 = ga_col0 + d

    mod = _ada_mod(c, ada_w, ada_b)
    x2 = x.reshape(n, d)

    w_in_t = jnp.swapaxes(w_in, 1, 2)

    def gate_spec(l):
        return pl.BlockSpec((None, LANES, d), lambda i: (l, main_cols // LANES, 0))

    router_spec = pl.BlockSpec((LANES, d), lambda i: (0, 0))

    for l in range(depth):
        b_g = jnp.zeros((1, LANES), F32).at[0, :M_HEADS].set(igate_b[l]).at[0, M_HEADS:2 * M_HEADS].set(fgate_b[l])
        w_r = (jnp.zeros((LANES, d), F32).at[:N_GROUPS].set(router_coarse_w[l].T)
               .at[N_GROUPS:N_GROUPS + N_EXPERTS].set(router_fine_w[l].T))
        b_r = (jnp.zeros((1, LANES), F32).at[0, :N_GROUPS].set(router_coarse_b[l])
               .at[0, N_GROUPS:N_GROUPS + N_EXPERTS].set(router_fine_b[l]))

        h, gates = _norm_mod(x2, norm1_w[l], mod[l], w_in_t, gate_spec(l), 2 * M_HEADS, b_g, seq=s,
                             shift_row=0, scale_row=1, precise=False, out_dtype=BF16)
        p_all = _proj(h, w_in_t, l, main_cols)
        y_attn = _attention(p_all, _attn_bias(rel_bias[l]), b, s)
        gates_t = gates[:, :2 * M_HEADS].reshape(b, nc, CHUNK, 2 * M_HEADS).transpose(0, 1, 3, 2)
        h_m = _mlstm(p_all, gates_t, conv_q, conv_k, mlstm_norm_w[l], l, b, s)
        merged = _merge(y_attn, h_m, p_all, w_branch_attn, w_branch_mlstm, l, ga_col0, gm_col0)
        x2 = _out_proj(merged, w_out, l, x2, mod[l], s, gate_row=2)

        h2, logits = _norm_mod(x2, norm2_w[l], mod[l], w_r, router_spec, N_GROUPS + N_EXPERTS, b_r, seq=s,
                               shift_row=3, scale_row=4, precise=True, out_dtype=F32)
        expert, weights = _route(logits)
        tok_src, blk_src, blk_expert, blk_first, blk_next, n_active, pos = _dispatch(expert)
        yb = _moe_experts(h2, tok_src, blk_src, blk_expert, blk_first, blk_next, n_active, w_gate, w_up, w_down, l)
        x2 = _combine(yb, pos, weights, x2, mod[l], s, gate_row=5)

    return _final_norm(x2, final_norm_w).reshape(b, s, d)
```

```python
import functools

import jax
import jax.numpy as jnp
from jax import lax
from jax.experimental import pallas as pl
from jax.experimental.pallas import tpu as pltpu

F32 = jnp.float32
BF16 = jnp.bfloat16

EPS = 1e-6
NEG_INF = -1e30
LOG2E = 1.4426950408889634
CHUNK = 64
LEFT_CHUNKS = 8
REL_CLIP = 256
A_HEADS = 8
A_HEAD_DIM = 128
M_HEADS = 4
M_HEAD_DIM = 256
CONV_W = 4
N_GROUPS = 4
EXPERTS_PER_GROUP = 8
N_EXPERTS = N_GROUPS * EXPERTS_PER_GROUP
TOP_K = 2

LANES = 128
VMEM_LIMIT = 60 * 1024 * 1024

ATT_QBLK = 256
ATT_KBLKS = 3
MLSTM_BATCH = 4
MOE_BLK = 256


def _cparams(sem):
    return pltpu.CompilerParams(dimension_semantics=sem, vmem_limit_bytes=VMEM_LIMIT)


def _sigmoid(t):
    return 1.0 / (1.0 + jnp.exp(-t))


def _silu(t):
    return t * _sigmoid(t)


def _ada_kernel(c_ref, w_ref, b_ref, o_ref):
    w = w_ref[...].astype(BF16)
    r = jnp.dot(c_ref[...], w, preferred_element_type=F32)
    bp = o_ref.shape[0]
    o_ref[...] = r[:bp] + r[bp:] + b_ref[...]


def _ada_mod(c, ada_w, ada_b):
    depth, d, n6 = ada_w.shape
    b = c.shape[0]
    bp = 8
    c_pad = jnp.zeros((bp, d), F32).at[:b].set(c)
    c_hi = c_pad.astype(BF16)
    c_lo = (c_pad - c_hi.astype(F32)).astype(BF16)
    c2 = jnp.concatenate([c_hi, c_lo], axis=0)
    tn = 1024
    out = pl.pallas_call(
        _ada_kernel,
        grid=(depth, n6 // tn),
        in_specs=[
            pl.BlockSpec((2 * bp, d), lambda l, j: (0, 0)),
            pl.BlockSpec((None, d, tn), lambda l, j: (l, 0, j)),
            pl.BlockSpec((None, 1, tn), lambda l, j: (l, 0, j)),
        ],
        out_specs=pl.BlockSpec((None, bp, tn), lambda l, j: (l, 0, j)),
        out_shape=jax.ShapeDtypeStruct((depth, bp, n6), F32),
        name="ada_mod",
        compiler_params=_cparams(("arbitrary", "arbitrary")),
    )(c2, ada_w, ada_b.reshape(depth, 1, n6))
    return out[:, :b].reshape(depth, b, 6, d)


def _norm_kernel(x_ref, nw_ref, mod_ref, ws_ref, bs_ref, h_ref, s_ref, *, shift_row, scale_row, precise,
                 side_cols):
    x = x_ref[...]
    y = x * lax.rsqrt(jnp.mean(x * x, axis=-1, keepdims=True) + EPS)
    y = y * nw_ref[...]
    h = y * (1.0 + mod_ref[scale_row:scale_row + 1, :]) + mod_ref[shift_row:shift_row + 1, :]
    h_ref[...] = h.astype(h_ref.dtype)
    wrow = lax.broadcasted_iota(jnp.int32, ws_ref.shape, 0)
    ws = jnp.where(wrow < side_cols, ws_ref[...], 0.0)
    nt = (((1,), (1,)), ((), ()))
    if precise:
        h_hi = h.astype(BF16)
        h_lo = (h - h_hi.astype(F32)).astype(BF16)
        w_hi = ws.astype(BF16)
        w_lo = (ws - w_hi.astype(F32)).astype(BF16)
        s = (lax.dot_general(h_hi, w_hi, nt, preferred_element_type=F32)
             + lax.dot_general(h_hi, w_lo, nt, preferred_element_type=F32)
             + lax.dot_general(h_lo, w_hi, nt, preferred_element_type=F32))
    else:
        s = lax.dot_general(h.astype(BF16), ws.astype(BF16), nt, preferred_element_type=F32)
    s_ref[...] = s + bs_ref[...]


def _norm_mod(x2, norm_w, mod_l, w_side, side_spec, side_cols, b_side, *, seq, shift_row, scale_row, precise,
              out_dtype, tm=256):
    n, d = x2.shape
    blocks_per_batch = seq // tm
    kern = functools.partial(_norm_kernel, shift_row=shift_row, scale_row=scale_row, precise=precise,
                             side_cols=side_cols)
    return pl.pallas_call(
        kern,
        grid=(n // tm,),
        in_specs=[
            pl.BlockSpec((tm, d), lambda i: (i, 0)),
            pl.BlockSpec((1, d), lambda i: (0, 0)),
            pl.BlockSpec((None, 6, d), lambda i: (i // blocks_per_batch, 0, 0)),
            side_spec,
            pl.BlockSpec((1, LANES), lambda i: (0, 0)),
        ],
        out_specs=[
            pl.BlockSpec((tm, d), lambda i: (i, 0)),
            pl.BlockSpec((tm, LANES), lambda i: (i, 0)),
        ],
        out_shape=[jax.ShapeDtypeStruct((n, d), out_dtype), jax.ShapeDtypeStruct((n, LANES), F32)],
        name="norm_mod",
        compiler_params=_cparams(("arbitrary",)),
    )(x2, norm_w.reshape(1, d), mod_l, w_side, b_side)


def _final_norm_kernel(x_ref, nw_ref, o_ref):
    x = x_ref[...]
    y = x * lax.rsqrt(jnp.mean(x * x, axis=-1, keepdims=True) + EPS)
    o_ref[...] = y * nw_ref[...]


def _final_norm(x2, norm_w, tm=256):
    n, d = x2.shape
    return pl.pallas_call(
        _final_norm_kernel,
        grid=(n // tm,),
        in_specs=[pl.BlockSpec((tm, d), lambda i: (i, 0)), pl.BlockSpec((1, d), lambda i: (0, 0))],
        out_specs=pl.BlockSpec((tm, d), lambda i: (i, 0)),
        out_shape=jax.ShapeDtypeStruct((n, d), F32),
        name="final_norm",
        compiler_params=_cparams(("arbitrary",)),
    )(x2, norm_w.reshape(1, d))


def _proj_kernel(a_ref, wt_ref, o_ref, wb_ref):
    @pl.when(pl.program_id(1) == 0)
    def _():
        wb_ref[...] = wt_ref[...].astype(BF16)

    o_ref[...] = lax.dot_general(a_ref[...], wb_ref[...], (((1,), (1,)), ((), ())),
                                 preferred_element_type=F32).astype(o_ref.dtype)


def _proj(a, wt_stack, layer, n_cols, tm=1024, tn=1024):
    m, k = a.shape
    return pl.pallas_call(
        _proj_kernel,
        grid=(n_cols // tn, m // tm),
        in_specs=[
            pl.BlockSpec((tm, k), lambda j, i: (i, 0)),
            pl.BlockSpec((None, tn, k), lambda j, i: (layer, j, 0)),
        ],
        out_specs=pl.BlockSpec((tm, tn), lambda j, i: (i, j)),
        out_shape=jax.ShapeDtypeStruct((m, n_cols), BF16),
        scratch_shapes=[pltpu.VMEM((tn, k), BF16)],
        name="in_proj",
        compiler_params=_cparams(("arbitrary", "arbitrary")),
    )(a, wt_stack)


def _attn_kernel(q_ref, k0_ref, k1_ref, k2_ref, v0_ref, v1_ref, v2_ref, bias_ref, o_ref):
    k_refs = (k0_ref, k1_ref, k2_ref)
    v_refs = (v0_ref, v1_ref, v2_ref)
    qb = q_ref.shape[0]
    half = qb // 2
    scale2 = (A_HEAD_DIM ** -0.5) * LOG2E
    heads = [slice(h * A_HEAD_DIM, (h + 1) * A_HEAD_DIM) for h in range(A_HEADS)]
    nt = (((1,), (1,)), ((), ()))
    for part in range(2):
        r0 = part * half
        c0 = part * half
        c1 = c0 + ATT_KBLKS * qb - half
        spans = [(max(c0, j * qb) - j * qb, min(c1, (j + 1) * qb) - j * qb) for j in range(ATT_KBLKS)]
        s = jnp.stack([
            jnp.concatenate([lax.dot_general(q_ref[r0:r0 + half, sl], k_refs[j][lo:hi, sl], nt,
                                             preferred_element_type=F32)
                             for j, (lo, hi) in enumerate(spans)], axis=1)
            for sl in heads])
        s = s * scale2 + bias_ref[:, r0:r0 + half, c0:c1]
        m = jnp.max(s, axis=-1, keepdims=True)
        e = jnp.exp2(s - m)
        denom = jnp.sum(e, axis=-1, keepdims=True)
        p = e.astype(BF16)
        for h, sl in enumerate(heads):
            acc = None
            off = 0
            for j, (lo, hi) in enumerate(spans):
                term = jnp.dot(p[h, :, off:off + hi - lo], v_refs[j][lo:hi, sl], preferred_element_type=F32)
                acc = term if acc is None else acc + term
                off += hi - lo
            o_ref[r0:r0 + half, sl] = (acc / denom[h]).astype(o_ref.dtype)


def _attn_bias(rel_table):
    qb, kw = ATT_QBLK, ATT_KBLKS * ATT_QBLK
    nh = rel_table.shape[0]
    qi = jnp.arange(qb)[:, None]
    kj = jnp.arange(kw)[None, :]
    off = kw - 1 - (ATT_KBLKS - 1) * qb
    glen = qb + kw
    n_lo = max(0, min(glen, off - REL_CLIP))
    n_lin = max(0, min(glen, off + REL_CLIP + 1) - n_lo)
    n_hi = glen - n_lo - n_lin
    lin0 = n_lo - off + REL_CLIP
    gr = jnp.concatenate([jnp.broadcast_to(rel_table[:, 2 * REL_CLIP:], (nh, n_hi)),
                          rel_table[:, lin0:lin0 + n_lin][:, ::-1],
                          jnp.broadcast_to(rel_table[:, :1], (nh, n_lo))], axis=1).astype(F32) * LOG2E
    c0 = glen - kw
    bias = jnp.tile(gr, (1, qb + 1))[:, c0:c0 + qb * (glen - 1)].reshape(nh, qb, glen - 1)[:, :, :kw]
    qc = qi // CHUNK + (ATT_KBLKS - 1) * (qb // CHUNK)
    kc = kj // CHUNK
    band = (kc <= qc) & (kc >= qc - LEFT_CHUNKS)
    tables = []
    for t in range(ATT_KBLKS):
        ok = band & (kj >= (ATT_KBLKS - 1 - t) * qb)
        tables.append(jnp.where(ok[None], bias, NEG_INF))
    return jnp.stack(tables)


def _attention(p_all, bias, batch, seq):
    n = p_all.shape[0]
    width = A_HEADS * A_HEAD_DIM
    qb = ATT_QBLK
    nb = seq // qb

    def kv_spec(back, colblk):
        return pl.BlockSpec((qb, width), lambda i, b: (b * nb + jnp.maximum(i - back, 0), colblk))

    return pl.pallas_call(
        _attn_kernel,
        grid=(nb, batch),
        in_specs=[
            pl.BlockSpec((qb, width), lambda i, b: (b * nb + i, 0)),
            kv_spec(2, 1), kv_spec(1, 1), kv_spec(0, 1),
            kv_spec(2, 2), kv_spec(1, 2), kv_spec(0, 2),
            pl.BlockSpec((None, A_HEADS, qb, ATT_KBLKS * qb), lambda i, b: (jnp.minimum(i, ATT_KBLKS - 1), 0, 0, 0)),
        ],
        out_specs=pl.BlockSpec((qb, width), lambda i, b: (b * nb + i, 0)),
        out_shape=jax.ShapeDtypeStruct((n, width), BF16),
        name="chunk_attn",
        compiler_params=_cparams(("arbitrary", "arbitrary")),
    )(p_all, p_all, p_all, p_all, p_all, p_all, p_all, bias)


def _log_sigmoid(t):
    return jnp.minimum(t, 0.0) - jnp.log(1.0 + jnp.exp(-jnp.abs(t)))


def _mlstm_kernel(q_ref, k_ref, v_ref, o_ref, g_ref, cq_ref, ck_ref, nw_ref, out_ref,
                  qbuf, kbuf, ct_ref, n_ref, m_ref):
    c = pl.program_id(1)
    L, D = CHUNK, M_HEAD_DIM
    nb = q_ref.shape[0]
    ns = nb * M_HEADS
    tail = 16

    @pl.when(c == 0)
    def _():
        qbuf[:, 0:tail, :] = jnp.zeros((nb, tail, qbuf.shape[2]), BF16)
        kbuf[:, 0:tail, :] = jnp.zeros((nb, tail, kbuf.shape[2]), BF16)
        ct_ref[...] = jnp.zeros(ct_ref.shape, F32)
        n_ref[...] = jnp.zeros(n_ref.shape, F32)
        m_ref[...] = jnp.zeros(m_ref.shape, F32)

    row = lax.broadcasted_iota(jnp.int32, (L, L), 0)
    colm = lax.broadcasted_iota(jnp.int32, (L, L), 1)
    causal = colm <= row
    eye = colm == row
    upper = (row <= colm).astype(F32)

    def to_col(r):
        return jnp.sum(jnp.where(eye, jnp.broadcast_to(r, (ns, L, L)), 0.0), axis=-1, keepdims=True)

    srow = lax.broadcasted_iota(jnp.int32, ((CONV_W - 1) * L, tail + L), 0)
    scol = lax.broadcasted_iota(jnp.int32, ((CONV_W - 1) * L, tail + L), 1)
    stap = srow // L
    shifts = (scol == srow - stap * L + stap + (tail - (CONV_W - 1))).astype(BF16)

    def conv(buf, x_ref, w_ref):
        shifted = []
        for bi in range(nb):
            buf[bi, tail:tail + L, :] = x_ref[bi]
            shifted.append(jnp.dot(shifts, buf[bi], preferred_element_type=F32))
            buf[bi, 0:tail, :] = buf[bi, L:L + tail, :]
        shifted = jnp.stack(shifted)
        acc = x_ref[...].astype(F32) * w_ref[CONV_W - 1:CONV_W, :]
        for j in range(CONV_W - 1):
            acc = acc + shifted[:, j * L:(j + 1) * L, :] * w_ref[j:j + 1, :]
        return acc

    def streams(x):
        return jnp.stack([x[bi, :, h * D:(h + 1) * D] for bi in range(nb) for h in range(M_HEADS)])

    q = streams(_silu(conv(qbuf, q_ref, cq_ref)) * (D ** -0.5))
    k = streams(_silu(conv(kbuf, k_ref, ck_ref)))
    qb16 = q.astype(BF16)
    kb16 = k.astype(BF16)
    v16 = [v_ref[bi, :, h * D:(h + 1) * D] for bi in range(nb) for h in range(M_HEADS)]

    g = g_ref[...]
    ig2 = jnp.concatenate([g[bi, 0:M_HEADS, :] for bi in range(nb)], axis=0)
    lf2 = _log_sigmoid(jnp.concatenate([g[bi, M_HEADS:2 * M_HEADS, :] for bi in range(nb)], axis=0))
    bcum2 = jnp.dot(lf2, upper, preferred_element_type=F32, precision=lax.Precision.HIGHEST)
    ig = jnp.stack([ig2[i:i + 1, :] for i in range(ns)])
    bcum = jnp.stack([bcum2[i:i + 1, :] for i in range(ns)])
    bcum_c = to_col(bcum)
    m_prev = m_ref[...].reshape(ns, 1, 1)

    logd = jnp.where(causal, bcum_c - bcum + ig, NEG_INF)
    inter = bcum_c + m_prev
    m_s = jnp.maximum(jnp.max(logd, axis=-1, keepdims=True), inter)
    nt = (((1,), (1,)), ((), ()))
    s = jnp.stack([lax.dot_general(qb16[i], kb16[i], nt, preferred_element_type=F32) for i in range(ns)])
    w_intra = s * jnp.exp(logd - m_s)
    w_inter = jnp.exp(inter - m_s)
    ct = ct_ref[...].reshape(ns, D, D)
    n_row = n_ref[...].reshape(ns, 1, D)
    wi16 = w_intra.astype(BF16)
    ct16 = ct.astype(BF16)
    num_intra = jnp.stack([jnp.dot(wi16[i], v16[i], preferred_element_type=F32) for i in range(ns)])
    num_inter = jnp.stack([jnp.dot(qb16[i], ct16[i], preferred_element_type=F32) for i in range(ns)])
    num = num_intra + w_inter * num_inter
    den = (jnp.sum(w_intra, axis=-1, keepdims=True)
           + w_inter * jnp.sum(q * n_row, axis=-1, keepdims=True))
    hs = num / jnp.maximum(jnp.abs(den), jnp.exp(-m_s))

    b_last = bcum[:, :, L - 1:L]
    log_wk = b_last - bcum + ig
    m_new = jnp.maximum(b_last + m_prev, jnp.max(log_wk, axis=-1, keepdims=True))
    wk = jnp.exp(log_wk - m_new)
    decay = jnp.exp(b_last + m_prev - m_new)
    kw = k * to_col(wk)
    kw16 = kw.astype(BF16)
    tn = (((0,), (0,)), ((), ()))
    upd = jnp.stack([lax.dot_general(kw16[i], v16[i], tn, preferred_element_type=F32) for i in range(ns)])
    ct_ref[...] = (decay * ct + upd).reshape(ct_ref.shape)
    n_ref[...] = (decay * n_row + jnp.sum(kw, axis=1, keepdims=True)).reshape(n_ref.shape)
    m_ref[...] = m_new.reshape(m_ref.shape)

    og = jnp.stack([o_ref[bi, :, h * D:(h + 1) * D] for bi in range(nb) for h in range(M_HEADS)]).astype(F32)
    nw = jnp.stack([nw_ref[:, h * D:(h + 1) * D] for _ in range(nb) for h in range(M_HEADS)])
    hm = _sigmoid(og) * hs
    y = (hm * lax.rsqrt(jnp.mean(hm * hm, axis=-1, keepdims=True) + EPS) * nw).astype(out_ref.dtype)
    for bi in range(nb):
        for h in range(M_HEADS):
            out_ref[bi, :, h * D:(h + 1) * D] = y[bi * M_HEADS + h]


def _mlstm(p_all, gates_t, conv_q, conv_k, norm_w, layer, batch, seq):
    n, cols = p_all.shape
    width = M_HEADS * M_HEAD_DIM
    nc = seq // CHUNK
    L = CHUNK
    nb = MLSTM_BATCH
    p3 = p_all.reshape(batch, seq, cols)

    def p_spec(colblk):
        return pl.BlockSpec((nb, L, width), lambda g, c: (g, c, colblk))

    out = pl.pallas_call(
        _mlstm_kernel,
        grid=(batch // nb, nc),
        in_specs=[
            p_spec(3), p_spec(4), p_spec(5), p_spec(6),
            pl.BlockSpec((nb, None, 2 * M_HEADS, L), lambda g, c: (g, c, 0, 0)),
            pl.BlockSpec((None, CONV_W, width), lambda g, c: (layer, 0, 0)),
            pl.BlockSpec((None, CONV_W, width), lambda g, c: (layer, 0, 0)),
            pl.BlockSpec((1, width), lambda g, c: (0, 0)),
        ],
        out_specs=pl.BlockSpec((nb, L, width), lambda g, c: (g, c, 0)),
        out_shape=jax.ShapeDtypeStruct((batch, seq, width), BF16),
        scratch_shapes=[
            pltpu.VMEM((nb, L + 16, width), BF16),
            pltpu.VMEM((nb, L + 16, width), BF16),
            pltpu.VMEM((nb, M_HEADS, M_HEAD_DIM, M_HEAD_DIM), F32),
            pltpu.VMEM((nb, M_HEADS, 1, M_HEAD_DIM), F32),
            pltpu.VMEM((nb, M_HEADS, 1, 1), F32),
        ],
        name="mlstm",
        compiler_params=_cparams(("arbitrary", "arbitrary")),
    )(p3, p3, p3, p3, gates_t, conv_q, conv_k, norm_w.reshape(1, width))
    return out.reshape(n, width)


def _merge_kernel(ya_ref, hm_ref, ga_ref, gm_ref, wa_ref, wm_ref, o_ref, wab_ref, wmb_ref):
    @pl.when(pl.program_id(1) == 0)
    def _():
        wab_ref[...] = wa_ref[...].astype(BF16)
        wmb_ref[...] = wm_ref[...].astype(BF16)

    a = jnp.dot(ya_ref[...], wab_ref[...], preferred_element_type=F32)
    m = jnp.dot(hm_ref[...], wmb_ref[...], preferred_element_type=F32)
    out = _sigmoid(ga_ref[...].astype(F32)) * a + _sigmoid(gm_ref[...].astype(F32)) * m
    o_ref[...] = out.astype(o_ref.dtype)


def _merge(y_attn, h_m, p_all, w_ba, w_bm, layer, ga_col0, gm_col0, tm=1024, tn=1024):
    n, ka = y_attn.shape
    km = h_m.shape[1]
    d = w_ba.shape[2]
    return pl.pallas_call(
        _merge_kernel,
        grid=(d // tn, n // tm),
        in_specs=[
            pl.BlockSpec((tm, ka), lambda j, i: (i, 0)),
            pl.BlockSpec((tm, km), lambda j, i: (i, 0)),
            pl.BlockSpec((tm, tn), lambda j, i: (i, ga_col0 // tn + j)),
            pl.BlockSpec((tm, tn), lambda j, i: (i, gm_col0 // tn + j)),
            pl.BlockSpec((None, ka, tn), lambda j, i: (layer, 0, j)),
            pl.BlockSpec((None, km, tn), lambda j, i: (layer, 0, j)),
        ],
        out_specs=pl.BlockSpec((tm, tn), lambda j, i: (i, j)),
        out_shape=jax.ShapeDtypeStruct((n, d), BF16),
        scratch_shapes=[pltpu.VMEM((ka, tn), BF16), pltpu.VMEM((km, tn), BF16)],
        name="branch_merge",
        compiler_params=_cparams(("arbitrary", "arbitrary")),
    )(y_attn, h_m, p_all, p_all, w_ba, w_bm)


def _out_kernel(a_ref, w_ref, x_ref, mod_ref, o_ref, wb_ref, *, gate_row):
    @pl.when(pl.program_id(1) == 0)
    def _():
        wb_ref[...] = w_ref[...].astype(BF16)

    y = jnp.dot(a_ref[...], wb_ref[...], preferred_element_type=F32)
    o_ref[...] = x_ref[...] + mod_ref[gate_row:gate_row + 1, :] * y


def _out_proj(a, w_stack, layer, x2, mod_l, seq, gate_row, tm=1024, tn=1024):
    n, k = a.shape
    d = x2.shape[1]
    tm = min(tm, seq)
    blocks_per_batch = seq // tm
    return pl.pallas_call(
        functools.partial(_out_kernel, gate_row=gate_row),
        grid=(d // tn, n // tm),
        in_specs=[
            pl.BlockSpec((tm, k), lambda j, i: (i, 0)),
            pl.BlockSpec((None, k, tn), lambda j, i: (layer, 0, j)),
            pl.BlockSpec((tm, tn), lambda j, i: (i, j)),
            pl.BlockSpec((None, 6, tn), lambda j, i: (i // blocks_per_batch, 0, j)),
        ],
        out_specs=pl.BlockSpec((tm, tn), lambda j, i: (i, j)),
        out_shape=jax.ShapeDtypeStruct((n, d), F32),
        scratch_shapes=[pltpu.VMEM((k, tn), BF16)],
        name="out_proj",
        compiler_params=_cparams(("arbitrary", "arbitrary")),
    )(a, w_stack, x2, mod_l)


def _moe_kernel(tok_ref, src_ref, be_ref, first_ref, nxt_ref, nact_ref, h_hbm, wg_hbm, wu_hbm, wd_hbm, o_ref,
                xbuf, xb16, wg_st, wu_st, wd_st, wg_b, wu_b, wd_b, xsem, wsem, *, layer):
    i = pl.program_id(0)
    nact = nact_ref[0]
    blk = xbuf.shape[0]
    stages = ((wg_hbm, wg_st, wg_b), (wu_hbm, wu_st, wu_b), (wd_hbm, wd_st, wd_b))

    def weight_copy(k, e):
        return pltpu.make_async_copy(stages[k][0].at[layer, e], stages[k][1], wsem.at[k])

    weight_queue = 1

    def start_gather(j):
        base = src_ref[j]
        for r in range(blk):
            tok = tok_ref[base + r]
            pltpu.make_async_copy(h_hbm.at[pl.ds(tok, 1)], xbuf.at[pl.ds(r, 1)], xsem.at[0]).start()

    def wait_gather():
        pltpu.make_async_copy(h_hbm.at[pl.ds(0, blk)], xbuf, xsem.at[0]).wait()

    def cast_stage(st, dst):
        rows = st.shape[0]
        step = 256

        def body(c, carry):
            r0 = pl.multiple_of(c * step, step)
            dst[pl.ds(r0, step), :] = st[pl.ds(r0, step), :].astype(BF16)
            return carry
        lax.fori_loop(0, rows // step, body, 0)

    @pl.when(i == 0)
    def _():
        for k in range(3):
            weight_copy(k, be_ref[0]).start(priority=weight_queue)
        start_gather(0)

    @pl.when(i < nact)
    def _():
        @pl.when(first_ref[i] == 1)
        def _():
            e_next = nxt_ref[i]
            for k in range(3):
                weight_copy(k, be_ref[i]).wait()
                cast_stage(stages[k][1], stages[k][2])

                @pl.when(e_next >= 0)
                def _(k=k):
                    weight_copy(k, e_next).start(priority=weight_queue)

        wait_gather()
        xb16[...] = xbuf[...].astype(BF16)
        start_gather(i + 1)
        x = xb16[...]
        g = jnp.dot(x, wg_b[...], preferred_element_type=F32)
        u = jnp.dot(x, wu_b[...], preferred_element_type=F32)
        a = (_silu(g) * u).astype(BF16)
        o_ref[...] = jnp.dot(a, wd_b[...], preferred_element_type=F32)

    @pl.when(i >= nact)
    def _():
        @pl.when(i == nact)
        def _():
            wait_gather()

        o_ref[...] = jnp.zeros(o_ref.shape, o_ref.dtype)


def _moe_experts(h2, tok_src, blk_src, blk_expert, blk_first, blk_next, n_active, w_gate, w_up, w_down, layer):
    n, d = h2.shape
    f = w_gate.shape[3]
    n_blocks = blk_src.shape[0]
    cap = n_blocks * MOE_BLK
    any_spec = pl.BlockSpec(memory_space=pl.ANY)
    grid_spec = pltpu.PrefetchScalarGridSpec(
        num_scalar_prefetch=6,
        grid=(n_blocks,),
        in_specs=[any_spec, any_spec, any_spec, any_spec],
        out_specs=pl.BlockSpec((MOE_BLK, d), lambda i, *_: (i, 0)),
        scratch_shapes=[
            pltpu.VMEM((MOE_BLK, d), F32),
            pltpu.VMEM((MOE_BLK, d), BF16),
            pltpu.VMEM((d, f), F32), pltpu.VMEM((d, f), F32), pltpu.VMEM((f, d), F32),
            pltpu.VMEM((d, f), BF16), pltpu.VMEM((d, f), BF16), pltpu.VMEM((f, d), BF16),
            pltpu.SemaphoreType.DMA((1,)),
            pltpu.SemaphoreType.DMA((3,)),
        ],
    )
    return pl.pallas_call(
        functools.partial(_moe_kernel, layer=layer),
        grid_spec=grid_spec,
        out_shape=jax.ShapeDtypeStruct((cap, d), F32),
        name="moe_experts",
        compiler_params=_cparams(("arbitrary",)),
    )(tok_src, blk_src, blk_expert, blk_first, blk_next, n_active, h2, w_gate, w_up, w_down)


def _combine_kernel(pos_ref, yb_hbm, x_ref, w_ref, mod_ref, o_ref, buf, sem, *, gate_row):
    i = pl.program_id(0)
    nsteps = pl.num_programs(0)
    t = x_ref.shape[0]

    def start(j, slot):
        base = j * (t * TOP_K)
        dst = buf.at[slot]
        for r in range(t):
            for k in range(TOP_K):
                p = pos_ref[base + r * TOP_K + k]
                pltpu.make_async_copy(yb_hbm.at[pl.ds(p, 1)], dst.at[k, pl.ds(r, 1)], sem.at[slot]).start()

    def wait(slot):
        for k in range(TOP_K):
            pltpu.make_async_copy(yb_hbm.at[pl.ds(0, t)], buf.at[slot, k], sem.at[slot]).wait()

    @pl.when(i == 0)
    def _():
        start(0, 0)

    slot = i % 2

    @pl.when(i + 1 < nsteps)
    def _():
        start(i + 1, 1 - slot)

    wait(slot)
    w = w_ref[...]
    y = w[:, 0:1] * buf[slot, 0] + w[:, 1:2] * buf[slot, 1]
    o_ref[...] = x_ref[...] + mod_ref[gate_row:gate_row + 1, :] * y


def _combine(yb, pos, weights, x2, mod_l, seq, gate_row, tm=256):
    n, d = x2.shape
    blocks_per_batch = seq // tm
    grid_spec = pltpu.PrefetchScalarGridSpec(
        num_scalar_prefetch=1,
        grid=(n // tm,),
        in_specs=[
            pl.BlockSpec(memory_space=pl.ANY),
            pl.BlockSpec((tm, d), lambda i, *_: (i, 0)),
            pl.BlockSpec((tm, TOP_K), lambda i, *_: (i, 0)),
            pl.BlockSpec((None, 6, d), lambda i, *_: (i // blocks_per_batch, 0, 0)),
        ],
        out_specs=pl.BlockSpec((tm, d), lambda i, *_: (i, 0)),
        scratch_shapes=[pltpu.VMEM((2, TOP_K, tm, d), F32), pltpu.SemaphoreType.DMA((2,))],
    )
    return pl.pallas_call(
        functools.partial(_combine_kernel, gate_row=gate_row),
        grid_spec=grid_spec,
        out_shape=jax.ShapeDtypeStruct((n, d), F32),
        name="moe_combine",
        compiler_params=_cparams(("arbitrary",)),
    )(pos.reshape(-1), yb, x2, weights, mod_l)


def _route(logits):
    n_tok = logits.shape[0]
    rows = jnp.arange(n_tok)
    coarse = logits[:, :N_GROUPS]
    grp = jnp.argmax(coarse, axis=-1)
    p_grp = jax.nn.softmax(coarse, axis=-1)[rows, grp]
    fine = logits[:, N_GROUPS:N_GROUPS + N_EXPERTS].reshape(n_tok, N_GROUPS, EXPERTS_PER_GROUP)
    top_val, top_idx = lax.top_k(fine[rows, grp], TOP_K)
    weights = p_grp[:, None] * jax.nn.softmax(top_val, axis=-1)
    expert = grp[:, None] * EXPERTS_PER_GROUP + top_idx
    return expert.astype(jnp.int32), weights


def _dispatch(expert):
    n_tok = expert.shape[0]
    n_assign = n_tok * TOP_K
    cap = n_assign + N_EXPERTS * MOE_BLK
    n_blocks = cap // MOE_BLK
    e_flat = expert.reshape(-1)
    onehot = (e_flat[:, None] == jnp.arange(N_EXPERTS, dtype=jnp.int32)[None, :]).astype(jnp.int32)
    cum = jnp.cumsum(onehot, axis=0)
    counts = cum[-1]
    rank = jnp.sum(onehot * (cum - 1), axis=1)
    padded = ((counts + MOE_BLK - 1) // MOE_BLK) * MOE_BLK
    pad_ends = jnp.cumsum(padded)
    pad_starts = pad_ends - padded
    dest = (jnp.sum(onehot * pad_starts[None, :], axis=1) + rank).astype(jnp.int32)
    order = jnp.argsort(e_flat, stable=True)
    tok_src = jnp.concatenate([(order // TOP_K).astype(jnp.int32), jnp.zeros((MOE_BLK,), jnp.int32)])
    starts = jnp.cumsum(counts) - counts
    blk_start = jnp.arange(n_blocks, dtype=jnp.int32) * MOE_BLK
    blk_expert = jnp.minimum(jnp.sum((pad_ends[None, :] <= blk_start[:, None]).astype(jnp.int32), axis=1),
                             N_EXPERTS - 1).astype(jnp.int32)
    blk_src = jnp.clip(blk_start - (pad_starts - starts)[blk_expert], 0, n_assign).astype(jnp.int32)
    n_active = (pad_ends[-1] // MOE_BLK).astype(jnp.int32)
    prev = jnp.concatenate([jnp.full((1,), -1, jnp.int32), blk_expert[:-1]])
    blk_first = (blk_expert != prev).astype(jnp.int32)
    run_end = pad_ends[blk_expert] // MOE_BLK
    blk_next = jnp.where(run_end < n_active, blk_expert[jnp.minimum(run_end, n_blocks - 1)], -1).astype(jnp.int32)
    return tok_src, blk_src, blk_expert, blk_first, blk_next, n_active.reshape(1), dest.reshape(n_tok, TOP_K)


def kernel(x, c, ada_w, ada_b, norm1_w, norm2_w, w_in, conv_q, conv_k, igate_b, fgate_b, rel_bias,
           mlstm_norm_w, w_branch_attn, w_branch_mlstm, w_out, router_coarse_w, router_coarse_b,
           router_fine_w, router_fine_b, w_gate, w_up, w_down, final_norm_w):
    b, s, d = x.shape
    depth = ada_w.shape[0]
    n = b * s
    nc = s // CHUNK
    a_width = A_HEADS * A_HEAD_DIM
    m_width = M_HEADS * M_HEAD_DIM
    main_cols = 3 * a_width + 4 * m_width + 2 * d
    ga_col0 = 3 * a_width + 4 * m_width
    gm_col0 = ga_col0 + d

    mod = _ada_mod(c, ada_w, ada_b)
    x2 = x.reshape(n, d)

    w_in_t = jnp.swapaxes(w_in, 1, 2)

    def gate_spec(l):
        return pl.BlockSpec((None, LANES, d), lambda i: (l, main_cols // LANES, 0))

    router_spec = pl.BlockSpec((LANES, d), lambda i: (0, 0))

    for l in range(depth):
        b_g = jnp.zeros((1, LANES), F32).at[0, :M_HEADS].set(igate_b[l]).at[0, M_HEADS:2 * M_HEADS].set(fgate_b[l])
        w_r = (jnp.zeros((LANES, d), F32).at[:N_GROUPS].set(router_coarse_w[l].T)
               .at[N_GROUPS:N_GROUPS + N_EXPERTS].set(router_fine_w[l].T))
        b_r = (jnp.zeros((1, LANES), F32).at[0, :N_GROUPS].set(router_coarse_b[l])
               .at[0, N_GROUPS:N_GROUPS + N_EXPERTS].set(router_fine_b[l]))

        h, gates = _norm_mod(x2, norm1_w[l], mod[l], w_in_t, gate_spec(l), 2 * M_HEADS, b_g, seq=s,
                             shift_row=0, scale_row=1, precise=False, out_dtype=BF16)
        p_all = _proj(h, w_in_t, l, main_cols)
        y_attn = _attention(p_all, _attn_bias(rel_bias[l]), b, s)
        gates_t = gates[:, :2 * M_HEADS].reshape(b, nc, CHUNK, 2 * M_HEADS).transpose(0, 1, 3, 2)
        h_m = _mlstm(p_all, gates_t, conv_q, conv_k, mlstm_norm_w[l], l, b, s)
        merged = _merge(y_attn, h_m, p_all, w_branch_attn, w_branch_mlstm, l, ga_col0, gm_col0)
        x2 = _out_proj(merged, w_out, l, x2, mod[l], s, gate_row=2)

        h2, logits = _norm_mod(x2, norm2_w[l], mod[l], w_r, router_spec, N_GROUPS + N_EXPERTS, b_r, seq=s,
                               shift_row=3, scale_row=4, precise=True, out_dtype=F32)
        expert, weights = _route(logits)
        tok_src, blk_src, blk_expert, blk_first, blk_next, n_active, pos = _dispatch(expert)
        yb = _moe_experts(h2, tok_src, blk_src, blk_expert, blk_first, blk_next, n_active, w_gate, w_up, w_down, l)
        x2 = _combine(yb, pos, weights, x2, mod[l], s, gate_row=5)

    return _final_norm(x2, final_norm_w).reshape(b, s, d)
```

```python
import functools

import jax
import jax.numpy as jnp
from jax import lax
from jax.experimental import pallas as pl
from jax.experimental.pallas import tpu as pltpu

F32 = jnp.float32
BF16 = jnp.bfloat16

EPS = 1e-6
NEG_INF = -1e30
LOG2E = 1.4426950408889634
CHUNK = 64
LEFT_CHUNKS = 8
REL_CLIP = 256
A_HEADS = 8
A_HEAD_DIM = 128
M_HEADS = 4
M_HEAD_DIM = 256
CONV_W = 4
N_GROUPS = 4
EXPERTS_PER_GROUP = 8
N_EXPERTS = N_GROUPS * EXPERTS_PER_GROUP
TOP_K = 2

LANES = 128
VMEM_LIMIT = 60 * 1024 * 1024

ATT_QBLK = 256
ATT_KBLKS = 3
MLSTM_BATCH = 4
MOE_BLK = 256


def _cparams(sem):
    return pltpu.CompilerParams(dimension_semantics=sem, vmem_limit_bytes=VMEM_LIMIT)


def _sigmoid(t):
    return 1.0 / (1.0 + jnp.exp(-t))


def _silu(t):
    return t * _sigmoid(t)


def _ada_kernel(c_ref, w_ref, b_ref, o_ref):
    w = w_ref[...].astype(BF16)
    r = jnp.dot(c_ref[...], w, preferred_element_type=F32)
    bp = o_ref.shape[0]
    o_ref[...] = r[:bp] + r[bp:] + b_ref[...]


def _ada_mod(c, ada_w, ada_b):
    depth, d, n6 = ada_w.shape
    b = c.shape[0]
    bp = 8
    c_pad = jnp.zeros((bp, d), F32).at[:b].set(c)
    c_hi = c_pad.astype(BF16)
    c_lo = (c_pad - c_hi.astype(F32)).astype(BF16)
    c2 = jnp.concatenate([c_hi, c_lo], axis=0)
    tn = 1024
    out = pl.pallas_call(
        _ada_kernel,
        grid=(depth, n6 // tn),
        in_specs=[
            pl.BlockSpec((2 * bp, d), lambda l, j: (0, 0)),
            pl.BlockSpec((None, d, tn), lambda l, j: (l, 0, j)),
            pl.BlockSpec((None, 1, tn), lambda l, j: (l, 0, j)),
        ],
        out_specs=pl.BlockSpec((None, bp, tn), lambda l, j: (l, 0, j)),
        out_shape=jax.ShapeDtypeStruct((depth, bp, n6), F32),
        name="ada_mod",
        compiler_params=_cparams(("arbitrary", "arbitrary")),
    )(c2, ada_w, ada_b.reshape(depth, 1, n6))
    return out[:, :b].reshape(depth, b, 6, d)


def _norm_kernel(x_ref, nw_ref, mod_ref, ws_ref, bs_ref, h_ref, s_ref, *, shift_row, scale_row, precise,
                 side_cols):
    x = x_ref[...]
    y = x * lax.rsqrt(jnp.mean(x * x, axis=-1, keepdims=True) + EPS)
    y = y * nw_ref[...]
    h = y * (1.0 + mod_ref[scale_row:scale_row + 1, :]) + mod_ref[shift_row:shift_row + 1, :]
    h_ref[...] = h.astype(h_ref.dtype)
    wrow = lax.broadcasted_iota(jnp.int32, ws_ref.shape, 0)
    ws = jnp.where(wrow < side_cols, ws_ref[...], 0.0)
    nt = (((1,), (1,)), ((), ()))
    if precise:
        h_hi = h.astype(BF16)
        h_lo = (h - h_hi.astype(F32)).astype(BF16)
        w_hi = ws.astype(BF16)
        w_lo = (ws - w_hi.astype(F32)).astype(BF16)
        s = (lax.dot_general(h_hi, w_hi, nt, preferred_element_type=F32)
             + lax.dot_general(h_hi, w_lo, nt, preferred_element_type=F32)
             + lax.dot_general(h_lo, w_hi, nt, preferred_element_type=F32))
    else:
        s = lax.dot_general(h.astype(BF16), ws.astype(BF16), nt, preferred_element_type=F32)
    s_ref[...] = s + bs_ref[...]


def _norm_mod(x2, norm_w, mod_l, w_side, side_spec, side_cols, b_side, *, seq, shift_row, scale_row, precise,
              out_dtype, tm=256):
    n, d = x2.shape
    blocks_per_batch = seq // tm
    kern = functools.partial(_norm_kernel, shift_row=shift_row, scale_row=scale_row, precise=precise,
                             side_cols=side_cols)
    return pl.pallas_call(
        kern,
        grid=(n // tm,),
        in_specs=[
            pl.BlockSpec((tm, d), lambda i: (i, 0)),
            pl.BlockSpec((1, d), lambda i: (0, 0)),
            pl.BlockSpec((None, 6, d), lambda i: (i // blocks_per_batch, 0, 0)),
            side_spec,
            pl.BlockSpec((1, LANES), lambda i: (0, 0)),
        ],
        out_specs=[
            pl.BlockSpec((tm, d), lambda i: (i, 0)),
            pl.BlockSpec((tm, LANES), lambda i: (i, 0)),
        ],
        out_shape=[jax.ShapeDtypeStruct((n, d), out_dtype), jax.ShapeDtypeStruct((n, LANES), F32)],
        name="norm_mod",
        compiler_params=_cparams(("arbitrary",)),
    )(x2, norm_w.reshape(1, d), mod_l, w_side, b_side)


def _final_norm_kernel(x_ref, nw_ref, o_ref):
    x = x_ref[...]
    y = x * lax.rsqrt(jnp.mean(x * x, axis=-1, keepdims=True) + EPS)
    o_ref[...] = y * nw_ref[...]


def _final_norm(x2, norm_w, tm=256):
    n, d = x2.shape
    return pl.pallas_call(
        _final_norm_kernel,
        grid=(n // tm,),
        in_specs=[pl.BlockSpec((tm, d), lambda i: (i, 0)), pl.BlockSpec((1, d), lambda i: (0, 0))],
        out_specs=pl.BlockSpec((tm, d), lambda i: (i, 0)),
        out_shape=jax.ShapeDtypeStruct((n, d), F32),
        name="final_norm",
        compiler_params=_cparams(("arbitrary",)),
    )(x2, norm_w.reshape(1, d))


def _proj_kernel(a_ref, wt_ref, o_ref, wb_ref):
    @pl.when(pl.program_id(1) == 0)
    def _():
        wb_ref[...] = wt_ref[...].astype(BF16)

    o_ref[...] = lax.dot_general(a_ref[...], wb_ref[...], (((1,), (1,)), ((), ())),
                                 preferred_element_type=F32).astype(o_ref.dtype)


def _proj(a, wt_stack, layer, n_cols, tm=1024, tn=1024):
    m, k = a.shape
    return pl.pallas_call(
        _proj_kernel,
        grid=(n_cols // tn, m // tm),
        in_specs=[
            pl.BlockSpec((tm, k), lambda j, i: (i, 0)),
            pl.BlockSpec((None, tn, k), lambda j, i: (layer, j, 0)),
        ],
        out_specs=pl.BlockSpec((tm, tn), lambda j, i: (i, j)),
        out_shape=jax.ShapeDtypeStruct((m, n_cols), BF16),
        scratch_shapes=[pltpu.VMEM((tn, k), BF16)],
        name="in_proj",
        compiler_params=_cparams(("arbitrary", "arbitrary")),
    )(a, wt_stack)


def _attn_kernel(q_ref, k0_ref, k1_ref, k2_ref, v0_ref, v1_ref, v2_ref, bias_ref, o_ref):
    k_refs = (k0_ref, k1_ref, k2_ref)
    v_refs = (v0_ref, v1_ref, v2_ref)
    qb = q_ref.shape[0]
    half = qb // 2
    scale2 = (A_HEAD_DIM ** -0.5) * LOG2E
    heads = [slice(h * A_HEAD_DIM, (h + 1) * A_HEAD_DIM) for h in range(A_HEADS)]
    nt = (((1,), (1,)), ((), ()))
    for part in range(2):
        r0 = part * half
        c0 = part * half
        c1 = c0 + ATT_KBLKS * qb - half
        spans = [(max(c0, j * qb) - j * qb, min(c1, (j + 1) * qb) - j * qb) for j in range(ATT_KBLKS)]
        s = jnp.stack([
            jnp.concatenate([lax.dot_general(q_ref[r0:r0 + half, sl], k_refs[j][lo:hi, sl], nt,
                                             preferred_element_type=F32)
                             for j, (lo, hi) in enumerate(spans)], axis=1)
            for sl in heads])
        s = s * scale2 + bias_ref[:, r0:r0 + half, c0:c1]
        m = jnp.max(s, axis=-1, keepdims=True)
        e = jnp.exp2(s - m)
        denom = jnp.sum(e, axis=-1, keepdims=True)
        p = e.astype(BF16)
        for h, sl in enumerate(heads):
            acc = None
            off = 0
            for j, (lo, hi) in enumerate(spans):
                term = jnp.dot(p[h, :, off:off + hi - lo], v_refs[j][lo:hi, sl], preferred_element_type=F32)
                acc = term if acc is None else acc + term
                off += hi - lo
            o_ref[r0:r0 + half, sl] = (acc / denom[h]).astype(o_ref.dtype)


def _attn_bias(rel_table):
    qb, kw = ATT_QBLK, ATT_KBLKS * ATT_QBLK
    nh = rel_table.shape[0]
    qi = jnp.arange(qb)[:, None]
    kj = jnp.arange(kw)[None, :]
    off = kw - 1 - (ATT_KBLKS - 1) * qb
    glen = qb + kw
    n_lo = max(0, min(glen, off - REL_CLIP))
    n_lin = max(0, min(glen, off + REL_CLIP + 1) - n_lo)
    n_hi = glen - n_lo - n_lin
    lin0 = n_lo - off + REL_CLIP
    gr = jnp.concatenate([jnp.broadcast_to(rel_table[:, 2 * REL_CLIP:], (nh, n_hi)),
                          rel_table[:, lin0:lin0 + n_lin][:, ::-1],
                          jnp.broadcast_to(rel_table[:, :1], (nh, n_lo))], axis=1).astype(F32) * LOG2E
    c0 = glen - kw
    bias = jnp.tile(gr, (1, qb + 1))[:, c0:c0 + qb * (glen - 1)].reshape(nh, qb, glen - 1)[:, :, :kw]
    qc = qi // CHUNK + (ATT_KBLKS - 1) * (qb // CHUNK)
    kc = kj // CHUNK
    band = (kc <= qc) & (kc >= qc - LEFT_CHUNKS)
    tables = []
    for t in range(ATT_KBLKS):
        ok = band & (kj >= (ATT_KBLKS - 1 - t) * qb)
        tables.append(jnp.where(ok[None], bias, NEG_INF))
    return jnp.stack(tables)


def _attention(p_all, bias, batch, seq):
    n = p_all.shape[0]
    width = A_HEADS * A_HEAD_DIM
    qb = ATT_QBLK
    nb = seq // qb

    def kv_spec(back, colblk):
        return pl.BlockSpec((qb, width), lambda i, b: (b * nb + jnp.maximum(i - back, 0), colblk))

    return pl.pallas_call(
        _attn_kernel,
        grid=(nb, batch),
        in_specs=[
            pl.BlockSpec((qb, width), lambda i, b: (b * nb + i, 0)),
            kv_spec(2, 1), kv_spec(1, 1), kv_spec(0, 1),
            kv_spec(2, 2), kv_spec(1, 2), kv_spec(0, 2),
            pl.BlockSpec((None, A_HEADS, qb, ATT_KBLKS * qb), lambda i, b: (jnp.minimum(i, ATT_KBLKS - 1), 0, 0, 0)),
        ],
        out_specs=pl.BlockSpec((qb, width), lambda i, b: (b * nb + i, 0)),
        out_shape=jax.ShapeDtypeStruct((n, width), BF16),
        name="chunk_attn",
        compiler_params=_cparams(("arbitrary", "arbitrary")),
    )(p_all, p_all, p_all, p_all, p_all, p_all, p_all, bias)


def _log_sigmoid(t):
    return jnp.minimum(t, 0.0) - jnp.log(1.0 + jnp.exp(-jnp.abs(t)))


def _mlstm_kernel(q_ref, k_ref, v_ref, o_ref, g_ref, cq_ref, ck_ref, nw_ref, out_ref,
                  qbuf, kbuf, ct_ref, n_ref, m_ref):
    c = pl.program_id(1)
    L, D = CHUNK, M_HEAD_DIM
    nb = q_ref.shape[0]
    ns = nb * M_HEADS
    tail = 16

    @pl.when(c == 0)
    def _():
        qbuf[:, 0:tail, :] = jnp.zeros((nb, tail, qbuf.shape[2]), BF16)
        kbuf[:, 0:tail, :] = jnp.zeros((nb, tail, kbuf.shape[2]), BF16)
        ct_ref[...] = jnp.zeros(ct_ref.shape, F32)
        n_ref[...] = jnp.zeros(n_ref.shape, F32)
        m_ref[...] = jnp.zeros(m_ref.shape, F32)

    row = lax.broadcasted_iota(jnp.int32, (L, L), 0)
    colm = lax.broadcasted_iota(jnp.int32, (L, L), 1)
    causal = colm <= row
    eye = colm == row
    upper = (row <= colm).astype(F32)

    def to_col(r):
        return jnp.sum(jnp.where(eye, jnp.broadcast_to(r, (ns, L, L)), 0.0), axis=-1, keepdims=True)

    srow = lax.broadcasted_iota(jnp.int32, ((CONV_W - 1) * L, tail + L), 0)
    scol = lax.broadcasted_iota(jnp.int32, ((CONV_W - 1) * L, tail + L), 1)
    stap = srow // L
    shifts = (scol == srow - stap * L + stap + (tail - (CONV_W - 1))).astype(BF16)

    def conv(buf, x_ref, w_ref):
        shifted = []
        for bi in range(nb):
            buf[bi, tail:tail + L, :] = x_ref[bi]
            shifted.append(jnp.dot(shifts, buf[bi], preferred_element_type=F32))
            buf[bi, 0:tail, :] = buf[bi, L:L + tail, :]
        shifted = jnp.stack(shifted)
        acc = x_ref[...].astype(F32) * w_ref[CONV_W - 1:CONV_W, :]
        for j in range(CONV_W - 1):
            acc = acc + shifted[:, j * L:(j + 1) * L, :] * w_ref[j:j + 1, :]
        return acc

    def streams(x):
        return jnp.stack([x[bi, :, h * D:(h + 1) * D] for bi in range(nb) for h in range(M_HEADS)])

    q = streams(_silu(conv(qbuf, q_ref, cq_ref)) * (D ** -0.5))
    k = streams(_silu(conv(kbuf, k_ref, ck_ref)))
    qb16 = q.astype(BF16)
    kb16 = k.astype(BF16)
    v16 = [v_ref[bi, :, h * D:(h + 1) * D] for bi in range(nb) for h in range(M_HEADS)]

    g = g_ref[...]
    ig2 = jnp.concatenate([g[bi, 0:M_HEADS, :] for bi in range(nb)], axis=0)
    lf2 = _log_sigmoid(jnp.concatenate([g[bi, M_HEADS:2 * M_HEADS, :] for bi in range(nb)], axis=0))
    bcum2 = jnp.dot(lf2, upper, preferred_element_type=F32, precision=lax.Precision.HIGHEST)
    ig = jnp.stack([ig2[i:i + 1, :] for i in range(ns)])
    bcum = jnp.stack([bcum2[i:i + 1, :] for i in range(ns)])
    bcum_c = to_col(bcum)
    m_prev = m_ref[...].reshape(ns, 1, 1)

    logd = jnp.where(causal, bcum_c - bcum + ig, NEG_INF)
    inter = bcum_c + m_prev
    m_s = jnp.maximum(jnp.max(logd, axis=-1, keepdims=True), inter)
    nt = (((1,), (1,)), ((), ()))
    s = jnp.stack([lax.dot_general(qb16[i], kb16[i], nt, preferred_element_type=F32) for i in range(ns)])
    w_intra = s * jnp.exp(logd - m_s)
    w_inter = jnp.exp(inter - m_s)
    ct = ct_ref[...].reshape(ns, D, D)
    n_row = n_ref[...].reshape(ns, 1, D)
    wi16 = w_intra.astype(BF16)
    ct16 = ct.astype(BF16)
    num_intra = jnp.stack([jnp.dot(wi16[i], v16[i], preferred_element_type=F32) for i in range(ns)])
    num_inter = jnp.stack([jnp.dot(qb16[i], ct16[i], preferred_element_type=F32) for i in range(ns)])
    num = num_intra + w_inter * num_inter
    den = (jnp.sum(w_intra, axis=-1, keepdims=True)
           + w_inter * jnp.sum(q * n_row, axis=-1, keepdims=True))
    hs = num / jnp.maximum(jnp.abs(den), jnp.exp(-m_s))

    b_last = bcum[:, :, L - 1:L]
    log_wk = b_last - bcum + ig
    m_new = jnp.maximum(b_last + m_prev, jnp.max(log_wk, axis=-1, keepdims=True))
    wk = jnp.exp(log_wk - m_new)
    decay = jnp.exp(b_last + m_prev - m_new)
    kw = k * to_col(wk)
    kw16 = kw.astype(BF16)
    tn = (((0,), (0,)), ((), ()))
    upd = jnp.stack([lax.dot_general(kw16[i], v16[i], tn, preferred_element_type=F32) for i in range(ns)])
    ct_ref[...] = (decay * ct + upd).reshape(ct_ref.shape)
    n_ref[...] = (decay * n_row + jnp.sum(kw, axis=1, keepdims=True)).reshape(n_ref.shape)
    m_ref[...] = m_new.reshape(m_ref.shape)

    og = jnp.stack([o_ref[bi, :, h * D:(h + 1) * D] for bi in range(nb) for h in range(M_HEADS)]).astype(F32)
    nw = jnp.stack([nw_ref[:, h * D:(h + 1) * D] for _ in range(nb) for h in range(M_HEADS)])
    hm = _sigmoid(og) * hs
    y = (hm * lax.rsqrt(jnp.mean(hm * hm, axis=-1, keepdims=True) + EPS) * nw).astype(out_ref.dtype)
    for bi in range(nb):
        for h in range(M_HEADS):
            out_ref[bi, :, h * D:(h + 1) * D] = y[bi * M_HEADS + h]


def _mlstm(p_all, gates_t, conv_q, conv_k, norm_w, layer, batch, seq):
    n, cols = p_all.shape
    width = M_HEADS * M_HEAD_DIM
    nc = seq // CHUNK
    L = CHUNK
    nb = MLSTM_BATCH
    p3 = p_all.reshape(batch, seq, cols)

    def p_spec(colblk):
        return pl.BlockSpec((nb, L, width), lambda g, c: (g, c, colblk))

    out = pl.pallas_call(
        _mlstm_kernel,
        grid=(batch // nb, nc),
        in_specs=[
            p_spec(3), p_spec(4), p_spec(5), p_spec(6),
            pl.BlockSpec((nb, None, 2 * M_HEADS, L), lambda g, c: (g, c, 0, 0)),
            pl.BlockSpec((None, CONV_W, width), lambda g, c: (layer, 0, 0)),
            pl.BlockSpec((None, CONV_W, width), lambda g, c: (layer, 0, 0)),
            pl.BlockSpec((1, width), lambda g, c: (0, 0)),
        ],
        out_specs=pl.BlockSpec((nb, L, width), lambda g, c: (g, c, 0)),
        out_shape=jax.ShapeDtypeStruct((batch, seq, width), BF16),
        scratch_shapes=[
            pltpu.VMEM((nb, L + 16, width), BF16),
            pltpu.VMEM((nb, L + 16, width), BF16),
            pltpu.VMEM((nb, M_HEADS, M_HEAD_DIM, M_HEAD_DIM), F32),
            pltpu.VMEM((nb, M_HEADS, 1, M_HEAD_DIM), F32),
            pltpu.VMEM((nb, M_HEADS, 1, 1), F32),
        ],
        name="mlstm",
        compiler_params=_cparams(("arbitrary", "arbitrary")),
    )(p3, p3, p3, p3, gates_t, conv_q, conv_k, norm_w.reshape(1, width))
    return out.reshape(n, width)


def _merge_kernel(ya_ref, hm_ref, ga_ref, gm_ref, wa_ref, wm_ref, o_ref, wab_ref, wmb_ref):
    @pl.when(pl.program_id(1) == 0)
    def _():
        wab_ref[...] = wa_ref[...].astype(BF16)
        wmb_ref[...] = wm_ref[...].astype(BF16)

    a = jnp.dot(ya_ref[...], wab_ref[...], preferred_element_type=F32)
    m = jnp.dot(hm_ref[...], wmb_ref[...], preferred_element_type=F32)
    out = _sigmoid(ga_ref[...].astype(F32)) * a + _sigmoid(gm_ref[...].astype(F32)) * m
    o_ref[...] = out.astype(o_ref.dtype)


def _merge(y_attn, h_m, p_all, w_ba, w_bm, layer, ga_col0, gm_col0, tm=1024, tn=1024):
    n, ka = y_attn.shape
    km = h_m.shape[1]
    d = w_ba.shape[2]
    return pl.pallas_call(
        _merge_kernel,
        grid=(d // tn, n // tm),
        in_specs=[
            pl.BlockSpec((tm, ka), lambda j, i: (i, 0)),
            pl.BlockSpec((tm, km), lambda j, i: (i, 0)),
            pl.BlockSpec((tm, tn), lambda j, i: (i, ga_col0 // tn + j)),
            pl.BlockSpec((tm, tn), lambda j, i: (i, gm_col0 // tn + j)),
            pl.BlockSpec((None, ka, tn), lambda j, i: (layer, 0, j)),
            pl.BlockSpec((None, km, tn), lambda j, i: (layer, 0, j)),
        ],
        out_specs=pl.BlockSpec((tm, tn), lambda j, i: (i, j)),
        out_shape=jax.ShapeDtypeStruct((n, d), BF16),
        scratch_shapes=[pltpu.VMEM((ka, tn), BF16), pltpu.VMEM((km, tn), BF16)],
        name="branch_merge",
        compiler_params=_cparams(("arbitrary", "arbitrary")),
    )(y_attn, h_m, p_all, p_all, w_ba, w_bm)


def _out_kernel(a_ref, w_ref, x_ref, mod_ref, o_ref, wb_ref, *, gate_row):
    @pl.when(pl.program_id(1) == 0)
    def _():
        wb_ref[...] = w_ref[...].astype(BF16)

    y = jnp.dot(a_ref[...], wb_ref[...], preferred_element_type=F32)
    o_ref[...] = x_ref[...] + mod_ref[gate_row:gate_row + 1, :] * y


def _out_proj(a, w_stack, layer, x2, mod_l, seq, gate_row, tm=1024, tn=1024):
    n, k = a.shape
    d = x2.shape[1]
    tm = min(tm, seq)
    blocks_per_batch = seq // tm
    return pl.pallas_call(
        functools.partial(_out_kernel, gate_row=gate_row),
        grid=(d // tn, n // tm),
        in_specs=[
            pl.BlockSpec((tm, k), lambda j, i: (i, 0)),
            pl.BlockSpec((None, k, tn), lambda j, i: (layer, 0, j)),
            pl.BlockSpec((tm, tn), lambda j, i: (i, j)),
            pl.BlockSpec((None, 6, tn), lambda j, i: (i // blocks_per_batch, 0, j)),
        ],
        out_specs=pl.BlockSpec((tm, tn), lambda j, i: (i, j)),
        out_shape=jax.ShapeDtypeStruct((n, d), F32),
        scratch_shapes=[pltpu.VMEM((k, tn), BF16)],
        name="out_proj",
        compiler_params=_cparams(("arbitrary", "arbitrary")),
    )(a, w_stack, x2, mod_l)


def _moe_kernel(tok_ref, src_ref, be_ref, first_ref, nxt_ref, wslot_ref, nact_ref, h_hbm, wg_hbm, wu_hbm, wd_hbm,
                o_ref, xbuf, xb16, wg_st, wu_st, wd_st, xsem, wsem, *, layer):
    i = pl.program_id(0)
    nact = nact_ref[0]
    blk = xbuf.shape[0]
    stages = ((wg_hbm, wg_st), (wu_hbm, wu_st), (wd_hbm, wd_st))

    def weight_copy(k, e, slot):
        return pltpu.make_async_copy(stages[k][0].at[layer, e], stages[k][1].at[slot], wsem.at[slot, k])

    weight_queue = 1

    def start_gather(j):
        base = src_ref[j]
        for r in range(blk):
            tok = tok_ref[base + r]
            pltpu.make_async_copy(h_hbm.at[pl.ds(tok, 1)], xbuf.at[pl.ds(r, 1)], xsem.at[0]).start()

    def wait_gather():
        pltpu.make_async_copy(h_hbm.at[pl.ds(0, blk)], xbuf, xsem.at[0]).wait()

    @pl.when(i == 0)
    def _():
        for k in range(3):
            weight_copy(k, be_ref[0], wslot_ref[0]).start(priority=weight_queue)
        start_gather(0)

    @pl.when((i < nact) & (first_ref[i] == 1))
    def _():
        slot = wslot_ref[i]
        e_next = nxt_ref[i]

        @pl.when(e_next >= 0)
        def _():
            for k in range(3):
                weight_copy(k, e_next, 1 - slot).start(priority=weight_queue)

        for k in range(3):
            weight_copy(k, be_ref[i], slot).wait()

    def compute(slot):
        wait_gather()
        xb16[...] = xbuf[...].astype(BF16)
        start_gather(i + 1)
        x = xb16[...]
        g = jnp.dot(x, wg_st[slot].astype(BF16), preferred_element_type=F32)
        u = jnp.dot(x, wu_st[slot].astype(BF16), preferred_element_type=F32)
        a = (_silu(g) * u).astype(BF16)
        o_ref[...] = jnp.dot(a, wd_st[slot].astype(BF16), preferred_element_type=F32)

    for static_slot in range(2):
        @pl.when((i < nact) & (wslot_ref[i] == static_slot))
        def _(static_slot=static_slot):
            compute(static_slot)

    @pl.when(i >= nact)
    def _():
        @pl.when(i == nact)
        def _():
            wait_gather()

        o_ref[...] = jnp.zeros(o_ref.shape, o_ref.dtype)


def _moe_experts(h2, tok_src, blk_src, blk_expert, blk_first, blk_next, blk_wslot, n_active, w_gate, w_up, w_down,
                 layer):
    n, d = h2.shape
    f = w_gate.shape[3]
    n_blocks = blk_src.shape[0]
    cap = n_blocks * MOE_BLK
    any_spec = pl.BlockSpec(memory_space=pl.ANY)
    grid_spec = pltpu.PrefetchScalarGridSpec(
        num_scalar_prefetch=7,
        grid=(n_blocks,),
        in_specs=[any_spec, any_spec, any_spec, any_spec],
        out_specs=pl.BlockSpec((MOE_BLK, d), lambda i, *_: (i, 0)),
        scratch_shapes=[
            pltpu.VMEM((MOE_BLK, d), F32),
            pltpu.VMEM((MOE_BLK, d), BF16),
            pltpu.VMEM((2, d, f), F32), pltpu.VMEM((2, d, f), F32), pltpu.VMEM((2, f, d), F32),
            pltpu.SemaphoreType.DMA((1,)),
            pltpu.SemaphoreType.DMA((2, 3)),
        ],
    )
    return pl.pallas_call(
        functools.partial(_moe_kernel, layer=layer),
        grid_spec=grid_spec,
        out_shape=jax.ShapeDtypeStruct((cap, d), F32),
        name="moe_experts",
        compiler_params=_cparams(("arbitrary",)),
    )(tok_src, blk_src, blk_expert, blk_first, blk_next, blk_wslot, n_active, h2, w_gate, w_up, w_down)


def _combine_kernel(pos_ref, yb_hbm, x_ref, w_ref, mod_ref, o_ref, buf, sem, *, gate_row):
    i = pl.program_id(0)
    nsteps = pl.num_programs(0)
    t = x_ref.shape[0]

    def start(j, slot):
        base = j * (t * TOP_K)
        dst = buf.at[slot]
        for r in range(t):
            for k in range(TOP_K):
                p = pos_ref[base + r * TOP_K + k]
                pltpu.make_async_copy(yb_hbm.at[pl.ds(p, 1)], dst.at[k, pl.ds(r, 1)], sem.at[slot]).start()

    def wait(slot):
        for k in range(TOP_K):
            pltpu.make_async_copy(yb_hbm.at[pl.ds(0, t)], buf.at[slot, k], sem.at[slot]).wait()

    @pl.when(i == 0)
    def _():
        start(0, 0)

    slot = i % 2

    @pl.when(i + 1 < nsteps)
    def _():
        start(i + 1, 1 - slot)

    wait(slot)
    w = w_ref[...]
    y = w[:, 0:1] * buf[slot, 0] + w[:, 1:2] * buf[slot, 1]
    o_ref[...] = x_ref[...] + mod_ref[gate_row:gate_row + 1, :] * y


def _combine(yb, pos, weights, x2, mod_l, seq, gate_row, tm=256):
    n, d = x2.shape
    blocks_per_batch = seq // tm
    grid_spec = pltpu.PrefetchScalarGridSpec(
        num_scalar_prefetch=1,
        grid=(n // tm,),
        in_specs=[
            pl.BlockSpec(memory_space=pl.ANY),
            pl.BlockSpec((tm, d), lambda i, *_: (i, 0)),
            pl.BlockSpec((tm, TOP_K), lambda i, *_: (i, 0)),
            pl.BlockSpec((None, 6, d), lambda i, *_: (i // blocks_per_batch, 0, 0)),
        ],
        out_specs=pl.BlockSpec((tm, d), lambda i, *_: (i, 0)),
        scratch_shapes=[pltpu.VMEM((2, TOP_K, tm, d), F32), pltpu.SemaphoreType.DMA((2,))],
    )
    return pl.pallas_call(
        functools.partial(_combine_kernel, gate_row=gate_row),
        grid_spec=grid_spec,
        out_shape=jax.ShapeDtypeStruct((n, d), F32),
        name="moe_combine",
        compiler_params=_cparams(("arbitrary",)),
    )(pos.reshape(-1), yb, x2, weights, mod_l)


def _route(logits):
    n_tok = logits.shape[0]
    rows = jnp.arange(n_tok)
    coarse = logits[:, :N_GROUPS]
    grp = jnp.argmax(coarse, axis=-1)
    p_grp = jax.nn.softmax(coarse, axis=-1)[rows, grp]
    fine = logits[:, N_GROUPS:N_GROUPS + N_EXPERTS].reshape(n_tok, N_GROUPS, EXPERTS_PER_GROUP)
    top_val, top_idx = lax.top_k(fine[rows, grp], TOP_K)
    weights = p_grp[:, None] * jax.nn.softmax(top_val, axis=-1)
    expert = grp[:, None] * EXPERTS_PER_GROUP + top_idx
    return expert.astype(jnp.int32), weights


def _dispatch(expert):
    n_tok = expert.shape[0]
    n_assign = n_tok * TOP_K
    cap = n_assign + N_EXPERTS * MOE_BLK
    n_blocks = cap // MOE_BLK
    e_flat = expert.reshape(-1)
    onehot = (e_flat[:, None] == jnp.arange(N_EXPERTS, dtype=jnp.int32)[None, :]).astype(jnp.int32)
    cum = jnp.cumsum(onehot, axis=0)
    counts = cum[-1]
    rank = jnp.sum(onehot * (cum - 1), axis=1)
    padded = ((counts + MOE_BLK - 1) // MOE_BLK) * MOE_BLK
    pad_ends = jnp.cumsum(padded)
    pad_starts = pad_ends - padded
    dest = (jnp.sum(onehot * pad_starts[None, :], axis=1) + rank).astype(jnp.int32)
    order = jnp.argsort(e_flat, stable=True)
    tok_src = jnp.concatenate([(order // TOP_K).astype(jnp.int32), jnp.zeros((MOE_BLK,), jnp.int32)])
    starts = jnp.cumsum(counts) - counts
    blk_start = jnp.arange(n_blocks, dtype=jnp.int32) * MOE_BLK
    blk_expert = jnp.minimum(jnp.sum((pad_ends[None, :] <= blk_start[:, None]).astype(jnp.int32), axis=1),
                             N_EXPERTS - 1).astype(jnp.int32)
    blk_src = jnp.clip(blk_start - (pad_starts - starts)[blk_expert], 0, n_assign).astype(jnp.int32)
    n_active = (pad_ends[-1] // MOE_BLK).astype(jnp.int32)
    prev = jnp.concatenate([jnp.full((1,), -1, jnp.int32), blk_expert[:-1]])
    blk_first = (blk_expert != prev).astype(jnp.int32)
    run_end = pad_ends[blk_expert] // MOE_BLK
    blk_next = jnp.where(run_end < n_active, blk_expert[jnp.minimum(run_end, n_blocks - 1)], -1).astype(jnp.int32)
    blk_wslot = ((jnp.cumsum((counts > 0).astype(jnp.int32)) - 1)[blk_expert] % 2).astype(jnp.int32)
    return (tok_src, blk_src, blk_expert, blk_first, blk_next, blk_wslot, n_active.reshape(1),
            dest.reshape(n_tok, TOP_K))


def kernel(x, c, ada_w, ada_b, norm1_w, norm2_w, w_in, conv_q, conv_k, igate_b, fgate_b, rel_bias,
           mlstm_norm_w, w_branch_attn, w_branch_mlstm, w_out, router_coarse_w, router_coarse_b,
           router_fine_w, router_fine_b, w_gate, w_up, w_down, final_norm_w):
    b, s, d = x.shape
    depth = ada_w.shape[0]
    n = b * s
    nc = s // CHUNK
    a_width = A_HEADS * A_HEAD_DIM
    m_width = M_HEADS * M_HEAD_DIM
    main_cols = 3 * a_width + 4 * m_width + 2 * d
    ga_col0 = 3 * a_width + 4 * m_width
    gm_col0 = ga_col0 + d

    mod = _ada_mod(c, ada_w, ada_b)
    x2 = x.reshape(n, d)

    w_in_t = jnp.swapaxes(w_in, 1, 2)

    def gate_spec(l):
        return pl.BlockSpec((None, LANES, d), lambda i: (l, main_cols // LANES, 0))

    router_spec = pl.BlockSpec((LANES, d), lambda i: (0, 0))

    for l in range(depth):
        b_g = jnp.zeros((1, LANES), F32).at[0, :M_HEADS].set(igate_b[l]).at[0, M_HEADS:2 * M_HEADS].set(fgate_b[l])
        w_r = (jnp.zeros((LANES, d), F32).at[:N_GROUPS].set(router_coarse_w[l].T)
               .at[N_GROUPS:N_GROUPS + N_EXPERTS].set(router_fine_w[l].T))
        b_r = (jnp.zeros((1, LANES), F32).at[0, :N_GROUPS].set(router_coarse_b[l])
               .at[0, N_GROUPS:N_GROUPS + N_EXPERTS].set(router_fine_b[l]))

        h, gates = _norm_mod(x2, norm1_w[l], mod[l], w_in_t, gate_spec(l), 2 * M_HEADS, b_g, seq=s,
                             shift_row=0, scale_row=1, precise=False, out_dtype=BF16)
        p_all = _proj(h, w_in_t, l, main_cols)
        y_attn = _attention(p_all, _attn_bias(rel_bias[l]), b, s)
        gates_t = gates[:, :2 * M_HEADS].reshape(b, nc, CHUNK, 2 * M_HEADS).transpose(0, 1, 3, 2)
        h_m = _mlstm(p_all, gates_t, conv_q, conv_k, mlstm_norm_w[l], l, b, s)
        merged = _merge(y_attn, h_m, p_all, w_branch_attn, w_branch_mlstm, l, ga_col0, gm_col0)
        x2 = _out_proj(merged, w_out, l, x2, mod[l], s, gate_row=2)

        h2, logits = _norm_mod(x2, norm2_w[l], mod[l], w_r, router_spec, N_GROUPS + N_EXPERTS, b_r, seq=s,
                               shift_row=3, scale_row=4, precise=True, out_dtype=F32)
        expert, weights = _route(logits)
        tok_src, blk_src, blk_expert, blk_first, blk_next, blk_wslot, n_active, pos = _dispatch(expert)
        yb = _moe_experts(h2, tok_src, blk_src, blk_expert, blk_first, blk_next, blk_wslot, n_active,
                          w_gate, w_up, w_down, l)
        x2 = _combine(yb, pos, weights, x2, mod[l], s, gate_row=5)

    return _final_norm(x2, final_norm_w).reshape(b, s, d)
```

```python
import functools

import jax
import jax.numpy as jnp
from jax import lax
from jax.experimental import pallas as pl
from jax.experimental.pallas import tpu as pltpu

F32 = jnp.float32
BF16 = jnp.bfloat16

EPS = 1e-6
NEG_INF = -1e30
LOG2E = 1.4426950408889634
CHUNK = 64
LEFT_CHUNKS = 8
REL_CLIP = 256
A_HEADS = 8
A_HEAD_DIM = 128
M_HEADS = 4
M_HEAD_DIM = 256
CONV_W = 4
N_GROUPS = 4
EXPERTS_PER_GROUP = 8
N_EXPERTS = N_GROUPS * EXPERTS_PER_GROUP
TOP_K = 2

LANES = 128
VMEM_LIMIT = 60 * 1024 * 1024

ATT_QBLK = 256
ATT_KBLKS = 3
MLSTM_BATCH = 4
MOE_BLK = 256


def _cparams(sem):
    return pltpu.CompilerParams(dimension_semantics=sem, vmem_limit_bytes=VMEM_LIMIT)


def _sigmoid(t):
    return 1.0 / (1.0 + jnp.exp(-t))


def _silu(t):
    return t * _sigmoid(t)


def _pack_bf16_pairs(x):
    c = x.shape[1] // 2
    lo = lax.bitcast_convert_type(x[:, :c].astype(BF16).astype(F32), jnp.uint32)
    hi = lax.bitcast_convert_type(x[:, c:].astype(BF16).astype(F32), jnp.uint32)
    return (lo >> 16) | hi


def _unpack_bf16_pairs(w):
    lo = lax.bitcast_convert_type(w << 16, F32)
    hi = lax.bitcast_convert_type(w & jnp.uint32(0xFFFF0000), F32)
    return jnp.concatenate([lo, hi], axis=1)


def _ada_kernel(c_ref, w_ref, b_ref, o_ref):
    w = w_ref[...].astype(BF16)
    r = jnp.dot(c_ref[...], w, preferred_element_type=F32)
    bp = o_ref.shape[0]
    o_ref[...] = r[:bp] + r[bp:] + b_ref[...]


def _ada_mod(c, ada_w, ada_b):
    depth, d, n6 = ada_w.shape
    b = c.shape[0]
    bp = 8
    c_pad = jnp.zeros((bp, d), F32).at[:b].set(c)
    c_hi = c_pad.astype(BF16)
    c_lo = (c_pad - c_hi.astype(F32)).astype(BF16)
    c2 = jnp.concatenate([c_hi, c_lo], axis=0)
    tn = 1024
    out = pl.pallas_call(
        _ada_kernel,
        grid=(depth, n6 // tn),
        in_specs=[
            pl.BlockSpec((2 * bp, d), lambda l, j: (0, 0)),
            pl.BlockSpec((None, d, tn), lambda l, j: (l, 0, j)),
            pl.BlockSpec((None, 1, tn), lambda l, j: (l, 0, j)),
        ],
        out_specs=pl.BlockSpec((None, bp, tn), lambda l, j: (l, 0, j)),
        out_shape=jax.ShapeDtypeStruct((depth, bp, n6), F32),
        name="ada_mod",
        compiler_params=_cparams(("arbitrary", "arbitrary")),
    )(c2, ada_w, ada_b.reshape(depth, 1, n6))
    return out[:, :b].reshape(depth, b, 6, d)


def _norm_kernel(x_ref, nw_ref, mod_ref, ws_ref, bs_ref, h_ref, s_ref, *, shift_row, scale_row, precise,
                 side_cols):
    x = x_ref[...]
    y = x * lax.rsqrt(jnp.mean(x * x, axis=-1, keepdims=True) + EPS)
    y = y * nw_ref[...]
    h = y * (1.0 + mod_ref[scale_row:scale_row + 1, :]) + mod_ref[shift_row:shift_row + 1, :]
    h_ref[...] = _pack_bf16_pairs(h) if h_ref.dtype == jnp.uint32 else h.astype(h_ref.dtype)
    wrow = lax.broadcasted_iota(jnp.int32, ws_ref.shape, 0)
    ws = jnp.where(wrow < side_cols, ws_ref[...], 0.0)
    nt = (((1,), (1,)), ((), ()))
    if precise:
        h_hi = h.astype(BF16)
        h_lo = (h - h_hi.astype(F32)).astype(BF16)
        w_hi = ws.astype(BF16)
        w_lo = (ws - w_hi.astype(F32)).astype(BF16)
        s = (lax.dot_general(h_hi, w_hi, nt, preferred_element_type=F32)
             + lax.dot_general(h_hi, w_lo, nt, preferred_element_type=F32)
             + lax.dot_general(h_lo, w_hi, nt, preferred_element_type=F32))
    else:
        s = lax.dot_general(h.astype(BF16), ws.astype(BF16), nt, preferred_element_type=F32)
    s_ref[...] = s + bs_ref[...]


def _norm_mod(x2, norm_w, mod_l, w_side, side_spec, side_cols, b_side, *, seq, shift_row, scale_row, precise,
              out_dtype, tm=512):
    n, d = x2.shape
    tm = min(tm, seq)
    blocks_per_batch = seq // tm
    dh = d // 2 if out_dtype == jnp.uint32 else d
    kern = functools.partial(_norm_kernel, shift_row=shift_row, scale_row=scale_row, precise=precise,
                             side_cols=side_cols)
    return pl.pallas_call(
        kern,
        grid=(n // tm,),
        in_specs=[
            pl.BlockSpec((tm, d), lambda i: (i, 0)),
            pl.BlockSpec((1, d), lambda i: (0, 0)),
            pl.BlockSpec((None, 6, d), lambda i: (i // blocks_per_batch, 0, 0)),
            side_spec,
            pl.BlockSpec((1, LANES), lambda i: (0, 0)),
        ],
        out_specs=[
            pl.BlockSpec((tm, dh), lambda i: (i, 0)),
            pl.BlockSpec((tm, LANES), lambda i: (i, 0)),
        ],
        out_shape=[jax.ShapeDtypeStruct((n, dh), out_dtype), jax.ShapeDtypeStruct((n, LANES), F32)],
        name="norm_mod",
        compiler_params=_cparams(("arbitrary",)),
    )(x2, norm_w.reshape(1, d), mod_l, w_side, b_side)


def _final_norm_kernel(x_ref, nw_ref, o_ref):
    x = x_ref[...]
    y = x * lax.rsqrt(jnp.mean(x * x, axis=-1, keepdims=True) + EPS)
    o_ref[...] = y * nw_ref[...]


def _final_norm(x2, norm_w, tm=256):
    n, d = x2.shape
    return pl.pallas_call(
        _final_norm_kernel,
        grid=(n // tm,),
        in_specs=[pl.BlockSpec((tm, d), lambda i: (i, 0)), pl.BlockSpec((1, d), lambda i: (0, 0))],
        out_specs=pl.BlockSpec((tm, d), lambda i: (i, 0)),
        out_shape=jax.ShapeDtypeStruct((n, d), F32),
        name="final_norm",
        compiler_params=_cparams(("arbitrary",)),
    )(x2, norm_w.reshape(1, d))


def _proj_kernel(a_ref, wt_ref, o_ref, wb_ref):
    @pl.when(pl.program_id(1) == 0)
    def _():
        wb_ref[...] = wt_ref[...].astype(BF16)

    o_ref[...] = lax.dot_general(a_ref[...], wb_ref[...], (((1,), (1,)), ((), ())),
                                 preferred_element_type=F32).astype(o_ref.dtype)


def _proj(a, wt_stack, layer, n_cols, tm=1024, tn=1024):
    m, k = a.shape
    return pl.pallas_call(
        _proj_kernel,
        grid=(n_cols // tn, m // tm),
        in_specs=[
            pl.BlockSpec((tm, k), lambda j, i: (i, 0)),
            pl.BlockSpec((None, tn, k), lambda j, i: (layer, j, 0)),
        ],
        out_specs=pl.BlockSpec((tm, tn), lambda j, i: (i, j)),
        out_shape=jax.ShapeDtypeStruct((m, n_cols), BF16),
        scratch_shapes=[pltpu.VMEM((tn, k), BF16)],
        name="in_proj",
        compiler_params=_cparams(("arbitrary", "arbitrary")),
    )(a, wt_stack)


def _attn_kernel(q_ref, k0_ref, k1_ref, k2_ref, v0_ref, v1_ref, v2_ref, bias_ref, o_ref):
    k_refs = (k0_ref, k1_ref, k2_ref)
    v_refs = (v0_ref, v1_ref, v2_ref)
    qb = q_ref.shape[0]
    half = qb // 2
    scale2 = (A_HEAD_DIM ** -0.5) * LOG2E
    heads = [slice(h * A_HEAD_DIM, (h + 1) * A_HEAD_DIM) for h in range(A_HEADS)]
    nt = (((1,), (1,)), ((), ()))
    for part in range(2):
        r0 = part * half
        c0 = part * half
        c1 = c0 + ATT_KBLKS * qb - half
        spans = [(max(c0, j * qb) - j * qb, min(c1, (j + 1) * qb) - j * qb) for j in range(ATT_KBLKS)]
        s = jnp.stack([
            jnp.concatenate([lax.dot_general(q_ref[r0:r0 + half, sl], k_refs[j][lo:hi, sl], nt,
                                             preferred_element_type=F32)
                             for j, (lo, hi) in enumerate(spans)], axis=1)
            for sl in heads])
        s = s * scale2 + bias_ref[:, r0:r0 + half, c0:c1]
        m = jnp.max(s, axis=-1, keepdims=True)
        e = jnp.exp2(s - m)
        denom = jnp.sum(e, axis=-1, keepdims=True)
        p = e.astype(BF16)
        for h, sl in enumerate(heads):
            acc = None
            off = 0
            for j, (lo, hi) in enumerate(spans):
                term = jnp.dot(p[h, :, off:off + hi - lo], v_refs[j][lo:hi, sl], preferred_element_type=F32)
                acc = term if acc is None else acc + term
                off += hi - lo
            o_ref[r0:r0 + half, sl] = (acc / denom[h]).astype(o_ref.dtype)


def _attn_bias(rel_table):
    qb, kw = ATT_QBLK, ATT_KBLKS * ATT_QBLK
    nh = rel_table.shape[0]
    qi = jnp.arange(qb)[:, None]
    kj = jnp.arange(kw)[None, :]
    off = kw - 1 - (ATT_KBLKS - 1) * qb
    glen = qb + kw
    n_lo = max(0, min(glen, off - REL_CLIP))
    n_lin = max(0, min(glen, off + REL_CLIP + 1) - n_lo)
    n_hi = glen - n_lo - n_lin
    lin0 = n_lo - off + REL_CLIP
    gr = jnp.concatenate([jnp.broadcast_to(rel_table[:, 2 * REL_CLIP:], (nh, n_hi)),
                          rel_table[:, lin0:lin0 + n_lin][:, ::-1],
                          jnp.broadcast_to(rel_table[:, :1], (nh, n_lo))], axis=1).astype(F32) * LOG2E
    c0 = glen - kw
    bias = jnp.tile(gr, (1, qb + 1))[:, c0:c0 + qb * (glen - 1)].reshape(nh, qb, glen - 1)[:, :, :kw]
    qc = qi // CHUNK + (ATT_KBLKS - 1) * (qb // CHUNK)
    kc = kj // CHUNK
    band = (kc <= qc) & (kc >= qc - LEFT_CHUNKS)
    tables = []
    for t in range(ATT_KBLKS):
        ok = band & (kj >= (ATT_KBLKS - 1 - t) * qb)
        tables.append(jnp.where(ok[None], bias, NEG_INF))
    return jnp.stack(tables)


def _attention(p_all, bias, batch, seq):
    n = p_all.shape[0]
    width = A_HEADS * A_HEAD_DIM
    qb = ATT_QBLK
    nb = seq // qb

    def kv_spec(back, colblk):
        return pl.BlockSpec((qb, width), lambda i, b: (b * nb + jnp.maximum(i - back, 0), colblk))

    return pl.pallas_call(
        _attn_kernel,
        grid=(nb, batch),
        in_specs=[
            pl.BlockSpec((qb, width), lambda i, b: (b * nb + i, 0)),
            kv_spec(2, 1), kv_spec(1, 1), kv_spec(0, 1),
            kv_spec(2, 2), kv_spec(1, 2), kv_spec(0, 2),
            pl.BlockSpec((None, A_HEADS, qb, ATT_KBLKS * qb), lambda i, b: (jnp.minimum(i, ATT_KBLKS - 1), 0, 0, 0)),
        ],
        out_specs=pl.BlockSpec((qb, width), lambda i, b: (b * nb + i, 0)),
        out_shape=jax.ShapeDtypeStruct((n, width), BF16),
        name="chunk_attn",
        compiler_params=_cparams(("arbitrary", "arbitrary")),
    )(p_all, p_all, p_all, p_all, p_all, p_all, p_all, bias)


def _log_sigmoid(t):
    return jnp.minimum(t, 0.0) - jnp.log(1.0 + jnp.exp(-jnp.abs(t)))


def _mlstm_kernel(q_ref, k_ref, v_ref, o_ref, g_ref, cq_ref, ck_ref, nw_ref, out_ref,
                  qbuf, kbuf, ct_ref, n_ref, m_ref):
    c = pl.program_id(1)
    L, D = CHUNK, M_HEAD_DIM
    nb = q_ref.shape[0]
    ns = nb * M_HEADS
    tail = 16

    @pl.when(c == 0)
    def _():
        qbuf[:, 0:tail, :] = jnp.zeros((nb, tail, qbuf.shape[2]), BF16)
        kbuf[:, 0:tail, :] = jnp.zeros((nb, tail, kbuf.shape[2]), BF16)
        ct_ref[...] = jnp.zeros(ct_ref.shape, F32)
        n_ref[...] = jnp.zeros(n_ref.shape, F32)
        m_ref[...] = jnp.zeros(m_ref.shape, F32)

    row = lax.broadcasted_iota(jnp.int32, (L, L), 0)
    colm = lax.broadcasted_iota(jnp.int32, (L, L), 1)
    causal = colm <= row
    eye = colm == row
    upper = (row <= colm).astype(F32)

    def to_col(r):
        return jnp.sum(jnp.where(eye, jnp.broadcast_to(r, (ns, L, L)), 0.0), axis=-1, keepdims=True)

    srow = lax.broadcasted_iota(jnp.int32, ((CONV_W - 1) * L, tail + L), 0)
    scol = lax.broadcasted_iota(jnp.int32, ((CONV_W - 1) * L, tail + L), 1)
    stap = srow // L
    shifts = (scol == srow - stap * L + stap + (tail - (CONV_W - 1))).astype(BF16)

    def conv(buf, x_ref, w_ref):
        shifted = []
        for bi in range(nb):
            buf[bi, tail:tail + L, :] = x_ref[bi]
            shifted.append(jnp.dot(shifts, buf[bi], preferred_element_type=F32))
            buf[bi, 0:tail, :] = buf[bi, L:L + tail, :]
        shifted = jnp.stack(shifted)
        acc = x_ref[...].astype(F32) * w_ref[CONV_W - 1:CONV_W, :]
        for j in range(CONV_W - 1):
            acc = acc + shifted[:, j * L:(j + 1) * L, :] * w_ref[j:j + 1, :]
        return acc

    def streams(x):
        return jnp.stack([x[bi, :, h * D:(h + 1) * D] for bi in range(nb) for h in range(M_HEADS)])

    q = streams(_silu(conv(qbuf, q_ref, cq_ref)) * (D ** -0.5))
    k = streams(_silu(conv(kbuf, k_ref, ck_ref)))
    qb16 = q.astype(BF16)
    kb16 = k.astype(BF16)
    v16 = [v_ref[bi, :, h * D:(h + 1) * D] for bi in range(nb) for h in range(M_HEADS)]

    g = g_ref[...]
    ig2 = jnp.concatenate([g[bi, 0:M_HEADS, :] for bi in range(nb)], axis=0)
    lf2 = _log_sigmoid(jnp.concatenate([g[bi, M_HEADS:2 * M_HEADS, :] for bi in range(nb)], axis=0))
    bcum2 = jnp.dot(lf2, upper, preferred_element_type=F32, precision=lax.Precision.HIGHEST)
    ig = jnp.stack([ig2[i:i + 1, :] for i in range(ns)])
    bcum = jnp.stack([bcum2[i:i + 1, :] for i in range(ns)])
    bcum_c = to_col(bcum)
    m_prev = m_ref[...].reshape(ns, 1, 1)

    logd = jnp.where(causal, bcum_c - bcum + ig, NEG_INF)
    inter = bcum_c + m_prev
    m_s = jnp.maximum(jnp.max(logd, axis=-1, keepdims=True), inter)
    nt = (((1,), (1,)), ((), ()))
    s = jnp.stack([lax.dot_general(qb16[i], kb16[i], nt, preferred_element_type=F32) for i in range(ns)])
    w_intra = s * jnp.exp(logd - m_s)
    w_inter = jnp.exp(inter - m_s)
    ct = ct_ref[...].reshape(ns, D, D)
    n_row = n_ref[...].reshape(ns, 1, D)
    wi16 = w_intra.astype(BF16)
    ct16 = ct.astype(BF16)
    num_intra = jnp.stack([jnp.dot(wi16[i], v16[i], preferred_element_type=F32) for i in range(ns)])
    num_inter = jnp.stack([jnp.dot(qb16[i], ct16[i], preferred_element_type=F32) for i in range(ns)])
    num = num_intra + w_inter * num_inter
    den = (jnp.sum(w_intra, axis=-1, keepdims=True)
           + w_inter * jnp.sum(q * n_row, axis=-1, keepdims=True))
    hs = num / jnp.maximum(jnp.abs(den), jnp.exp(-m_s))

    b_last = bcum[:, :, L - 1:L]
    log_wk = b_last - bcum + ig
    m_new = jnp.maximum(b_last + m_prev, jnp.max(log_wk, axis=-1, keepdims=True))
    wk = jnp.exp(log_wk - m_new)
    decay = jnp.exp(b_last + m_prev - m_new)
    kw = k * to_col(wk)
    kw16 = kw.astype(BF16)
    tn = (((0,), (0,)), ((), ()))
    upd = jnp.stack([lax.dot_general(kw16[i], v16[i], tn, preferred_element_type=F32) for i in range(ns)])
    ct_ref[...] = (decay * ct + upd).reshape(ct_ref.shape)
    n_ref[...] = (decay * n_row + jnp.sum(kw, axis=1, keepdims=True)).reshape(n_ref.shape)
    m_ref[...] = m_new.reshape(m_ref.shape)

    og = jnp.stack([o_ref[bi, :, h * D:(h + 1) * D] for bi in range(nb) for h in range(M_HEADS)]).astype(F32)
    nw = jnp.stack([nw_ref[:, h * D:(h + 1) * D] for _ in range(nb) for h in range(M_HEADS)])
    hm = _sigmoid(og) * hs
    y = (hm * lax.rsqrt(jnp.mean(hm * hm, axis=-1, keepdims=True) + EPS) * nw).astype(out_ref.dtype)
    for bi in range(nb):
        for h in range(M_HEADS):
            out_ref[bi, :, h * D:(h + 1) * D] = y[bi * M_HEADS + h]


def _mlstm(p_all, gates_t, conv_q, conv_k, norm_w, layer, batch, seq):
    n, cols = p_all.shape
    width = M_HEADS * M_HEAD_DIM
    nc = seq // CHUNK
    L = CHUNK
    nb = MLSTM_BATCH
    p3 = p_all.reshape(batch, seq, cols)

    def p_spec(colblk):
        return pl.BlockSpec((nb, L, width), lambda g, c: (g, c, colblk))

    out = pl.pallas_call(
        _mlstm_kernel,
        grid=(batch // nb, nc),
        in_specs=[
            p_spec(3), p_spec(4), p_spec(5), p_spec(6),
            pl.BlockSpec((nb, None, 2 * M_HEADS, L), lambda g, c: (g, c, 0, 0)),
            pl.BlockSpec((None, CONV_W, width), lambda g, c: (layer, 0, 0)),
            pl.BlockSpec((None, CONV_W, width), lambda g, c: (layer, 0, 0)),
            pl.BlockSpec((1, width), lambda g, c: (0, 0)),
        ],
        out_specs=pl.BlockSpec((nb, L, width), lambda g, c: (g, c, 0)),
        out_shape=jax.ShapeDtypeStruct((batch, seq, width), BF16),
        scratch_shapes=[
            pltpu.VMEM((nb, L + 16, width), BF16),
            pltpu.VMEM((nb, L + 16, width), BF16),
            pltpu.VMEM((nb, M_HEADS, M_HEAD_DIM, M_HEAD_DIM), F32),
            pltpu.VMEM((nb, M_HEADS, 1, M_HEAD_DIM), F32),
            pltpu.VMEM((nb, M_HEADS, 1, 1), F32),
        ],
        name="mlstm",
        compiler_params=_cparams(("arbitrary", "arbitrary")),
    )(p3, p3, p3, p3, gates_t, conv_q, conv_k, norm_w.reshape(1, width))
    return out.reshape(n, width)


def _merge_kernel(ya_ref, hm_ref, ga_ref, gm_ref, wa_ref, wm_ref, o_ref, wab_ref, wmb_ref):
    @pl.when(pl.program_id(1) == 0)
    def _():
        wab_ref[...] = wa_ref[...].astype(BF16)
        wmb_ref[...] = wm_ref[...].astype(BF16)

    a = jnp.dot(ya_ref[...], wab_ref[...], preferred_element_type=F32)
    m = jnp.dot(hm_ref[...], wmb_ref[...], preferred_element_type=F32)
    out = _sigmoid(ga_ref[...].astype(F32)) * a + _sigmoid(gm_ref[...].astype(F32)) * m
    o_ref[...] = out.astype(o_ref.dtype)


def _merge(y_attn, h_m, p_all, w_ba, w_bm, layer, ga_col0, gm_col0, tm=1024, tn=1024):
    n, ka = y_attn.shape
    km = h_m.shape[1]
    d = w_ba.shape[2]
    return pl.pallas_call(
        _merge_kernel,
        grid=(d // tn, n // tm),
        in_specs=[
            pl.BlockSpec((tm, ka), lambda j, i: (i, 0)),
            pl.BlockSpec((tm, km), lambda j, i: (i, 0)),
            pl.BlockSpec((tm, tn), lambda j, i: (i, ga_col0 // tn + j)),
            pl.BlockSpec((tm, tn), lambda j, i: (i, gm_col0 // tn + j)),
            pl.BlockSpec((None, ka, tn), lambda j, i: (layer, 0, j)),
            pl.BlockSpec((None, km, tn), lambda j, i: (layer, 0, j)),
        ],
        out_specs=pl.BlockSpec((tm, tn), lambda j, i: (i, j)),
        out_shape=jax.ShapeDtypeStruct((n, d), BF16),
        scratch_shapes=[pltpu.VMEM((ka, tn), BF16), pltpu.VMEM((km, tn), BF16)],
        name="branch_merge",
        compiler_params=_cparams(("arbitrary", "arbitrary")),
    )(y_attn, h_m, p_all, p_all, w_ba, w_bm)


def _out_kernel(a_ref, w_ref, x_ref, mod_ref, o_ref, wb_ref, *, gate_row):
    @pl.when(pl.program_id(1) == 0)
    def _():
        wb_ref[...] = w_ref[...].astype(BF16)

    y = jnp.dot(a_ref[...], wb_ref[...], preferred_element_type=F32)
    o_ref[...] = x_ref[...] + mod_ref[gate_row:gate_row + 1, :] * y


def _out_proj(a, w_stack, layer, x2, mod_l, seq, gate_row, tm=1024, tn=1024):
    n, k = a.shape
    d = x2.shape[1]
    tm = min(tm, seq)
    blocks_per_batch = seq // tm
    return pl.pallas_call(
        functools.partial(_out_kernel, gate_row=gate_row),
        grid=(d // tn, n // tm),
        in_specs=[
            pl.BlockSpec((tm, k), lambda j, i: (i, 0)),
            pl.BlockSpec((None, k, tn), lambda j, i: (layer, 0, j)),
            pl.BlockSpec((tm, tn), lambda j, i: (i, j)),
            pl.BlockSpec((None, 6, tn), lambda j, i: (i // blocks_per_batch, 0, j)),
        ],
        out_specs=pl.BlockSpec((tm, tn), lambda j, i: (i, j)),
        out_shape=jax.ShapeDtypeStruct((n, d), F32),
        scratch_shapes=[pltpu.VMEM((k, tn), BF16)],
        name="out_proj",
        compiler_params=_cparams(("arbitrary", "arbitrary")),
    )(a, w_stack, x2, mod_l)


def _moe_kernel(tok_ref, src_ref, be_ref, first_ref, nxt_ref, wslot_ref, nact_ref, h_hbm, wg_hbm, wu_hbm, wd_hbm,
                o_ref, xbuf, xb16, wg_st, wu_st, wd_st, xsem, wsem, *, layer):
    i = pl.program_id(0)
    nact = nact_ref[0]
    blk = xbuf.shape[0]
    stages = ((wg_hbm, wg_st), (wu_hbm, wu_st), (wd_hbm, wd_st))

    def weight_copy(k, e, slot):
        return pltpu.make_async_copy(stages[k][0].at[layer, e], stages[k][1].at[slot], wsem.at[slot, k])

    weight_queue = 1

    def start_gather(j):
        base = src_ref[j]
        for r in range(blk):
            tok = tok_ref[base + r]
            pltpu.make_async_copy(h_hbm.at[pl.ds(tok, 1)], xbuf.at[pl.ds(r, 1)], xsem.at[0]).start()

    def wait_gather():
        pltpu.make_async_copy(h_hbm.at[pl.ds(0, blk)], xbuf, xsem.at[0]).wait()

    @pl.when(i == 0)
    def _():
        for k in range(3):
            weight_copy(k, be_ref[0], wslot_ref[0]).start(priority=weight_queue)
        start_gather(0)

    @pl.when((i < nact) & (first_ref[i] == 1))
    def _():
        slot = wslot_ref[i]
        e_next = nxt_ref[i]

        @pl.when(e_next >= 0)
        def _():
            for k in range(3):
                weight_copy(k, e_next, 1 - slot).start(priority=weight_queue)

        for k in range(3):
            weight_copy(k, be_ref[i], slot).wait()

    def compute(slot):
        wait_gather()
        xb16[...] = _unpack_bf16_pairs(xbuf[...]).astype(BF16)
        start_gather(i + 1)
        x = xb16[...]
        g = jnp.dot(x, wg_st[slot].astype(BF16), preferred_element_type=F32)
        u = jnp.dot(x, wu_st[slot].astype(BF16), preferred_element_type=F32)
        a = (_silu(g) * u).astype(BF16)
        o_ref[...] = _pack_bf16_pairs(jnp.dot(a, wd_st[slot].astype(BF16), preferred_element_type=F32))

    for static_slot in range(2):
        @pl.when((i < nact) & (wslot_ref[i] == static_slot))
        def _(static_slot=static_slot):
            compute(static_slot)

    @pl.when(i == nact)
    def _():
        wait_gather()


def _moe_experts(h2, tok_src, blk_src, blk_expert, blk_first, blk_next, blk_wslot, n_active, w_gate, w_up, w_down,
                 layer):
    n, dh = h2.shape
    d = 2 * dh
    f = w_gate.shape[3]
    n_blocks = blk_src.shape[0]
    cap = n_blocks * MOE_BLK
    any_spec = pl.BlockSpec(memory_space=pl.ANY)
    grid_spec = pltpu.PrefetchScalarGridSpec(
        num_scalar_prefetch=7,
        grid=(n_blocks,),
        in_specs=[any_spec, any_spec, any_spec, any_spec],
        out_specs=pl.BlockSpec((MOE_BLK, dh), lambda i, *s: (jnp.minimum(i, s[-1][0] - 1), 0)),
        scratch_shapes=[
            pltpu.VMEM((MOE_BLK, dh), jnp.uint32),
            pltpu.VMEM((MOE_BLK, d), BF16),
            pltpu.VMEM((2, d, f), F32), pltpu.VMEM((2, d, f), F32), pltpu.VMEM((2, f, d), F32),
            pltpu.SemaphoreType.DMA((1,)),
            pltpu.SemaphoreType.DMA((2, 3)),
        ],
    )
    return pl.pallas_call(
        functools.partial(_moe_kernel, layer=layer),
        grid_spec=grid_spec,
        out_shape=jax.ShapeDtypeStruct((cap, dh), jnp.uint32),
        name="moe_experts",
        compiler_params=_cparams(("arbitrary",)),
    )(tok_src, blk_src, blk_expert, blk_first, blk_next, blk_wslot, n_active, h2, w_gate, w_up, w_down)


def _combine_kernel(pos_ref, yb_hbm, x_ref, w_ref, mod_ref, o_ref, buf, sem, *, gate_row):
    i = pl.program_id(0)
    nsteps = pl.num_programs(0)
    t = x_ref.shape[0]

    def start(j, slot):
        base = j * (t * TOP_K)
        dst = buf.at[slot]
        for r in range(t):
            for k in range(TOP_K):
                p = pos_ref[base + r * TOP_K + k]
                pltpu.make_async_copy(yb_hbm.at[pl.ds(p, 1)], dst.at[k, pl.ds(r, 1)], sem.at[slot]).start()

    def wait(slot):
        for k in range(TOP_K):
            pltpu.make_async_copy(yb_hbm.at[pl.ds(0, t)], buf.at[slot, k], sem.at[slot]).wait()

    @pl.when(i == 0)
    def _():
        start(0, 0)

    slot = i % 2

    @pl.when(i + 1 < nsteps)
    def _():
        start(i + 1, 1 - slot)

    wait(slot)
    w = w_ref[...]
    y = w[:, 0:1] * _unpack_bf16_pairs(buf[slot, 0]) + w[:, 1:2] * _unpack_bf16_pairs(buf[slot, 1])
    o_ref[...] = x_ref[...] + mod_ref[gate_row:gate_row + 1, :] * y


def _combine(yb, pos, weights, x2, mod_l, seq, gate_row, tm=256):
    n, d = x2.shape
    blocks_per_batch = seq // tm
    grid_spec = pltpu.PrefetchScalarGridSpec(
        num_scalar_prefetch=1,
        grid=(n // tm,),
        in_specs=[
            pl.BlockSpec(memory_space=pl.ANY),
            pl.BlockSpec((tm, d), lambda i, *_: (i, 0)),
            pl.BlockSpec((tm, TOP_K), lambda i, *_: (i, 0)),
            pl.BlockSpec((None, 6, d), lambda i, *_: (i // blocks_per_batch, 0, 0)),
        ],
        out_specs=pl.BlockSpec((tm, d), lambda i, *_: (i, 0)),
        scratch_shapes=[pltpu.VMEM((2, TOP_K, tm, yb.shape[1]), yb.dtype), pltpu.SemaphoreType.DMA((2,))],
    )
    return pl.pallas_call(
        functools.partial(_combine_kernel, gate_row=gate_row),
        grid_spec=grid_spec,
        out_shape=jax.ShapeDtypeStruct((n, d), F32),
        name="moe_combine",
        compiler_params=_cparams(("arbitrary",)),
    )(pos.reshape(-1), yb, x2, weights, mod_l)


def _route(logits):
    n_tok = logits.shape[0]
    rows = jnp.arange(n_tok)
    coarse = logits[:, :N_GROUPS]
    grp = jnp.argmax(coarse, axis=-1)
    p_grp = jax.nn.softmax(coarse, axis=-1)[rows, grp]
    fine = logits[:, N_GROUPS:N_GROUPS + N_EXPERTS].reshape(n_tok, N_GROUPS, EXPERTS_PER_GROUP)
    top_val, top_idx = lax.top_k(fine[rows, grp], TOP_K)
    weights = p_grp[:, None] * jax.nn.softmax(top_val, axis=-1)
    expert = grp[:, None] * EXPERTS_PER_GROUP + top_idx
    return expert.astype(jnp.int32), weights


def _dispatch(expert):
    n_tok = expert.shape[0]
    n_assign = n_tok * TOP_K
    cap = n_assign + N_EXPERTS * MOE_BLK
    n_blocks = cap // MOE_BLK
    e_flat = expert.reshape(-1)
    onehot = (e_flat[:, None] == jnp.arange(N_EXPERTS, dtype=jnp.int32)[None, :]).astype(jnp.int32)
    cum = jnp.cumsum(onehot, axis=0)
    counts = cum[-1]
    rank = jnp.sum(onehot * (cum - 1), axis=1)
    padded = ((counts + MOE_BLK - 1) // MOE_BLK) * MOE_BLK
    pad_ends = jnp.cumsum(padded)
    pad_starts = pad_ends - padded
    dest = (jnp.sum(onehot * pad_starts[None, :], axis=1) + rank).astype(jnp.int32)
    order = jnp.argsort(e_flat, stable=True)
    tok_src = jnp.concatenate([(order // TOP_K).astype(jnp.int32), jnp.zeros((MOE_BLK,), jnp.int32)])
    starts = jnp.cumsum(counts) - counts
    blk_start = jnp.arange(n_blocks, dtype=jnp.int32) * MOE_BLK
    blk_expert = jnp.minimum(jnp.sum((pad_ends[None, :] <= blk_start[:, None]).astype(jnp.int32), axis=1),
                             N_EXPERTS - 1).astype(jnp.int32)
    blk_src = jnp.clip(blk_start - (pad_starts - starts)[blk_expert], 0, n_assign).astype(jnp.int32)
    n_active = (pad_ends[-1] // MOE_BLK).astype(jnp.int32)
    prev = jnp.concatenate([jnp.full((1,), -1, jnp.int32), blk_expert[:-1]])
    blk_first = (blk_expert != prev).astype(jnp.int32)
    run_end = pad_ends[blk_expert] // MOE_BLK
    blk_next = jnp.where(run_end < n_active, blk_expert[jnp.minimum(run_end, n_blocks - 1)], -1).astype(jnp.int32)
    blk_wslot = ((jnp.cumsum((counts > 0).astype(jnp.int32)) - 1)[blk_expert] % 2).astype(jnp.int32)
    return (tok_src, blk_src, blk_expert, blk_first, blk_next, blk_wslot, n_active.reshape(1),
            dest.reshape(n_tok, TOP_K))


def kernel(x, c, ada_w, ada_b, norm1_w, norm2_w, w_in, conv_q, conv_k, igate_b, fgate_b, rel_bias,
           mlstm_norm_w, w_branch_attn, w_branch_mlstm, w_out, router_coarse_w, router_coarse_b,
           router_fine_w, router_fine_b, w_gate, w_up, w_down, final_norm_w):
    b, s, d = x.shape
    depth = ada_w.shape[0]
    n = b * s
    nc = s // CHUNK
    a_width = A_HEADS * A_HEAD_DIM
    m_width = M_HEADS * M_HEAD_DIM
    main_cols = 3 * a_width + 4 * m_width + 2 * d
    ga_col0 = 3 * a_width + 4 * m_width
    gm_col0 = ga_col0 + d

    mod = _ada_mod(c, ada_w, ada_b)
    x2 = x.reshape(n, d)

    w_in_t = jnp.swapaxes(w_in, 1, 2)

    def gate_spec(l):
        return pl.BlockSpec((None, LANES, d), lambda i: (l, main_cols // LANES, 0))

    router_spec = pl.BlockSpec((LANES, d), lambda i: (0, 0))

    for l in range(depth):
        b_g = jnp.zeros((1, LANES), F32).at[0, :M_HEADS].set(igate_b[l]).at[0, M_HEADS:2 * M_HEADS].set(fgate_b[l])
        w_r = (jnp.zeros((LANES, d), F32).at[:N_GROUPS].set(router_coarse_w[l].T)
               .at[N_GROUPS:N_GROUPS + N_EXPERTS].set(router_fine_w[l].T))
        b_r = (jnp.zeros((1, LANES), F32).at[0, :N_GROUPS].set(router_coarse_b[l])
               .at[0, N_GROUPS:N_GROUPS + N_EXPERTS].set(router_fine_b[l]))

        h, gates = _norm_mod(x2, norm1_w[l], mod[l], w_in_t, gate_spec(l), 2 * M_HEADS, b_g, seq=s,
                             shift_row=0, scale_row=1, precise=False, out_dtype=BF16)
        p_all = _proj(h, w_in_t, l, main_cols)
        y_attn = _attention(p_all, _attn_bias(rel_bias[l]), b, s)
        gates_t = gates[:, :2 * M_HEADS].reshape(b, nc, CHUNK, 2 * M_HEADS).transpose(0, 1, 3, 2)
        h_m = _mlstm(p_all, gates_t, conv_q, conv_k, mlstm_norm_w[l], l, b, s)
        merged = _merge(y_attn, h_m, p_all, w_branch_attn, w_branch_mlstm, l, ga_col0, gm_col0)
        x2 = _out_proj(merged, w_out, l, x2, mod[l], s, gate_row=2)

        h2, logits = _norm_mod(x2, norm2_w[l], mod[l], w_r, router_spec, N_GROUPS + N_EXPERTS, b_r, seq=s,
                               shift_row=3, scale_row=4, precise=True, out_dtype=jnp.uint32)
        expert, weights = _route(logits)
        tok_src, blk_src, blk_expert, blk_first, blk_next, blk_wslot, n_active, pos = _dispatch(expert)
        yb = _moe_experts(h2, tok_src, blk_src, blk_expert, blk_first, blk_next, blk_wslot, n_active,
                          w_gate, w_up, w_down, l)
        x2 = _combine(yb, pos, weights, x2, mod[l], s, gate_row=5)

    return _final_norm(x2, final_norm_w).reshape(b, s, d)
```

```python
import functools

import jax
import jax.numpy as jnp
from jax import lax
from jax.experimental import pallas as pl
from jax.experimental.pallas import tpu as pltpu

F32 = jnp.float32
BF16 = jnp.bfloat16

EPS = 1e-6
NEG_INF = -1e30
LOG2E = 1.4426950408889634
CHUNK = 64
LEFT_CHUNKS = 8
REL_CLIP = 256
A_HEADS = 8
A_HEAD_DIM = 128
M_HEADS = 4
M_HEAD_DIM = 256
CONV_W = 4
N_GROUPS = 4
EXPERTS_PER_GROUP = 8
N_EXPERTS = N_GROUPS * EXPERTS_PER_GROUP
TOP_K = 2

LANES = 128
VMEM_LIMIT = 60 * 1024 * 1024

ATT_QBLK = 256
ATT_KBLKS = 3
MLSTM_BATCH = 4
MOE_BLK = 256


def _cparams(sem):
    return pltpu.CompilerParams(dimension_semantics=sem, vmem_limit_bytes=VMEM_LIMIT)


def _sigmoid(t):
    return 1.0 / (1.0 + jnp.exp(-t))


def _silu(t):
    return t * _sigmoid(t)


def _pack_bf16_pairs(x):
    c = x.shape[1] // 2
    lo = lax.bitcast_convert_type(x[:, :c].astype(BF16).astype(F32), jnp.uint32)
    hi = lax.bitcast_convert_type(x[:, c:].astype(BF16).astype(F32), jnp.uint32)
    return (lo >> 16) | hi


def _unpack_bf16_pairs(w):
    lo = lax.bitcast_convert_type(w << 16, F32)
    hi = lax.bitcast_convert_type(w & jnp.uint32(0xFFFF0000), F32)
    return jnp.concatenate([lo, hi], axis=1)


def _ada_kernel(c_ref, w_ref, b_ref, o_ref):
    w = w_ref[...].astype(BF16)
    r = jnp.dot(c_ref[...], w, preferred_element_type=F32)
    bp = o_ref.shape[0]
    o_ref[...] = r[:bp] + r[bp:] + b_ref[...]


def _ada_mod(c, ada_w, ada_b):
    depth, d, n6 = ada_w.shape
    b = c.shape[0]
    bp = 8
    c_pad = jnp.zeros((bp, d), F32).at[:b].set(c)
    c_hi = c_pad.astype(BF16)
    c_lo = (c_pad - c_hi.astype(F32)).astype(BF16)
    c2 = jnp.concatenate([c_hi, c_lo], axis=0)
    tn = 1024
    out = pl.pallas_call(
        _ada_kernel,
        grid=(depth, n6 // tn),
        in_specs=[
            pl.BlockSpec((2 * bp, d), lambda l, j: (0, 0)),
            pl.BlockSpec((None, d, tn), lambda l, j: (l, 0, j)),
            pl.BlockSpec((None, 1, tn), lambda l, j: (l, 0, j)),
        ],
        out_specs=pl.BlockSpec((None, bp, tn), lambda l, j: (l, 0, j)),
        out_shape=jax.ShapeDtypeStruct((depth, bp, n6), F32),
        name="ada_mod",
        compiler_params=_cparams(("arbitrary", "arbitrary")),
    )(c2, ada_w, ada_b.reshape(depth, 1, n6))
    return out[:, :b].reshape(depth, b, 6, d)


def _rms_mod(x, nw, scale, shift):
    y = x * lax.rsqrt(jnp.mean(x * x, axis=-1, keepdims=True) + EPS)
    return (y * nw) * (1.0 + scale) + shift


def _side_proj(h, ws_ref, bs_ref, side_cols, precise):
    wrow = lax.broadcasted_iota(jnp.int32, ws_ref.shape, 0)
    ws = jnp.where(wrow < side_cols, ws_ref[...], 0.0)
    nt = (((1,), (1,)), ((), ()))
    if precise:
        h_hi = h.astype(BF16)
        h_lo = (h - h_hi.astype(F32)).astype(BF16)
        w_hi = ws.astype(BF16)
        w_lo = (ws - w_hi.astype(F32)).astype(BF16)
        s = (lax.dot_general(h_hi, w_hi, nt, preferred_element_type=F32)
             + lax.dot_general(h_hi, w_lo, nt, preferred_element_type=F32)
             + lax.dot_general(h_lo, w_hi, nt, preferred_element_type=F32))
    else:
        s = lax.dot_general(h.astype(BF16), ws.astype(BF16), nt, preferred_element_type=F32)
    return s + bs_ref[...]


def _norm_kernel(x_ref, nw_ref, mod_ref, ws_ref, bs_ref, h_ref, s_ref, *, shift_row, scale_row, precise,
                 side_cols):
    h = _rms_mod(x_ref[...], nw_ref[...], mod_ref[scale_row:scale_row + 1, :], mod_ref[shift_row:shift_row + 1, :])
    h_ref[...] = _pack_bf16_pairs(h) if h_ref.dtype == jnp.uint32 else h.astype(h_ref.dtype)
    s_ref[...] = _side_proj(h, ws_ref, bs_ref, side_cols, precise)


def _norm_mod(x2, norm_w, mod_l, w_side, side_spec, side_cols, b_side, *, seq, shift_row, scale_row, precise,
              out_dtype, tm=512):
    n, d = x2.shape
    tm = min(tm, seq)
    blocks_per_batch = seq // tm
    dh = d // 2 if out_dtype == jnp.uint32 else d
    kern = functools.partial(_norm_kernel, shift_row=shift_row, scale_row=scale_row, precise=precise,
                             side_cols=side_cols)
    return pl.pallas_call(
        kern,
        grid=(n // tm,),
        in_specs=[
            pl.BlockSpec((tm, d), lambda i: (i, 0)),
            pl.BlockSpec((1, d), lambda i: (0, 0)),
            pl.BlockSpec((None, 6, d), lambda i: (i // blocks_per_batch, 0, 0)),
            side_spec,
            pl.BlockSpec((1, LANES), lambda i: (0, 0)),
        ],
        out_specs=[
            pl.BlockSpec((tm, dh), lambda i: (i, 0)),
            pl.BlockSpec((tm, LANES), lambda i: (i, 0)),
        ],
        out_shape=[jax.ShapeDtypeStruct((n, dh), out_dtype), jax.ShapeDtypeStruct((n, LANES), F32)],
        name="norm_mod",
        compiler_params=_cparams(("arbitrary",)),
    )(x2, norm_w.reshape(1, d), mod_l, w_side, b_side)


def _proj_kernel(a_ref, wt_ref, o_ref, wb_ref):
    @pl.when(pl.program_id(1) == 0)
    def _():
        wb_ref[...] = wt_ref[...].astype(BF16)

    o_ref[...] = lax.dot_general(a_ref[...], wb_ref[...], (((1,), (1,)), ((), ())),
                                 preferred_element_type=F32).astype(o_ref.dtype)


def _proj(a, wt_stack, layer, n_cols, tm=1024, tn=1024):
    m, k = a.shape
    return pl.pallas_call(
        _proj_kernel,
        grid=(n_cols // tn, m // tm),
        in_specs=[
            pl.BlockSpec((tm, k), lambda j, i: (i, 0)),
            pl.BlockSpec((None, tn, k), lambda j, i: (layer, j, 0)),
        ],
        out_specs=pl.BlockSpec((tm, tn), lambda j, i: (i, j)),
        out_shape=jax.ShapeDtypeStruct((m, n_cols), BF16),
        scratch_shapes=[pltpu.VMEM((tn, k), BF16)],
        name="in_proj",
        compiler_params=_cparams(("arbitrary", "arbitrary")),
    )(a, wt_stack)


def _attn_kernel(q_ref, k0_ref, k1_ref, k2_ref, v0_ref, v1_ref, v2_ref, bias_ref, o_ref):
    k_refs = (k0_ref, k1_ref, k2_ref)
    v_refs = (v0_ref, v1_ref, v2_ref)
    qb = q_ref.shape[0]
    half = qb // 2
    scale2 = (A_HEAD_DIM ** -0.5) * LOG2E
    heads = [slice(h * A_HEAD_DIM, (h + 1) * A_HEAD_DIM) for h in range(A_HEADS)]
    nt = (((1,), (1,)), ((), ()))
    for part in range(2):
        r0 = part * half
        c0 = part * half
        c1 = c0 + ATT_KBLKS * qb - half
        spans = [(max(c0, j * qb) - j * qb, min(c1, (j + 1) * qb) - j * qb) for j in range(ATT_KBLKS)]
        s = jnp.stack([
            jnp.concatenate([lax.dot_general(q_ref[r0:r0 + half, sl], k_refs[j][lo:hi, sl], nt,
                                             preferred_element_type=F32)
                             for j, (lo, hi) in enumerate(spans)], axis=1)
            for sl in heads])
        s = s * scale2 + bias_ref[:, r0:r0 + half, c0:c1]
        m = jnp.max(s, axis=-1, keepdims=True)
        e = jnp.exp2(s - m)
        denom = jnp.sum(e, axis=-1, keepdims=True)
        p = e.astype(BF16)
        for h, sl in enumerate(heads):
            acc = None
            off = 0
            for j, (lo, hi) in enumerate(spans):
                term = jnp.dot(p[h, :, off:off + hi - lo], v_refs[j][lo:hi, sl], preferred_element_type=F32)
                acc = term if acc is None else acc + term
                off += hi - lo
            o_ref[r0:r0 + half, sl] = (acc / denom[h]).astype(o_ref.dtype)


def _attn_bias(rel_table):
    qb, kw = ATT_QBLK, ATT_KBLKS * ATT_QBLK
    nh = rel_table.shape[0]
    qi = jnp.arange(qb)[:, None]
    kj = jnp.arange(kw)[None, :]
    off = kw - 1 - (ATT_KBLKS - 1) * qb
    glen = qb + kw
    n_lo = max(0, min(glen, off - REL_CLIP))
    n_lin = max(0, min(glen, off + REL_CLIP + 1) - n_lo)
    n_hi = glen - n_lo - n_lin
    lin0 = n_lo - off + REL_CLIP
    gr = jnp.concatenate([jnp.broadcast_to(rel_table[:, 2 * REL_CLIP:], (nh, n_hi)),
                          rel_table[:, lin0:lin0 + n_lin][:, ::-1],
                          jnp.broadcast_to(rel_table[:, :1], (nh, n_lo))], axis=1).astype(F32) * LOG2E
    c0 = glen - kw
    bias = jnp.tile(gr, (1, qb + 1))[:, c0:c0 + qb * (glen - 1)].reshape(nh, qb, glen - 1)[:, :, :kw]
    qc = qi // CHUNK + (ATT_KBLKS - 1) * (qb // CHUNK)
    kc = kj // CHUNK
    band = (kc <= qc) & (kc >= qc - LEFT_CHUNKS)
    tables = []
    for t in range(ATT_KBLKS):
        ok = band & (kj >= (ATT_KBLKS - 1 - t) * qb)
        tables.append(jnp.where(ok[None], bias, NEG_INF))
    return jnp.stack(tables)


def _attention(p_all, bias, batch, seq):
    n = p_all.shape[0]
    width = A_HEADS * A_HEAD_DIM
    qb = ATT_QBLK
    nb = seq // qb

    def kv_spec(back, colblk):
        return pl.BlockSpec((qb, width), lambda i, b: (b * nb + jnp.maximum(i - back, 0), colblk))

    return pl.pallas_call(
        _attn_kernel,
        grid=(nb, batch),
        in_specs=[
            pl.BlockSpec((qb, width), lambda i, b: (b * nb + i, 0)),
            kv_spec(2, 1), kv_spec(1, 1), kv_spec(0, 1),
            kv_spec(2, 2), kv_spec(1, 2), kv_spec(0, 2),
            pl.BlockSpec((None, A_HEADS, qb, ATT_KBLKS * qb), lambda i, b: (jnp.minimum(i, ATT_KBLKS - 1), 0, 0, 0)),
        ],
        out_specs=pl.BlockSpec((qb, width), lambda i, b: (b * nb + i, 0)),
        out_shape=jax.ShapeDtypeStruct((n, width), BF16),
        name="chunk_attn",
        compiler_params=_cparams(("arbitrary", "arbitrary")),
    )(p_all, p_all, p_all, p_all, p_all, p_all, p_all, bias)


def _log_sigmoid(t):
    return jnp.minimum(t, 0.0) - jnp.log(1.0 + jnp.exp(-jnp.abs(t)))


def _mlstm_kernel(q_ref, k_ref, v_ref, o_ref, g_ref, cq_ref, ck_ref, nw_ref, out_ref,
                  qbuf, kbuf, ct_ref, n_ref, m_ref):
    c = pl.program_id(1)
    L, D = CHUNK, M_HEAD_DIM
    nb = q_ref.shape[0]
    ns = nb * M_HEADS
    tail = 16

    @pl.when(c == 0)
    def _():
        qbuf[:, 0:tail, :] = jnp.zeros((nb, tail, qbuf.shape[2]), BF16)
        kbuf[:, 0:tail, :] = jnp.zeros((nb, tail, kbuf.shape[2]), BF16)
        ct_ref[...] = jnp.zeros(ct_ref.shape, F32)
        n_ref[...] = jnp.zeros(n_ref.shape, F32)
        m_ref[...] = jnp.zeros(m_ref.shape, F32)

    row = lax.broadcasted_iota(jnp.int32, (L, L), 0)
    colm = lax.broadcasted_iota(jnp.int32, (L, L), 1)
    causal = colm <= row
    eye = colm == row
    upper = (row <= colm).astype(F32)

    def to_col(r):
        return jnp.sum(jnp.where(eye, jnp.broadcast_to(r, (ns, L, L)), 0.0), axis=-1, keepdims=True)

    srow = lax.broadcasted_iota(jnp.int32, ((CONV_W - 1) * L, tail + L), 0)
    scol = lax.broadcasted_iota(jnp.int32, ((CONV_W - 1) * L, tail + L), 1)
    stap = srow // L
    shifts = (scol == srow - stap * L + stap + (tail - (CONV_W - 1))).astype(BF16)

    def conv(buf, x_ref, w_ref):
        shifted = []
        for bi in range(nb):
            buf[bi, tail:tail + L, :] = x_ref[bi]
            shifted.append(jnp.dot(shifts, buf[bi], preferred_element_type=F32))
            buf[bi, 0:tail, :] = buf[bi, L:L + tail, :]
        shifted = jnp.stack(shifted)
        acc = x_ref[...].astype(F32) * w_ref[CONV_W - 1:CONV_W, :]
        for j in range(CONV_W - 1):
            acc = acc + shifted[:, j * L:(j + 1) * L, :] * w_ref[j:j + 1, :]
        return acc

    def streams(x):
        return jnp.stack([x[bi, :, h * D:(h + 1) * D] for bi in range(nb) for h in range(M_HEADS)])

    q = streams(_silu(conv(qbuf, q_ref, cq_ref)) * (D ** -0.5))
    k = streams(_silu(conv(kbuf, k_ref, ck_ref)))
    qb16 = q.astype(BF16)
    kb16 = k.astype(BF16)
    v16 = [v_ref[bi, :, h * D:(h + 1) * D] for bi in range(nb) for h in range(M_HEADS)]

    g = g_ref[...]
    ig2 = jnp.concatenate([g[bi, 0:M_HEADS, :] for bi in range(nb)], axis=0)
    lf2 = _log_sigmoid(jnp.concatenate([g[bi, M_HEADS:2 * M_HEADS, :] for bi in range(nb)], axis=0))
    bcum2 = jnp.dot(lf2, upper, preferred_element_type=F32, precision=lax.Precision.HIGHEST)
    ig = jnp.stack([ig2[i:i + 1, :] for i in range(ns)])
    bcum = jnp.stack([bcum2[i:i + 1, :] for i in range(ns)])
    bcum_c = to_col(bcum)
    m_prev = m_ref[...].reshape(ns, 1, 1)

    logd = jnp.where(causal, bcum_c - bcum + ig, NEG_INF)
    inter = bcum_c + m_prev
    m_s = jnp.maximum(jnp.max(logd, axis=-1, keepdims=True), inter)
    nt = (((1,), (1,)), ((), ()))
    s = jnp.stack([lax.dot_general(qb16[i], kb16[i], nt, preferred_element_type=F32) for i in range(ns)])
    w_intra = s * jnp.exp(logd - m_s)
    w_inter = jnp.exp(inter - m_s)
    ct = ct_ref[...].reshape(ns, D, D)
    n_row = n_ref[...].reshape(ns, 1, D)
    wi16 = w_intra.astype(BF16)
    ct16 = ct.astype(BF16)
    num_intra = jnp.stack([jnp.dot(wi16[i], v16[i], preferred_element_type=F32) for i in range(ns)])
    num_inter = jnp.stack([jnp.dot(qb16[i], ct16[i], preferred_element_type=F32) for i in range(ns)])
    num = num_intra + w_inter * num_inter
    den = (jnp.sum(w_intra, axis=-1, keepdims=True)
           + w_inter * jnp.sum(q * n_row, axis=-1, keepdims=True))
    hs = num / jnp.maximum(jnp.abs(den), jnp.exp(-m_s))

    b_last = bcum[:, :, L - 1:L]
    log_wk = b_last - bcum + ig
    m_new = jnp.maximum(b_last + m_prev, jnp.max(log_wk, axis=-1, keepdims=True))
    wk = jnp.exp(log_wk - m_new)
    decay = jnp.exp(b_last + m_prev - m_new)
    kw = k * to_col(wk)
    kw16 = kw.astype(BF16)
    tn = (((0,), (0,)), ((), ()))
    upd = jnp.stack([lax.dot_general(kw16[i], v16[i], tn, preferred_element_type=F32) for i in range(ns)])
    ct_ref[...] = (decay * ct + upd).reshape(ct_ref.shape)
    n_ref[...] = (decay * n_row + jnp.sum(kw, axis=1, keepdims=True)).reshape(n_ref.shape)
    m_ref[...] = m_new.reshape(m_ref.shape)

    og = jnp.stack([o_ref[bi, :, h * D:(h + 1) * D] for bi in range(nb) for h in range(M_HEADS)]).astype(F32)
    nw = jnp.stack([nw_ref[:, h * D:(h + 1) * D] for _ in range(nb) for h in range(M_HEADS)])
    hm = _sigmoid(og) * hs
    y = (hm * lax.rsqrt(jnp.mean(hm * hm, axis=-1, keepdims=True) + EPS) * nw).astype(out_ref.dtype)
    for bi in range(nb):
        for h in range(M_HEADS):
            out_ref[bi, :, h * D:(h + 1) * D] = y[bi * M_HEADS + h]


def _mlstm(p_all, gates_t, conv_q, conv_k, norm_w, layer, batch, seq):
    n, cols = p_all.shape
    width = M_HEADS * M_HEAD_DIM
    nc = seq // CHUNK
    L = CHUNK
    nb = MLSTM_BATCH
    p3 = p_all.reshape(batch, seq, cols)

    def p_spec(colblk):
        return pl.BlockSpec((nb, L, width), lambda g, c: (g, c, colblk))

    out = pl.pallas_call(
        _mlstm_kernel,
        grid=(batch // nb, nc),
        in_specs=[
            p_spec(3), p_spec(4), p_spec(5), p_spec(6),
            pl.BlockSpec((nb, None, 2 * M_HEADS, L), lambda g, c: (g, c, 0, 0)),
            pl.BlockSpec((None, CONV_W, width), lambda g, c: (layer, 0, 0)),
            pl.BlockSpec((None, CONV_W, width), lambda g, c: (layer, 0, 0)),
            pl.BlockSpec((1, width), lambda g, c: (0, 0)),
        ],
        out_specs=pl.BlockSpec((nb, L, width), lambda g, c: (g, c, 0)),
        out_shape=jax.ShapeDtypeStruct((batch, seq, width), BF16),
        scratch_shapes=[
            pltpu.VMEM((nb, L + 16, width), BF16),
            pltpu.VMEM((nb, L + 16, width), BF16),
            pltpu.VMEM((nb, M_HEADS, M_HEAD_DIM, M_HEAD_DIM), F32),
            pltpu.VMEM((nb, M_HEADS, 1, M_HEAD_DIM), F32),
            pltpu.VMEM((nb, M_HEADS, 1, 1), F32),
        ],
        name="mlstm",
        compiler_params=_cparams(("arbitrary", "arbitrary")),
    )(p3, p3, p3, p3, gates_t, conv_q, conv_k, norm_w.reshape(1, width))
    return out.reshape(n, width)


def _tail_kernel(ya_ref, hm_ref, ga0_ref, ga1_ref, gm0_ref, gm1_ref, wa_ref, wm_ref, wo_ref, x_ref, mod_ref,
                 nw_ref, ws_ref, bs_ref, xo_ref, h_ref, s_ref, *, side_cols):
    a = jnp.dot(ya_ref[...], wa_ref[...], preferred_element_type=F32)
    m = jnp.dot(hm_ref[...], wm_ref[...], preferred_element_type=F32)
    ga = jnp.concatenate([ga0_ref[...], ga1_ref[...]], axis=1).astype(F32)
    gm = jnp.concatenate([gm0_ref[...], gm1_ref[...]], axis=1).astype(F32)
    merged = (_sigmoid(ga) * a + _sigmoid(gm) * m).astype(BF16)
    x = x_ref[...] + mod_ref[2:3, :] * jnp.dot(merged, wo_ref[...], preferred_element_type=F32)
    xo_ref[...] = x
    h = _rms_mod(x, nw_ref[...], mod_ref[4:5, :], mod_ref[3:4, :])
    h_ref[...] = _pack_bf16_pairs(h)
    s_ref[...] = _side_proj(h, ws_ref, bs_ref, side_cols, True)


def _mixer_tail(y_attn, h_m, p_all, w_ba16, w_bm16, w_out16, layer, x2, mod_l, norm_w, w_side, b_side, side_cols,
                seq, ga_col0, gm_col0, tm=256):
    n, ka = y_attn.shape
    km = h_m.shape[1]
    d = x2.shape[1]
    half = d // 2
    blocks_per_batch = seq // tm

    def gate_spec(col0, part):
        return pl.BlockSpec((tm, half), lambda i: (i, col0 // half + part))

    return pl.pallas_call(
        functools.partial(_tail_kernel, side_cols=side_cols),
        grid=(n // tm,),
        in_specs=[
            pl.BlockSpec((tm, ka), lambda i: (i, 0)),
            pl.BlockSpec((tm, km), lambda i: (i, 0)),
            gate_spec(ga_col0, 0), gate_spec(ga_col0, 1), gate_spec(gm_col0, 0), gate_spec(gm_col0, 1),
            pl.BlockSpec((None, ka, d), lambda i: (layer, 0, 0)),
            pl.BlockSpec((None, km, d), lambda i: (layer, 0, 0)),
            pl.BlockSpec((None, d, d), lambda i: (layer, 0, 0)),
            pl.BlockSpec((tm, d), lambda i: (i, 0)),
            pl.BlockSpec((None, 6, d), lambda i: (i // blocks_per_batch, 0, 0)),
            pl.BlockSpec((1, d), lambda i: (0, 0)),
            pl.BlockSpec((LANES, d), lambda i: (0, 0)),
            pl.BlockSpec((1, LANES), lambda i: (0, 0)),
        ],
        out_specs=[
            pl.BlockSpec((tm, d), lambda i: (i, 0)),
            pl.BlockSpec((tm, half), lambda i: (i, 0)),
            pl.BlockSpec((tm, LANES), lambda i: (i, 0)),
        ],
        out_shape=[jax.ShapeDtypeStruct((n, d), F32), jax.ShapeDtypeStruct((n, half), jnp.uint32),
                   jax.ShapeDtypeStruct((n, LANES), F32)],
        name="mixer_tail",
        compiler_params=_cparams(("arbitrary",)),
    )(y_attn, h_m, p_all, p_all, p_all, p_all, w_ba16, w_bm16, w_out16, x2, mod_l, norm_w.reshape(1, d),
      w_side, b_side)


def _moe_kernel(tok_ref, src_ref, be_ref, first_ref, nxt_ref, wslot_ref, nact_ref, h_hbm, wg_hbm, wu_hbm, wd_hbm,
                o_ref, xbuf, xb16, wg_st, wu_st, wd_st, xsem, wsem, *, layer):
    i = pl.program_id(0)
    nact = nact_ref[0]
    blk = xbuf.shape[0]
    stages = ((wg_hbm, wg_st), (wu_hbm, wu_st), (wd_hbm, wd_st))

    def weight_copy(k, e, slot):
        return pltpu.make_async_copy(stages[k][0].at[layer, e], stages[k][1].at[slot], wsem.at[slot, k])

    weight_queue = 1

    def start_gather(j):
        base = src_ref[j]
        for r in range(blk):
            tok = tok_ref[base + r]
            pltpu.make_async_copy(h_hbm.at[pl.ds(tok, 1)], xbuf.at[pl.ds(r, 1)], xsem.at[0]).start()

    def wait_gather():
        pltpu.make_async_copy(h_hbm.at[pl.ds(0, blk)], xbuf, xsem.at[0]).wait()

    @pl.when(i == 0)
    def _():
        for k in range(3):
            weight_copy(k, be_ref[0], wslot_ref[0]).start(priority=weight_queue)
        start_gather(0)

    @pl.when((i < nact) & (first_ref[i] == 1))
    def _():
        slot = wslot_ref[i]
        e_next = nxt_ref[i]

        @pl.when(e_next >= 0)
        def _():
            for k in range(3):
                weight_copy(k, e_next, 1 - slot).start(priority=weight_queue)

        for k in range(3):
            weight_copy(k, be_ref[i], slot).wait()

    def compute(slot):
        wait_gather()
        xb16[...] = _unpack_bf16_pairs(xbuf[...]).astype(BF16)
        start_gather(i + 1)
        x = xb16[...]
        g = jnp.dot(x, wg_st[slot].astype(BF16), preferred_element_type=F32)
        u = jnp.dot(x, wu_st[slot].astype(BF16), preferred_element_type=F32)
        a = (_silu(g) * u).astype(BF16)
        o_ref[...] = _pack_bf16_pairs(jnp.dot(a, wd_st[slot].astype(BF16), preferred_element_type=F32))

    for static_slot in range(2):
        @pl.when((i < nact) & (wslot_ref[i] == static_slot))
        def _(static_slot=static_slot):
            compute(static_slot)

    @pl.when(i >= nact)
    def _():
        @pl.when(i == nact)
        def _():
            wait_gather()

        o_ref[...] = jnp.zeros(o_ref.shape, o_ref.dtype)


def _moe_experts(h2, tok_src, blk_src, blk_expert, blk_first, blk_next, blk_wslot, n_active, w_gate, w_up, w_down,
                 layer):
    n, dh = h2.shape
    d = 2 * dh
    f = w_gate.shape[3]
    n_blocks = blk_src.shape[0]
    cap = n_blocks * MOE_BLK
    any_spec = pl.BlockSpec(memory_space=pl.ANY)
    grid_spec = pltpu.PrefetchScalarGridSpec(
        num_scalar_prefetch=7,
        grid=(n_blocks,),
        in_specs=[any_spec, any_spec, any_spec, any_spec],
        out_specs=pl.BlockSpec((MOE_BLK, dh), lambda i, *_: (i, 0)),
        scratch_shapes=[
            pltpu.VMEM((MOE_BLK, dh), jnp.uint32),
            pltpu.VMEM((MOE_BLK, d), BF16),
            pltpu.VMEM((2, d, f), F32), pltpu.VMEM((2, d, f), F32), pltpu.VMEM((2, f, d), F32),
            pltpu.SemaphoreType.DMA((1,)),
            pltpu.SemaphoreType.DMA((2, 3)),
        ],
    )
    return pl.pallas_call(
        functools.partial(_moe_kernel, layer=layer),
        grid_spec=grid_spec,
        out_shape=jax.ShapeDtypeStruct((cap, dh), jnp.uint32),
        name="moe_experts",
        compiler_params=_cparams(("arbitrary",)),
    )(tok_src, blk_src, blk_expert, blk_first, blk_next, blk_wslot, n_active, h2, w_gate, w_up, w_down)


def _combine_kernel(pos_ref, yb_hbm, x_ref, w_ref, mod_ref, nw_ref, nmod_ref, ws_ref, bs_ref, *out_and_scratch,
                    last, side_cols):
    if last:
        o_ref, buf, sem = out_and_scratch
    else:
        o_ref, h_ref, s_ref, buf, sem = out_and_scratch
    i = pl.program_id(0)
    nsteps = pl.num_programs(0)
    t = x_ref.shape[0]

    def start(j, slot):
        base = j * (t * TOP_K)
        dst = buf.at[slot]
        for r in range(t):
            for k in range(TOP_K):
                p = pos_ref[base + r * TOP_K + k]
                pltpu.make_async_copy(yb_hbm.at[pl.ds(p, 1)], dst.at[k, pl.ds(r, 1)], sem.at[slot]).start()

    def wait(slot):
        for k in range(TOP_K):
            pltpu.make_async_copy(yb_hbm.at[pl.ds(0, t)], buf.at[slot, k], sem.at[slot]).wait()

    @pl.when(i == 0)
    def _():
        start(0, 0)

    slot = i % 2

    @pl.when(i + 1 < nsteps)
    def _():
        start(i + 1, 1 - slot)

    wait(slot)
    w = w_ref[...]
    y = w[:, 0:1] * _unpack_bf16_pairs(buf[slot, 0]) + w[:, 1:2] * _unpack_bf16_pairs(buf[slot, 1])
    x = x_ref[...] + mod_ref[5:6, :] * y
    if last:
        o_ref[...] = x * lax.rsqrt(jnp.mean(x * x, axis=-1, keepdims=True) + EPS) * nw_ref[...]
    else:
        o_ref[...] = x
        h = _rms_mod(x, nw_ref[...], nmod_ref[1:2, :], nmod_ref[0:1, :])
        h_ref[...] = h.astype(h_ref.dtype)
        s_ref[...] = _side_proj(h, ws_ref, bs_ref, side_cols, False)


def _combine(yb, pos, weights, x2, mod_l, seq, next_norm_w, next_mod, w_side, side_spec, b_side, side_cols, last,
             tm=256):
    n, d = x2.shape
    blocks_per_batch = seq // tm
    row_spec = pl.BlockSpec((tm, d), lambda i, *_: (i, 0))
    mod_spec = pl.BlockSpec((None, 6, d), lambda i, *_: (i // blocks_per_batch, 0, 0))
    if last:
        out_specs = row_spec
        out_shape = jax.ShapeDtypeStruct((n, d), F32)
    else:
        out_specs = [row_spec, row_spec, pl.BlockSpec((tm, LANES), lambda i, *_: (i, 0))]
        out_shape = [jax.ShapeDtypeStruct((n, d), F32), jax.ShapeDtypeStruct((n, d), BF16),
                     jax.ShapeDtypeStruct((n, LANES), F32)]
    grid_spec = pltpu.PrefetchScalarGridSpec(
        num_scalar_prefetch=1,
        grid=(n // tm,),
        in_specs=[
            pl.BlockSpec(memory_space=pl.ANY),
            row_spec,
            pl.BlockSpec((tm, TOP_K), lambda i, *_: (i, 0)),
            mod_spec,
            pl.BlockSpec((1, d), lambda i, *_: (0, 0)),
            mod_spec,
            side_spec,
            pl.BlockSpec((1, LANES), lambda i, *_: (0, 0)),
        ],
        out_specs=out_specs,
        scratch_shapes=[pltpu.VMEM((2, TOP_K, tm, yb.shape[1]), yb.dtype), pltpu.SemaphoreType.DMA((2,))],
    )
    return pl.pallas_call(
        functools.partial(_combine_kernel, last=last, side_cols=side_cols),
        grid_spec=grid_spec,
        out_shape=out_shape,
        name="moe_combine",
        compiler_params=_cparams(("arbitrary",)),
    )(pos.reshape(-1), yb, x2, weights, mod_l, next_norm_w.reshape(1, d), next_mod, w_side, b_side)


def _route(logits):
    n_tok = logits.shape[0]
    rows = jnp.arange(n_tok)
    coarse = logits[:, :N_GROUPS]
    grp = jnp.argmax(coarse, axis=-1)
    p_grp = jax.nn.softmax(coarse, axis=-1)[rows, grp]
    fine = logits[:, N_GROUPS:N_GROUPS + N_EXPERTS].reshape(n_tok, N_GROUPS, EXPERTS_PER_GROUP)
    top_val, top_idx = lax.top_k(fine[rows, grp], TOP_K)
    weights = p_grp[:, None] * jax.nn.softmax(top_val, axis=-1)
    expert = grp[:, None] * EXPERTS_PER_GROUP + top_idx
    return expert.astype(jnp.int32), weights


def _dispatch(expert):
    n_tok = expert.shape[0]
    n_assign = n_tok * TOP_K
    cap = n_assign + N_EXPERTS * MOE_BLK
    n_blocks = cap // MOE_BLK
    e_flat = expert.reshape(-1)
    onehot = (e_flat[:, None] == jnp.arange(N_EXPERTS, dtype=jnp.int32)[None, :]).astype(jnp.int32)
    cum = jnp.cumsum(onehot, axis=0)
    counts = cum[-1]
    rank = jnp.sum(onehot * (cum - 1), axis=1)
    padded = ((counts + MOE_BLK - 1) // MOE_BLK) * MOE_BLK
    pad_ends = jnp.cumsum(padded)
    pad_starts = pad_ends - padded
    dest = (jnp.sum(onehot * pad_starts[None, :], axis=1) + rank).astype(jnp.int32)
    order = jnp.argsort(e_flat, stable=True)
    tok_src = jnp.concatenate([(order // TOP_K).astype(jnp.int32), jnp.zeros((MOE_BLK,), jnp.int32)])
    starts = jnp.cumsum(counts) - counts
    blk_start = jnp.arange(n_blocks, dtype=jnp.int32) * MOE_BLK
    blk_expert = jnp.minimum(jnp.sum((pad_ends[None, :] <= blk_start[:, None]).astype(jnp.int32), axis=1),
                             N_EXPERTS - 1).astype(jnp.int32)
    blk_src = jnp.clip(blk_start - (pad_starts - starts)[blk_expert], 0, n_assign).astype(jnp.int32)
    n_active = (pad_ends[-1] // MOE_BLK).astype(jnp.int32)
    prev = jnp.concatenate([jnp.full((1,), -1, jnp.int32), blk_expert[:-1]])
    blk_first = (blk_expert != prev).astype(jnp.int32)
    run_end = pad_ends[blk_expert] // MOE_BLK
    blk_next = jnp.where(run_end < n_active, blk_expert[jnp.minimum(run_end, n_blocks - 1)], -1).astype(jnp.int32)
    blk_wslot = ((jnp.cumsum((counts > 0).astype(jnp.int32)) - 1)[blk_expert] % 2).astype(jnp.int32)
    return (tok_src, blk_src, blk_expert, blk_first, blk_next, blk_wslot, n_active.reshape(1),
            dest.reshape(n_tok, TOP_K))


def kernel(x, c, ada_w, ada_b, norm1_w, norm2_w, w_in, conv_q, conv_k, igate_b, fgate_b, rel_bias,
           mlstm_norm_w, w_branch_attn, w_branch_mlstm, w_out, router_coarse_w, router_coarse_b,
           router_fine_w, router_fine_b, w_gate, w_up, w_down, final_norm_w):
    b, s, d = x.shape
    depth = ada_w.shape[0]
    n = b * s
    nc = s // CHUNK
    a_width = A_HEADS * A_HEAD_DIM
    m_width = M_HEADS * M_HEAD_DIM
    main_cols = 3 * a_width + 4 * m_width + 2 * d
    ga_col0 = 3 * a_width + 4 * m_width
    gm_col0 = ga_col0 + d

    mod = _ada_mod(c, ada_w, ada_b)
    x2 = x.reshape(n, d)

    w_in_t = jnp.swapaxes(w_in, 1, 2)
    w_ba16 = w_branch_attn.astype(BF16)
    w_bm16 = w_branch_mlstm.astype(BF16)
    w_out16 = w_out.astype(BF16)

    def gate_spec(l):
        return pl.BlockSpec((None, LANES, d), lambda i, *_: (l, main_cols // LANES, 0))

    def gate_bias(l):
        return jnp.zeros((1, LANES), F32).at[0, :M_HEADS].set(igate_b[l]).at[0, M_HEADS:2 * M_HEADS].set(fgate_b[l])

    h, gates = _norm_mod(x2, norm1_w[0], mod[0], w_in_t, gate_spec(0), 2 * M_HEADS, gate_bias(0), seq=s,
                         shift_row=0, scale_row=1, precise=False, out_dtype=BF16)
    out = None
    for l in range(depth):
        w_r = (jnp.zeros((LANES, d), F32).at[:N_GROUPS].set(router_coarse_w[l].T)
               .at[N_GROUPS:N_GROUPS + N_EXPERTS].set(router_fine_w[l].T))
        b_r = (jnp.zeros((1, LANES), F32).at[0, :N_GROUPS].set(router_coarse_b[l])
               .at[0, N_GROUPS:N_GROUPS + N_EXPERTS].set(router_fine_b[l]))

        p_all = _proj(h, w_in_t, l, main_cols)
        y_attn = _attention(p_all, _attn_bias(rel_bias[l]), b, s)
        gates_t = gates[:, :2 * M_HEADS].reshape(b, nc, CHUNK, 2 * M_HEADS).transpose(0, 1, 3, 2)
        h_m = _mlstm(p_all, gates_t, conv_q, conv_k, mlstm_norm_w[l], l, b, s)
        x2, h2, logits = _mixer_tail(y_attn, h_m, p_all, w_ba16, w_bm16, w_out16, l, x2, mod[l], norm2_w[l],
                                     w_r, b_r, N_GROUPS + N_EXPERTS, s, ga_col0, gm_col0)
        expert, weights = _route(logits)
        tok_src, blk_src, blk_expert, blk_first, blk_next, blk_wslot, n_active, pos = _dispatch(expert)
        yb = _moe_experts(h2, tok_src, blk_src, blk_expert, blk_first, blk_next, blk_wslot, n_active,
                          w_gate, w_up, w_down, l)
        if l + 1 < depth:
            x2, h, gates = _combine(yb, pos, weights, x2, mod[l], s, norm1_w[l + 1], mod[l + 1], w_in_t,
                                    gate_spec(l + 1), gate_bias(l + 1), 2 * M_HEADS, last=False)
        else:
            out = _combine(yb, pos, weights, x2, mod[l], s, final_norm_w, mod[l], w_in_t, gate_spec(l),
                           gate_bias(l), 2 * M_HEADS, last=True)

    return out.reshape(b, s, d)
```

```python
import functools

import jax
import jax.numpy as jnp
from jax import lax
from jax.experimental import pallas as pl
from jax.experimental.pallas import tpu as pltpu

F32 = jnp.float32
BF16 = jnp.bfloat16

EPS = 1e-6
NEG_INF = -1e30
LOG2E = 1.4426950408889634
CHUNK = 64
LEFT_CHUNKS = 8
REL_CLIP = 256
A_HEADS = 8
A_HEAD_DIM = 128
M_HEADS = 4
M_HEAD_DIM = 256
CONV_W = 4
N_GROUPS = 4
EXPERTS_PER_GROUP = 8
N_EXPERTS = N_GROUPS * EXPERTS_PER_GROUP
TOP_K = 2

LANES = 128
VMEM_LIMIT = 60 * 1024 * 1024

ATT_QBLK = 256
ATT_KBLKS = 3
MLSTM_BATCH = 4
MOE_RING = 3
MOE_BLK = 256


def _cparams(sem):
    return pltpu.CompilerParams(dimension_semantics=sem, vmem_limit_bytes=VMEM_LIMIT)


def _sigmoid(t):
    return 1.0 / (1.0 + jnp.exp(-t))


def _silu(t):
    return t * _sigmoid(t)


def _pack_bf16_pairs(x):
    c = x.shape[1] // 2
    lo = lax.bitcast_convert_type(x[:, :c].astype(BF16).astype(F32), jnp.uint32)
    hi = lax.bitcast_convert_type(x[:, c:].astype(BF16).astype(F32), jnp.uint32)
    return (lo >> 16) | hi


def _unpack_bf16_pairs(w):
    lo = lax.bitcast_convert_type(w << 16, F32)
    hi = lax.bitcast_convert_type(w & jnp.uint32(0xFFFF0000), F32)
    return jnp.concatenate([lo, hi], axis=1)


def _ada_kernel(c_ref, w_ref, b_ref, o_ref):
    w = w_ref[...].astype(BF16)
    r = jnp.dot(c_ref[...], w, preferred_element_type=F32)
    bp = o_ref.shape[0]
    o_ref[...] = r[:bp] + r[bp:] + b_ref[...]


def _ada_mod(c, ada_w, ada_b):
    depth, d, n6 = ada_w.shape
    b = c.shape[0]
    bp = 8
    c_pad = jnp.zeros((bp, d), F32).at[:b].set(c)
    c_hi = c_pad.astype(BF16)
    c_lo = (c_pad - c_hi.astype(F32)).astype(BF16)
    c2 = jnp.concatenate([c_hi, c_lo], axis=0)
    tn = 1024
    out = pl.pallas_call(
        _ada_kernel,
        grid=(depth, n6 // tn),
        in_specs=[
            pl.BlockSpec((2 * bp, d), lambda l, j: (0, 0)),
            pl.BlockSpec((None, d, tn), lambda l, j: (l, 0, j)),
            pl.BlockSpec((None, 1, tn), lambda l, j: (l, 0, j)),
        ],
        out_specs=pl.BlockSpec((None, bp, tn), lambda l, j: (l, 0, j)),
        out_shape=jax.ShapeDtypeStruct((depth, bp, n6), F32),
        name="ada_mod",
        compiler_params=_cparams(("arbitrary", "arbitrary")),
    )(c2, ada_w, ada_b.reshape(depth, 1, n6))
    return out[:, :b].reshape(depth, b, 6, d)


def _rms_mod(x, nw, scale, shift):
    y = x * lax.rsqrt(jnp.mean(x * x, axis=-1, keepdims=True) + EPS)
    return (y * nw) * (1.0 + scale) + shift


def _side_proj(h, ws_ref, bs_ref, side_cols, precise):
    wrow = lax.broadcasted_iota(jnp.int32, ws_ref.shape, 0)
    ws = jnp.where(wrow < side_cols, ws_ref[...], 0.0)
    nt = (((1,), (1,)), ((), ()))
    if precise:
        h_hi = h.astype(BF16)
        h_lo = (h - h_hi.astype(F32)).astype(BF16)
        w_hi = ws.astype(BF16)
        w_lo = (ws - w_hi.astype(F32)).astype(BF16)
        s = (lax.dot_general(h_hi, w_hi, nt, preferred_element_type=F32)
             + lax.dot_general(h_hi, w_lo, nt, preferred_element_type=F32)
             + lax.dot_general(h_lo, w_hi, nt, preferred_element_type=F32))
    else:
        s = lax.dot_general(h.astype(BF16), ws.astype(BF16), nt, preferred_element_type=F32)
    return s + bs_ref[...]


def _norm_kernel(x_ref, nw_ref, mod_ref, ws_ref, bs_ref, h_ref, s_ref, *, shift_row, scale_row, precise,
                 side_cols):
    h = _rms_mod(x_ref[...], nw_ref[...], mod_ref[scale_row:scale_row + 1, :], mod_ref[shift_row:shift_row + 1, :])
    h_ref[...] = _pack_bf16_pairs(h) if h_ref.dtype == jnp.uint32 else h.astype(h_ref.dtype)
    s_ref[...] = _side_proj(h, ws_ref, bs_ref, side_cols, precise)


def _norm_mod(x2, norm_w, mod_l, w_side, side_spec, side_cols, b_side, *, seq, shift_row, scale_row, precise,
              out_dtype, tm=512):
    n, d = x2.shape
    tm = min(tm, seq)
    blocks_per_batch = seq // tm
    dh = d // 2 if out_dtype == jnp.uint32 else d
    kern = functools.partial(_norm_kernel, shift_row=shift_row, scale_row=scale_row, precise=precise,
                             side_cols=side_cols)
    return pl.pallas_call(
        kern,
        grid=(n // tm,),
        in_specs=[
            pl.BlockSpec((tm, d), lambda i: (i, 0)),
            pl.BlockSpec((1, d), lambda i: (0, 0)),
            pl.BlockSpec((None, 6, d), lambda i: (i // blocks_per_batch, 0, 0)),
            side_spec,
            pl.BlockSpec((1, LANES), lambda i: (0, 0)),
        ],
        out_specs=[
            pl.BlockSpec((tm, dh), lambda i: (i, 0)),
            pl.BlockSpec((tm, LANES), lambda i: (i, 0)),
        ],
        out_shape=[jax.ShapeDtypeStruct((n, dh), out_dtype), jax.ShapeDtypeStruct((n, LANES), F32)],
        name="norm_mod",
        compiler_params=_cparams(("arbitrary",)),
    )(x2, norm_w.reshape(1, d), mod_l, w_side, b_side)


def _proj_kernel(a_ref, wt_ref, o_ref, wb_ref):
    @pl.when(pl.program_id(1) == 0)
    def _():
        wb_ref[...] = wt_ref[...].astype(BF16)

    o_ref[...] = lax.dot_general(a_ref[...], wb_ref[...], (((1,), (1,)), ((), ())),
                                 preferred_element_type=F32).astype(o_ref.dtype)


def _proj(a, wt_stack, layer, n_cols, tm=2048, tn=1024):
    m, k = a.shape
    tm = min(tm, m)
    return pl.pallas_call(
        _proj_kernel,
        grid=(n_cols // tn, m // tm),
        in_specs=[
            pl.BlockSpec((tm, k), lambda j, i: (i, 0)),
            pl.BlockSpec((None, tn, k), lambda j, i: (layer, j, 0)),
        ],
        out_specs=pl.BlockSpec((tm, tn), lambda j, i: (i, j)),
        out_shape=jax.ShapeDtypeStruct((m, n_cols), BF16),
        scratch_shapes=[pltpu.VMEM((tn, k), BF16)],
        name="in_proj",
        compiler_params=_cparams(("arbitrary", "arbitrary")),
    )(a, wt_stack)


def _attn_kernel(q_ref, k0_ref, k1_ref, k2_ref, v0_ref, v1_ref, v2_ref, bias_ref, o_ref):
    k_refs = (k0_ref, k1_ref, k2_ref)
    v_refs = (v0_ref, v1_ref, v2_ref)
    qb = q_ref.shape[0]
    half = qb // 2
    scale2 = (A_HEAD_DIM ** -0.5) * LOG2E
    heads = [slice(h * A_HEAD_DIM, (h + 1) * A_HEAD_DIM) for h in range(A_HEADS)]
    nt = (((1,), (1,)), ((), ()))
    for part in range(2):
        r0 = part * half
        c0 = part * half
        c1 = c0 + ATT_KBLKS * qb - half
        spans = [(max(c0, j * qb) - j * qb, min(c1, (j + 1) * qb) - j * qb) for j in range(ATT_KBLKS)]
        s = jnp.stack([
            jnp.concatenate([lax.dot_general(q_ref[r0:r0 + half, sl], k_refs[j][lo:hi, sl], nt,
                                             preferred_element_type=F32)
                             for j, (lo, hi) in enumerate(spans)], axis=1)
            for sl in heads])
        s = s * scale2 + bias_ref[:, r0:r0 + half, c0:c1]
        m = jnp.max(s, axis=-1, keepdims=True)
        e = jnp.exp2(s - m)
        denom = jnp.sum(e, axis=-1, keepdims=True)
        p = e.astype(BF16)
        for h, sl in enumerate(heads):
            acc = None
            off = 0
            for j, (lo, hi) in enumerate(spans):
                term = jnp.dot(p[h, :, off:off + hi - lo], v_refs[j][lo:hi, sl], preferred_element_type=F32)
                acc = term if acc is None else acc + term
                off += hi - lo
            o_ref[r0:r0 + half, sl] = (acc / denom[h]).astype(o_ref.dtype)


def _attn_bias(rel_table):
    qb, kw = ATT_QBLK, ATT_KBLKS * ATT_QBLK
    nh = rel_table.shape[0]
    qi = jnp.arange(qb)[:, None]
    kj = jnp.arange(kw)[None, :]
    off = kw - 1 - (ATT_KBLKS - 1) * qb
    glen = qb + kw
    n_lo = max(0, min(glen, off - REL_CLIP))
    n_lin = max(0, min(glen, off + REL_CLIP + 1) - n_lo)
    n_hi = glen - n_lo - n_lin
    lin0 = n_lo - off + REL_CLIP
    gr = jnp.concatenate([jnp.broadcast_to(rel_table[:, 2 * REL_CLIP:], (nh, n_hi)),
                          rel_table[:, lin0:lin0 + n_lin][:, ::-1],
                          jnp.broadcast_to(rel_table[:, :1], (nh, n_lo))], axis=1).astype(F32) * LOG2E
    c0 = glen - kw
    bias = jnp.tile(gr, (1, qb + 1))[:, c0:c0 + qb * (glen - 1)].reshape(nh, qb, glen - 1)[:, :, :kw]
    qc = qi // CHUNK + (ATT_KBLKS - 1) * (qb // CHUNK)
    kc = kj // CHUNK
    band = (kc <= qc) & (kc >= qc - LEFT_CHUNKS)
    tables = []
    for t in range(ATT_KBLKS):
        ok = band & (kj >= (ATT_KBLKS - 1 - t) * qb)
        tables.append(jnp.where(ok[None], bias, NEG_INF))
    return jnp.stack(tables)


def _attention(p_all, bias, batch, seq):
    n = p_all.shape[0]
    width = A_HEADS * A_HEAD_DIM
    qb = ATT_QBLK
    nb = seq // qb

    def kv_spec(back, colblk):
        return pl.BlockSpec((qb, width), lambda i, b: (b * nb + jnp.maximum(i - back, 0), colblk))

    return pl.pallas_call(
        _attn_kernel,
        grid=(nb, batch),
        in_specs=[
            pl.BlockSpec((qb, width), lambda i, b: (b * nb + i, 0)),
            kv_spec(2, 1), kv_spec(1, 1), kv_spec(0, 1),
            kv_spec(2, 2), kv_spec(1, 2), kv_spec(0, 2),
            pl.BlockSpec((None, A_HEADS, qb, ATT_KBLKS * qb), lambda i, b: (jnp.minimum(i, ATT_KBLKS - 1), 0, 0, 0)),
        ],
        out_specs=pl.BlockSpec((qb, width), lambda i, b: (b * nb + i, 0)),
        out_shape=jax.ShapeDtypeStruct((n, width), BF16),
        name="chunk_attn",
        compiler_params=_cparams(("arbitrary", "arbitrary")),
    )(p_all, p_all, p_all, p_all, p_all, p_all, p_all, bias)


def _log_sigmoid(t):
    return jnp.minimum(t, 0.0) - jnp.log(1.0 + jnp.exp(-jnp.abs(t)))


def _mlstm_kernel(q_ref, k_ref, v_ref, o_ref, g_ref, cq_ref, ck_ref, nw_ref, out_ref,
                  qbuf, kbuf, ct_ref, n_ref, m_ref):
    c = pl.program_id(1)
    L, D = CHUNK, M_HEAD_DIM
    nb = q_ref.shape[0]
    ns = nb * M_HEADS
    tail = 16

    @pl.when(c == 0)
    def _():
        qbuf[:, 0:tail, :] = jnp.zeros((nb, tail, qbuf.shape[2]), BF16)
        kbuf[:, 0:tail, :] = jnp.zeros((nb, tail, kbuf.shape[2]), BF16)
        ct_ref[...] = jnp.zeros(ct_ref.shape, F32)
        n_ref[...] = jnp.zeros(n_ref.shape, F32)
        m_ref[...] = jnp.zeros(m_ref.shape, F32)

    row = lax.broadcasted_iota(jnp.int32, (L, L), 0)
    colm = lax.broadcasted_iota(jnp.int32, (L, L), 1)
    causal = colm <= row
    eye = colm == row
    upper = (row <= colm).astype(F32)

    def to_col(r):
        return jnp.sum(jnp.where(eye, jnp.broadcast_to(r, (ns, L, L)), 0.0), axis=-1, keepdims=True)

    srow = lax.broadcasted_iota(jnp.int32, ((CONV_W - 1) * L, tail + L), 0)
    scol = lax.broadcasted_iota(jnp.int32, ((CONV_W - 1) * L, tail + L), 1)
    stap = srow // L
    shifts = (scol == srow - stap * L + stap + (tail - (CONV_W - 1))).astype(BF16)

    def conv(buf, x_ref, w_ref):
        shifted = []
        for bi in range(nb):
            buf[bi, tail:tail + L, :] = x_ref[bi]
            shifted.append(jnp.dot(shifts, buf[bi], preferred_element_type=F32))
            buf[bi, 0:tail, :] = buf[bi, L:L + tail, :]
        shifted = jnp.stack(shifted)
        acc = x_ref[...].astype(F32) * w_ref[CONV_W - 1:CONV_W, :]
        for j in range(CONV_W - 1):
            acc = acc + shifted[:, j * L:(j + 1) * L, :] * w_ref[j:j + 1, :]
        return acc

    def streams(x):
        return jnp.stack([x[bi, :, h * D:(h + 1) * D] for bi in range(nb) for h in range(M_HEADS)])

    q = streams(_silu(conv(qbuf, q_ref, cq_ref)) * (D ** -0.5))
    k = streams(_silu(conv(kbuf, k_ref, ck_ref)))
    qb16 = q.astype(BF16)
    kb16 = k.astype(BF16)
    v16 = [v_ref[bi, :, h * D:(h + 1) * D] for bi in range(nb) for h in range(M_HEADS)]

    g = g_ref[...]
    ig2 = jnp.concatenate([g[bi, 0:M_HEADS, :] for bi in range(nb)], axis=0)
    lf2 = _log_sigmoid(jnp.concatenate([g[bi, M_HEADS:2 * M_HEADS, :] for bi in range(nb)], axis=0))
    bcum2 = jnp.dot(lf2, upper, preferred_element_type=F32, precision=lax.Precision.HIGHEST)
    ig = jnp.stack([ig2[i:i + 1, :] for i in range(ns)])
    bcum = jnp.stack([bcum2[i:i + 1, :] for i in range(ns)])
    bcum_c = to_col(bcum)
    m_prev = m_ref[...].reshape(ns, 1, 1)

    logd = jnp.where(causal, bcum_c - bcum + ig, NEG_INF)
    inter = bcum_c + m_prev
    m_s = jnp.maximum(jnp.max(logd, axis=-1, keepdims=True), inter)
    nt = (((1,), (1,)), ((), ()))
    s = jnp.stack([lax.dot_general(qb16[i], kb16[i], nt, preferred_element_type=F32) for i in range(ns)])
    w_intra = s * jnp.exp(logd - m_s)
    w_inter = jnp.exp(inter - m_s)
    ct = ct_ref[...].reshape(ns, D, D)
    n_row = n_ref[...].reshape(ns, 1, D)
    wi16 = w_intra.astype(BF16)
    ct16 = ct.astype(BF16)
    num_intra = jnp.stack([jnp.dot(wi16[i], v16[i], preferred_element_type=F32) for i in range(ns)])
    num_inter = jnp.stack([jnp.dot(qb16[i], ct16[i], preferred_element_type=F32) for i in range(ns)])
    num = num_intra + w_inter * num_inter
    den = (jnp.sum(w_intra, axis=-1, keepdims=True)
           + w_inter * jnp.sum(q * n_row, axis=-1, keepdims=True))
    hs = num / jnp.maximum(jnp.abs(den), jnp.exp(-m_s))

    b_last = bcum[:, :, L - 1:L]
    log_wk = b_last - bcum + ig
    m_new = jnp.maximum(b_last + m_prev, jnp.max(log_wk, axis=-1, keepdims=True))
    wk = jnp.exp(log_wk - m_new)
    decay = jnp.exp(b_last + m_prev - m_new)
    kw = k * to_col(wk)
    kw16 = kw.astype(BF16)
    tn = (((0,), (0,)), ((), ()))
    upd = jnp.stack([lax.dot_general(kw16[i], v16[i], tn, preferred_element_type=F32) for i in range(ns)])
    ct_ref[...] = (decay * ct + upd).reshape(ct_ref.shape)
    n_ref[...] = (decay * n_row + jnp.sum(kw, axis=1, keepdims=True)).reshape(n_ref.shape)
    m_ref[...] = m_new.reshape(m_ref.shape)

    og = jnp.stack([o_ref[bi, :, h * D:(h + 1) * D] for bi in range(nb) for h in range(M_HEADS)]).astype(F32)
    nw = jnp.stack([nw_ref[:, h * D:(h + 1) * D] for _ in range(nb) for h in range(M_HEADS)])
    hm = _sigmoid(og) * hs
    y = (hm * lax.rsqrt(jnp.mean(hm * hm, axis=-1, keepdims=True) + EPS) * nw).astype(out_ref.dtype)
    for bi in range(nb):
        for h in range(M_HEADS):
            out_ref[bi, :, h * D:(h + 1) * D] = y[bi * M_HEADS + h]


def _mlstm(p_all, gates_t, conv_q, conv_k, norm_w, layer, batch, seq):
    n, cols = p_all.shape
    width = M_HEADS * M_HEAD_DIM
    nc = seq // CHUNK
    L = CHUNK
    nb = MLSTM_BATCH
    p3 = p_all.reshape(batch, seq, cols)

    def p_spec(colblk):
        return pl.BlockSpec((nb, L, width), lambda g, c: (g, c, colblk))

    out = pl.pallas_call(
        _mlstm_kernel,
        grid=(batch // nb, nc),
        in_specs=[
            p_spec(3), p_spec(4), p_spec(5), p_spec(6),
            pl.BlockSpec((nb, None, 2 * M_HEADS, L), lambda g, c: (g, c, 0, 0)),
            pl.BlockSpec((None, CONV_W, width), lambda g, c: (layer, 0, 0)),
            pl.BlockSpec((None, CONV_W, width), lambda g, c: (layer, 0, 0)),
            pl.BlockSpec((1, width), lambda g, c: (0, 0)),
        ],
        out_specs=pl.BlockSpec((nb, L, width), lambda g, c: (g, c, 0)),
        out_shape=jax.ShapeDtypeStruct((batch, seq, width), BF16),
        scratch_shapes=[
            pltpu.VMEM((nb, L + 16, width), BF16),
            pltpu.VMEM((nb, L + 16, width), BF16),
            pltpu.VMEM((nb, M_HEADS, M_HEAD_DIM, M_HEAD_DIM), F32),
            pltpu.VMEM((nb, M_HEADS, 1, M_HEAD_DIM), F32),
            pltpu.VMEM((nb, M_HEADS, 1, 1), F32),
        ],
        name="mlstm",
        compiler_params=_cparams(("arbitrary", "arbitrary")),
    )(p3, p3, p3, p3, gates_t, conv_q, conv_k, norm_w.reshape(1, width))
    return out.reshape(n, width)


def _tail_kernel(ya_ref, hm_ref, ga0_ref, ga1_ref, gm0_ref, gm1_ref, wa_ref, wm_ref, wo_ref, x_ref, mod_ref,
                 nw_ref, ws_ref, bs_ref, xo_ref, h_ref, s_ref, *, side_cols):
    a = jnp.dot(ya_ref[...], wa_ref[...], preferred_element_type=F32)
    m = jnp.dot(hm_ref[...], wm_ref[...], preferred_element_type=F32)
    ga = jnp.concatenate([ga0_ref[...], ga1_ref[...]], axis=1).astype(F32)
    gm = jnp.concatenate([gm0_ref[...], gm1_ref[...]], axis=1).astype(F32)
    merged = (_sigmoid(ga) * a + _sigmoid(gm) * m).astype(BF16)
    x = x_ref[...] + mod_ref[2:3, :] * jnp.dot(merged, wo_ref[...], preferred_element_type=F32)
    xo_ref[...] = x
    h = _rms_mod(x, nw_ref[...], mod_ref[4:5, :], mod_ref[3:4, :])
    h_ref[...] = _pack_bf16_pairs(h)
    s_ref[...] = _side_proj(h, ws_ref, bs_ref, side_cols, True)


def _mixer_tail(y_attn, h_m, p_all, w_ba16, w_bm16, w_out16, layer, x2, mod_l, norm_w, w_side, b_side, side_cols,
                seq, ga_col0, gm_col0, tm=256):
    n, ka = y_attn.shape
    km = h_m.shape[1]
    d = x2.shape[1]
    half = d // 2
    blocks_per_batch = seq // tm

    def gate_spec(col0, part):
        return pl.BlockSpec((tm, half), lambda i: (i, col0 // half + part))

    return pl.pallas_call(
        functools.partial(_tail_kernel, side_cols=side_cols),
        grid=(n // tm,),
        in_specs=[
            pl.BlockSpec((tm, ka), lambda i: (i, 0)),
            pl.BlockSpec((tm, km), lambda i: (i, 0)),
            gate_spec(ga_col0, 0), gate_spec(ga_col0, 1), gate_spec(gm_col0, 0), gate_spec(gm_col0, 1),
            pl.BlockSpec((None, ka, d), lambda i: (layer, 0, 0)),
            pl.BlockSpec((None, km, d), lambda i: (layer, 0, 0)),
            pl.BlockSpec((None, d, d), lambda i: (layer, 0, 0)),
            pl.BlockSpec((tm, d), lambda i: (i, 0)),
            pl.BlockSpec((None, 6, d), lambda i: (i // blocks_per_batch, 0, 0)),
            pl.BlockSpec((1, d), lambda i: (0, 0)),
            pl.BlockSpec((LANES, d), lambda i: (0, 0)),
            pl.BlockSpec((1, LANES), lambda i: (0, 0)),
        ],
        out_specs=[
            pl.BlockSpec((tm, d), lambda i: (i, 0)),
            pl.BlockSpec((tm, half), lambda i: (i, 0)),
            pl.BlockSpec((tm, LANES), lambda i: (i, 0)),
        ],
        out_shape=[jax.ShapeDtypeStruct((n, d), F32), jax.ShapeDtypeStruct((n, half), jnp.uint32),
                   jax.ShapeDtypeStruct((n, LANES), F32)],
        name="mixer_tail",
        compiler_params=_cparams(("arbitrary",)),
    )(y_attn, h_m, p_all, p_all, p_all, p_all, w_ba16, w_bm16, w_out16, x2, mod_l, norm_w.reshape(1, d),
      w_side, b_side)


def _moe_kernel(tok_ref, src_ref, be_ref, first_ref, nxt_ref, wslot_ref, nact_ref, h_hbm, wg_hbm, wu_hbm, wd_hbm,
                o_ref, xbuf, xb16, wg_st, wu_st, wd_st, xsem, wsem, *, layer):
    i = pl.program_id(0)
    nact = nact_ref[0]
    nbuf, blk = xbuf.shape[0], xbuf.shape[1]
    stages = ((wg_hbm, wg_st), (wu_hbm, wu_st), (wd_hbm, wd_st))

    def weight_copy(k, e, slot):
        return pltpu.make_async_copy(stages[k][0].at[layer, e], stages[k][1].at[slot], wsem.at[slot, k])

    weight_queue = 1

    def start_gather(j):
        base = src_ref[j]
        ring = j % nbuf
        dst = xbuf.at[ring]
        for r in range(blk):
            tok = tok_ref[base + r]
            pltpu.make_async_copy(h_hbm.at[pl.ds(tok, 1)], dst.at[pl.ds(r, 1)], xsem.at[ring]).start()

    def wait_gather(j):
        ring = j % nbuf
        pltpu.make_async_copy(h_hbm.at[pl.ds(0, blk)], xbuf.at[ring], xsem.at[ring]).wait()

    @pl.when(i == 0)
    def _():
        for k in range(3):
            weight_copy(k, be_ref[0], wslot_ref[0]).start(priority=weight_queue)
        for j in range(nbuf - 1):
            start_gather(j)

    @pl.when((i < nact) & (first_ref[i] == 1))
    def _():
        slot = wslot_ref[i]
        e_next = nxt_ref[i]

        @pl.when(e_next >= 0)
        def _():
            for k in range(3):
                weight_copy(k, e_next, 1 - slot).start(priority=weight_queue)

        for k in range(3):
            weight_copy(k, be_ref[i], slot).wait()

    def compute(slot):
        wait_gather(i)
        xb16[...] = _unpack_bf16_pairs(xbuf[i % nbuf]).astype(BF16)
        start_gather(i + nbuf - 1)
        x = xb16[...]
        g = jnp.dot(x, wg_st[slot].astype(BF16), preferred_element_type=F32)
        u = jnp.dot(x, wu_st[slot].astype(BF16), preferred_element_type=F32)
        a = (_silu(g) * u).astype(BF16)
        o_ref[...] = _pack_bf16_pairs(jnp.dot(a, wd_st[slot].astype(BF16), preferred_element_type=F32))

    for static_slot in range(2):
        @pl.when((i < nact) & (wslot_ref[i] == static_slot))
        def _(static_slot=static_slot):
            compute(static_slot)

    @pl.when(i >= nact)
    def _():
        @pl.when(i < nact + nbuf - 1)
        def _():
            wait_gather(i)

        o_ref[...] = jnp.zeros(o_ref.shape, o_ref.dtype)


def _moe_experts(h2, tok_src, blk_src, blk_expert, blk_first, blk_next, blk_wslot, n_active, w_gate, w_up, w_down,
                 layer):
    n, dh = h2.shape
    d = 2 * dh
    f = w_gate.shape[3]
    n_steps = blk_expert.shape[0]
    n_blocks = n_steps - (MOE_RING - 2)
    cap = n_blocks * MOE_BLK
    any_spec = pl.BlockSpec(memory_space=pl.ANY)
    grid_spec = pltpu.PrefetchScalarGridSpec(
        num_scalar_prefetch=7,
        grid=(n_steps,),
        in_specs=[any_spec, any_spec, any_spec, any_spec],
        out_specs=pl.BlockSpec((MOE_BLK, dh), lambda i, *_: (jnp.minimum(i, n_blocks - 1), 0)),
        scratch_shapes=[
            pltpu.VMEM((MOE_RING, MOE_BLK, dh), jnp.uint32),
            pltpu.VMEM((MOE_BLK, d), BF16),
            pltpu.VMEM((2, d, f), F32), pltpu.VMEM((2, d, f), F32), pltpu.VMEM((2, f, d), F32),
            pltpu.SemaphoreType.DMA((MOE_RING,)),
            pltpu.SemaphoreType.DMA((2, 3)),
        ],
    )
    return pl.pallas_call(
        functools.partial(_moe_kernel, layer=layer),
        grid_spec=grid_spec,
        out_shape=jax.ShapeDtypeStruct((cap, dh), jnp.uint32),
        name="moe_experts",
        compiler_params=_cparams(("arbitrary",)),
    )(tok_src, blk_src, blk_expert, blk_first, blk_next, blk_wslot, n_active, h2, w_gate, w_up, w_down)


def _combine_kernel(pos_ref, yb_hbm, x_ref, w_ref, mod_ref, nw_ref, nmod_ref, ws_ref, bs_ref, *out_and_scratch,
                    last, side_cols):
    if last:
        o_ref, buf, sem = out_and_scratch
    else:
        o_ref, h_ref, s_ref, buf, sem = out_and_scratch
    i = pl.program_id(0)
    nsteps = pl.num_programs(0)
    t = x_ref.shape[0]

    def start(j, slot):
        base = j * (t * TOP_K)
        dst = buf.at[slot]
        for r in range(t):
            for k in range(TOP_K):
                p = pos_ref[base + r * TOP_K + k]
                pltpu.make_async_copy(yb_hbm.at[pl.ds(p, 1)], dst.at[k, pl.ds(r, 1)], sem.at[slot]).start()

    def wait(slot):
        for k in range(TOP_K):
            pltpu.make_async_copy(yb_hbm.at[pl.ds(0, t)], buf.at[slot, k], sem.at[slot]).wait()

    @pl.when(i == 0)
    def _():
        start(0, 0)

    slot = i % 2

    @pl.when(i + 1 < nsteps)
    def _():
        start(i + 1, 1 - slot)

    wait(slot)
    w = w_ref[...]
    y = w[:, 0:1] * _unpack_bf16_pairs(buf[slot, 0]) + w[:, 1:2] * _unpack_bf16_pairs(buf[slot, 1])
    x = x_ref[...] + mod_ref[5:6, :] * y
    if last:
        o_ref[...] = x * lax.rsqrt(jnp.mean(x * x, axis=-1, keepdims=True) + EPS) * nw_ref[...]
    else:
        o_ref[...] = x
        h = _rms_mod(x, nw_ref[...], nmod_ref[1:2, :], nmod_ref[0:1, :])
        h_ref[...] = h.astype(h_ref.dtype)
        s_ref[...] = _side_proj(h, ws_ref, bs_ref, side_cols, False)


def _combine(yb, pos, weights, x2, mod_l, seq, next_norm_w, next_mod, w_side, side_spec, b_side, side_cols, last,
             tm=256):
    n, d = x2.shape
    blocks_per_batch = seq // tm
    row_spec = pl.BlockSpec((tm, d), lambda i, *_: (i, 0))
    mod_spec = pl.BlockSpec((None, 6, d), lambda i, *_: (i // blocks_per_batch, 0, 0))
    if last:
        out_specs = row_spec
        out_shape = jax.ShapeDtypeStruct((n, d), F32)
    else:
        out_specs = [row_spec, row_spec, pl.BlockSpec((tm, LANES), lambda i, *_: (i, 0))]
        out_shape = [jax.ShapeDtypeStruct((n, d), F32), jax.ShapeDtypeStruct((n, d), BF16),
                     jax.ShapeDtypeStruct((n, LANES), F32)]
    grid_spec = pltpu.PrefetchScalarGridSpec(
        num_scalar_prefetch=1,
        grid=(n // tm,),
        in_specs=[
            pl.BlockSpec(memory_space=pl.ANY),
            row_spec,
            pl.BlockSpec((tm, TOP_K), lambda i, *_: (i, 0)),
            mod_spec,
            pl.BlockSpec((1, d), lambda i, *_: (0, 0)),
            mod_spec,
            side_spec,
            pl.BlockSpec((1, LANES), lambda i, *_: (0, 0)),
        ],
        out_specs=out_specs,
        scratch_shapes=[pltpu.VMEM((2, TOP_K, tm, yb.shape[1]), yb.dtype), pltpu.SemaphoreType.DMA((2,))],
    )
    return pl.pallas_call(
        functools.partial(_combine_kernel, last=last, side_cols=side_cols),
        grid_spec=grid_spec,
        out_shape=out_shape,
        name="moe_combine",
        compiler_params=_cparams(("arbitrary",)),
    )(pos.reshape(-1), yb, x2, weights, mod_l, next_norm_w.reshape(1, d), next_mod, w_side, b_side)


def _route(logits):
    n_tok = logits.shape[0]
    rows = jnp.arange(n_tok)
    coarse = logits[:, :N_GROUPS]
    grp = jnp.argmax(coarse, axis=-1)
    p_grp = jax.nn.softmax(coarse, axis=-1)[rows, grp]
    fine = logits[:, N_GROUPS:N_GROUPS + N_EXPERTS].reshape(n_tok, N_GROUPS, EXPERTS_PER_GROUP)
    top_val, top_idx = lax.top_k(fine[rows, grp], TOP_K)
    weights = p_grp[:, None] * jax.nn.softmax(top_val, axis=-1)
    expert = grp[:, None] * EXPERTS_PER_GROUP + top_idx
    return expert.astype(jnp.int32), weights


def _dispatch(expert):
    n_tok = expert.shape[0]
    n_assign = n_tok * TOP_K
    cap = n_assign + N_EXPERTS * MOE_BLK
    n_blocks = cap // MOE_BLK + MOE_RING - 2
    e_flat = expert.reshape(-1)
    onehot = (e_flat[:, None] == jnp.arange(N_EXPERTS, dtype=jnp.int32)[None, :]).astype(jnp.int32)
    cum = jnp.cumsum(onehot, axis=0)
    counts = cum[-1]
    rank = jnp.sum(onehot * (cum - 1), axis=1)
    padded = ((counts + MOE_BLK - 1) // MOE_BLK) * MOE_BLK
    pad_ends = jnp.cumsum(padded)
    pad_starts = pad_ends - padded
    dest = (jnp.sum(onehot * pad_starts[None, :], axis=1) + rank).astype(jnp.int32)
    order = jnp.argsort(e_flat, stable=True)
    tok_src = jnp.concatenate([(order // TOP_K).astype(jnp.int32), jnp.zeros((MOE_BLK,), jnp.int32)])
    starts = jnp.cumsum(counts) - counts
    blk_start = jnp.arange(n_blocks, dtype=jnp.int32) * MOE_BLK
    blk_expert = jnp.minimum(jnp.sum((pad_ends[None, :] <= blk_start[:, None]).astype(jnp.int32), axis=1),
                             N_EXPERTS - 1).astype(jnp.int32)
    blk_src = jnp.clip(blk_start - (pad_starts - starts)[blk_expert], 0, n_assign).astype(jnp.int32)
    n_active = (pad_ends[-1] // MOE_BLK).astype(jnp.int32)
    prev = jnp.concatenate([jnp.full((1,), -1, jnp.int32), blk_expert[:-1]])
    blk_first = (blk_expert != prev).astype(jnp.int32)
    run_end = pad_ends[blk_expert] // MOE_BLK
    blk_next = jnp.where(run_end < n_active, blk_expert[jnp.minimum(run_end, n_blocks - 1)], -1).astype(jnp.int32)
    blk_wslot = ((jnp.cumsum((counts > 0).astype(jnp.int32)) - 1)[blk_expert] % 2).astype(jnp.int32)
    return (tok_src, blk_src, blk_expert, blk_first, blk_next, blk_wslot, n_active.reshape(1),
            dest.reshape(n_tok, TOP_K))


def kernel(x, c, ada_w, ada_b, norm1_w, norm2_w, w_in, conv_q, conv_k, igate_b, fgate_b, rel_bias,
           mlstm_norm_w, w_branch_attn, w_branch_mlstm, w_out, router_coarse_w, router_coarse_b,
           router_fine_w, router_fine_b, w_gate, w_up, w_down, final_norm_w):
    b, s, d = x.shape
    depth = ada_w.shape[0]
    n = b * s
    nc = s // CHUNK
    a_width = A_HEADS * A_HEAD_DIM
    m_width = M_HEADS * M_HEAD_DIM
    main_cols = 3 * a_width + 4 * m_width + 2 * d
    ga_col0 = 3 * a_width + 4 * m_width
    gm_col0 = ga_col0 + d

    mod = _ada_mod(c, ada_w, ada_b)
    x2 = x.reshape(n, d)

    w_in_t = jnp.swapaxes(w_in, 1, 2)
    w_ba16 = w_branch_attn.astype(BF16)
    w_bm16 = w_branch_mlstm.astype(BF16)
    w_out16 = w_out.astype(BF16)

    def gate_spec(l):
        return pl.BlockSpec((None, LANES, d), lambda i, *_: (l, main_cols // LANES, 0))

    def gate_bias(l):
        return jnp.zeros((1, LANES), F32).at[0, :M_HEADS].set(igate_b[l]).at[0, M_HEADS:2 * M_HEADS].set(fgate_b[l])

    h, gates = _norm_mod(x2, norm1_w[0], mod[0], w_in_t, gate_spec(0), 2 * M_HEADS, gate_bias(0), seq=s,
                         shift_row=0, scale_row=1, precise=False, out_dtype=BF16)
    out = None
    for l in range(depth):
        w_r = (jnp.zeros((LANES, d), F32).at[:N_GROUPS].set(router_coarse_w[l].T)
               .at[N_GROUPS:N_GROUPS + N_EXPERTS].set(router_fine_w[l].T))
        b_r = (jnp.zeros((1, LANES), F32).at[0, :N_GROUPS].set(router_coarse_b[l])
               .at[0, N_GROUPS:N_GROUPS + N_EXPERTS].set(router_fine_b[l]))

        p_all = _proj(h, w_in_t, l, main_cols)
        y_attn = _attention(p_all, _attn_bias(rel_bias[l]), b, s)
        gates_t = gates[:, :2 * M_HEADS].reshape(b, nc, CHUNK, 2 * M_HEADS).transpose(0, 1, 3, 2)
        h_m = _mlstm(p_all, gates_t, conv_q, conv_k, mlstm_norm_w[l], l, b, s)
        x2, h2, logits = _mixer_tail(y_attn, h_m, p_all, w_ba16, w_bm16, w_out16, l, x2, mod[l], norm2_w[l],
                                     w_r, b_r, N_GROUPS + N_EXPERTS, s, ga_col0, gm_col0)
        expert, weights = _route(logits)
        tok_src, blk_src, blk_expert, blk_first, blk_next, blk_wslot, n_active, pos = _dispatch(expert)
        yb = _moe_experts(h2, tok_src, blk_src, blk_expert, blk_first, blk_next, blk_wslot, n_active,
                          w_gate, w_up, w_down, l)
        if l + 1 < depth:
            x2, h, gates = _combine(yb, pos, weights, x2, mod[l], s, norm1_w[l + 1], mod[l + 1], w_in_t,
                                    gate_spec(l + 1), gate_bias(l + 1), 2 * M_HEADS, last=False)
        else:
            out = _combine(yb, pos, weights, x2, mod[l], s, final_norm_w, mod[l], w_in_t, gate_spec(l),
                           gate_bias(l), 2 * M_HEADS, last=True)

    return out.reshape(b, s, d)
```

```python
import functools

import jax
import jax.numpy as jnp
from jax import lax
from jax.experimental import pallas as pl
from jax.experimental.pallas import tpu as pltpu

F32 = jnp.float32
BF16 = jnp.bfloat16

EPS = 1e-6
NEG_INF = -1e30
LOG2E = 1.4426950408889634
CHUNK = 64
LEFT_CHUNKS = 8
REL_CLIP = 256
A_HEADS = 8
A_HEAD_DIM = 128
M_HEADS = 4
M_HEAD_DIM = 256
CONV_W = 4
N_GROUPS = 4
EXPERTS_PER_GROUP = 8
N_EXPERTS = N_GROUPS * EXPERTS_PER_GROUP
TOP_K = 2

LANES = 128
VMEM_LIMIT = 60 * 1024 * 1024

ATT_QBLK = 256
ATT_KBLKS = 3
MLSTM_BATCH = 4
MOE_RING = 4
COMBINE_RING = 3
MOE_BLK = 256


def _cparams(sem):
    return pltpu.CompilerParams(dimension_semantics=sem, vmem_limit_bytes=VMEM_LIMIT)


def _sigmoid(t):
    return 1.0 / (1.0 + jnp.exp(-t))


def _silu(t):
    return t * _sigmoid(t)


def _pack_bf16_pairs(x):
    c = x.shape[1] // 2
    lo = lax.bitcast_convert_type(x[:, :c].astype(BF16).astype(F32), jnp.uint32)
    hi = lax.bitcast_convert_type(x[:, c:].astype(BF16).astype(F32), jnp.uint32)
    return (lo >> 16) | hi


def _unpack_bf16_pairs(w):
    lo = lax.bitcast_convert_type(w << 16, F32)
    hi = lax.bitcast_convert_type(w & jnp.uint32(0xFFFF0000), F32)
    return jnp.concatenate([lo, hi], axis=1)


def _ada_kernel(c_ref, w_ref, b_ref, o_ref):
    w = w_ref[...].astype(BF16)
    r = jnp.dot(c_ref[...], w, preferred_element_type=F32)
    bp = o_ref.shape[0]
    o_ref[...] = r[:bp] + r[bp:] + b_ref[...]


def _ada_mod(c, ada_w, ada_b):
    depth, d, n6 = ada_w.shape
    b = c.shape[0]
    bp = 8
    c_pad = jnp.zeros((bp, d), F32).at[:b].set(c)
    c_hi = c_pad.astype(BF16)
    c_lo = (c_pad - c_hi.astype(F32)).astype(BF16)
    c2 = jnp.concatenate([c_hi, c_lo], axis=0)
    tn = 1024
    out = pl.pallas_call(
        _ada_kernel,
        grid=(depth, n6 // tn),
        in_specs=[
            pl.BlockSpec((2 * bp, d), lambda l, j: (0, 0)),
            pl.BlockSpec((None, d, tn), lambda l, j: (l, 0, j)),
            pl.BlockSpec((None, 1, tn), lambda l, j: (l, 0, j)),
        ],
        out_specs=pl.BlockSpec((None, bp, tn), lambda l, j: (l, 0, j)),
        out_shape=jax.ShapeDtypeStruct((depth, bp, n6), F32),
        name="ada_mod",
        compiler_params=_cparams(("arbitrary", "arbitrary")),
    )(c2, ada_w, ada_b.reshape(depth, 1, n6))
    return out[:, :b].reshape(depth, b, 6, d)


def _rms_mod(x, nw, scale, shift):
    y = x * lax.rsqrt(jnp.mean(x * x, axis=-1, keepdims=True) + EPS)
    return (y * nw) * (1.0 + scale) + shift


def _side_proj(h, ws_ref, bs_ref, side_cols, precise):
    wrow = lax.broadcasted_iota(jnp.int32, ws_ref.shape, 0)
    ws = jnp.where(wrow < side_cols, ws_ref[...], 0.0)
    nt = (((1,), (1,)), ((), ()))
    if precise:
        h_hi = h.astype(BF16)
        h_lo = (h - h_hi.astype(F32)).astype(BF16)
        w_hi = ws.astype(BF16)
        w_lo = (ws - w_hi.astype(F32)).astype(BF16)
        s = (lax.dot_general(h_hi, w_hi, nt, preferred_element_type=F32)
             + lax.dot_general(h_hi, w_lo, nt, preferred_element_type=F32)
             + lax.dot_general(h_lo, w_hi, nt, preferred_element_type=F32))
    else:
        s = lax.dot_general(h.astype(BF16), ws.astype(BF16), nt, preferred_element_type=F32)
    return s + bs_ref[...]


def _norm_kernel(x_ref, nw_ref, mod_ref, ws_ref, bs_ref, h_ref, s_ref, *, shift_row, scale_row, precise,
                 side_cols):
    h = _rms_mod(x_ref[...], nw_ref[...], mod_ref[scale_row:scale_row + 1, :], mod_ref[shift_row:shift_row + 1, :])
    h_ref[...] = _pack_bf16_pairs(h) if h_ref.dtype == jnp.uint32 else h.astype(h_ref.dtype)
    s_ref[...] = _side_proj(h, ws_ref, bs_ref, side_cols, precise)


def _norm_mod(x2, norm_w, mod_l, w_side, side_spec, side_cols, b_side, *, seq, shift_row, scale_row, precise,
              out_dtype, tm=512):
    n, d = x2.shape
    tm = min(tm, seq)
    blocks_per_batch = seq // tm
    dh = d // 2 if out_dtype == jnp.uint32 else d
    kern = functools.partial(_norm_kernel, shift_row=shift_row, scale_row=scale_row, precise=precise,
                             side_cols=side_cols)
    return pl.pallas_call(
        kern,
        grid=(n // tm,),
        in_specs=[
            pl.BlockSpec((tm, d), lambda i: (i, 0)),
            pl.BlockSpec((1, d), lambda i: (0, 0)),
            pl.BlockSpec((None, 6, d), lambda i: (i // blocks_per_batch, 0, 0)),
            side_spec,
            pl.BlockSpec((1, LANES), lambda i: (0, 0)),
        ],
        out_specs=[
            pl.BlockSpec((tm, dh), lambda i: (i, 0)),
            pl.BlockSpec((tm, LANES), lambda i: (i, 0)),
        ],
        out_shape=[jax.ShapeDtypeStruct((n, dh), out_dtype), jax.ShapeDtypeStruct((n, LANES), F32)],
        name="norm_mod",
        compiler_params=_cparams(("arbitrary",)),
    )(x2, norm_w.reshape(1, d), mod_l, w_side, b_side)


def _proj_kernel(a_ref, wt_ref, o_ref, wb_ref):
    @pl.when(pl.program_id(1) == 0)
    def _():
        wb_ref[...] = wt_ref[...].astype(BF16)

    o_ref[...] = lax.dot_general(a_ref[...], wb_ref[...], (((1,), (1,)), ((), ())),
                                 preferred_element_type=F32).astype(o_ref.dtype)


def _proj(a, wt_stack, layer, n_cols, tm=2048, tn=1024):
    m, k = a.shape
    tm = min(tm, m)
    return pl.pallas_call(
        _proj_kernel,
        grid=(n_cols // tn, m // tm),
        in_specs=[
            pl.BlockSpec((tm, k), lambda j, i: (i, 0)),
            pl.BlockSpec((None, tn, k), lambda j, i: (layer, j, 0)),
        ],
        out_specs=pl.BlockSpec((tm, tn), lambda j, i: (i, j)),
        out_shape=jax.ShapeDtypeStruct((m, n_cols), BF16),
        scratch_shapes=[pltpu.VMEM((tn, k), BF16)],
        name="in_proj",
        compiler_params=_cparams(("arbitrary", "arbitrary")),
    )(a, wt_stack)


def _attn_kernel(q_ref, k0_ref, k1_ref, k2_ref, v0_ref, v1_ref, v2_ref, bias_ref, o_ref):
    k_refs = (k0_ref, k1_ref, k2_ref)
    v_refs = (v0_ref, v1_ref, v2_ref)
    qb = q_ref.shape[0]
    half = qb // 2
    scale2 = (A_HEAD_DIM ** -0.5) * LOG2E
    heads = [slice(h * A_HEAD_DIM, (h + 1) * A_HEAD_DIM) for h in range(A_HEADS)]
    nt = (((1,), (1,)), ((), ()))
    for part in range(2):
        r0 = part * half
        c0 = part * half
        c1 = c0 + ATT_KBLKS * qb - half
        spans = [(max(c0, j * qb) - j * qb, min(c1, (j + 1) * qb) - j * qb) for j in range(ATT_KBLKS)]
        s = jnp.stack([
            jnp.concatenate([lax.dot_general(q_ref[r0:r0 + half, sl], k_refs[j][lo:hi, sl], nt,
                                             preferred_element_type=F32)
                             for j, (lo, hi) in enumerate(spans)], axis=1)
            for sl in heads])
        s = s * scale2 + bias_ref[:, r0:r0 + half, c0:c1]
        m = jnp.max(s, axis=-1, keepdims=True)
        e = jnp.exp2(s - m)
        denom = jnp.sum(e, axis=-1, keepdims=True)
        p = e.astype(BF16)
        for h, sl in enumerate(heads):
            acc = None
            off = 0
            for j, (lo, hi) in enumerate(spans):
                term = jnp.dot(p[h, :, off:off + hi - lo], v_refs[j][lo:hi, sl], preferred_element_type=F32)
                acc = term if acc is None else acc + term
                off += hi - lo
            o_ref[r0:r0 + half, sl] = (acc / denom[h]).astype(o_ref.dtype)


def _attn_bias(rel_table):
    qb, kw = ATT_QBLK, ATT_KBLKS * ATT_QBLK
    nh = rel_table.shape[0]
    qi = jnp.arange(qb)[:, None]
    kj = jnp.arange(kw)[None, :]
    off = kw - 1 - (ATT_KBLKS - 1) * qb
    glen = qb + kw
    n_lo = max(0, min(glen, off - REL_CLIP))
    n_lin = max(0, min(glen, off + REL_CLIP + 1) - n_lo)
    n_hi = glen - n_lo - n_lin
    lin0 = n_lo - off + REL_CLIP
    gr = jnp.concatenate([jnp.broadcast_to(rel_table[:, 2 * REL_CLIP:], (nh, n_hi)),
                          rel_table[:, lin0:lin0 + n_lin][:, ::-1],
                          jnp.broadcast_to(rel_table[:, :1], (nh, n_lo))], axis=1).astype(F32) * LOG2E
    c0 = glen - kw
    bias = jnp.tile(gr, (1, qb + 1))[:, c0:c0 + qb * (glen - 1)].reshape(nh, qb, glen - 1)[:, :, :kw]
    qc = qi // CHUNK + (ATT_KBLKS - 1) * (qb // CHUNK)
    kc = kj // CHUNK
    band = (kc <= qc) & (kc >= qc - LEFT_CHUNKS)
    tables = []
    for t in range(ATT_KBLKS):
        ok = band & (kj >= (ATT_KBLKS - 1 - t) * qb)
        tables.append(jnp.where(ok[None], bias, NEG_INF))
    return jnp.stack(tables)


def _attention(p_all, bias, batch, seq):
    n = p_all.shape[0]
    width = A_HEADS * A_HEAD_DIM
    qb = ATT_QBLK
    nb = seq // qb

    def kv_spec(back, colblk):
        return pl.BlockSpec((qb, width), lambda i, b: (b * nb + jnp.maximum(i - back, 0), colblk))

    return pl.pallas_call(
        _attn_kernel,
        grid=(nb, batch),
        in_specs=[
            pl.BlockSpec((qb, width), lambda i, b: (b * nb + i, 0)),
            kv_spec(2, 1), kv_spec(1, 1), kv_spec(0, 1),
            kv_spec(2, 2), kv_spec(1, 2), kv_spec(0, 2),
            pl.BlockSpec((None, A_HEADS, qb, ATT_KBLKS * qb), lambda i, b: (jnp.minimum(i, ATT_KBLKS - 1), 0, 0, 0)),
        ],
        out_specs=pl.BlockSpec((qb, width), lambda i, b: (b * nb + i, 0)),
        out_shape=jax.ShapeDtypeStruct((n, width), BF16),
        name="chunk_attn",
        compiler_params=_cparams(("arbitrary", "arbitrary")),
    )(p_all, p_all, p_all, p_all, p_all, p_all, p_all, bias)


def _log_sigmoid(t):
    return jnp.minimum(t, 0.0) - jnp.log(1.0 + jnp.exp(-jnp.abs(t)))


def _mlstm_kernel(q_ref, k_ref, v_ref, o_ref, g_ref, cq_ref, ck_ref, nw_ref, out_ref,
                  qbuf, kbuf, ct_ref, n_ref, m_ref):
    c = pl.program_id(1)
    L, D = CHUNK, M_HEAD_DIM
    nb = q_ref.shape[0]
    ns = nb * M_HEADS
    tail = 16

    @pl.when(c == 0)
    def _():
        qbuf[:, 0:tail, :] = jnp.zeros((nb, tail, qbuf.shape[2]), BF16)
        kbuf[:, 0:tail, :] = jnp.zeros((nb, tail, kbuf.shape[2]), BF16)
        ct_ref[...] = jnp.zeros(ct_ref.shape, F32)
        n_ref[...] = jnp.zeros(n_ref.shape, F32)
        m_ref[...] = jnp.zeros(m_ref.shape, F32)

    row = lax.broadcasted_iota(jnp.int32, (L, L), 0)
    colm = lax.broadcasted_iota(jnp.int32, (L, L), 1)
    causal = colm <= row
    eye = colm == row
    upper = (row <= colm).astype(F32)

    def to_col(r):
        return jnp.sum(jnp.where(eye, jnp.broadcast_to(r, (ns, L, L)), 0.0), axis=-1, keepdims=True)

    srow = lax.broadcasted_iota(jnp.int32, ((CONV_W - 1) * L, tail + L), 0)
    scol = lax.broadcasted_iota(jnp.int32, ((CONV_W - 1) * L, tail + L), 1)
    stap = srow // L
    shifts = (scol == srow - stap * L + stap + (tail - (CONV_W - 1))).astype(BF16)

    def conv(buf, x_ref, w_ref):
        shifted = []
        for bi in range(nb):
            buf[bi, tail:tail + L, :] = x_ref[bi]
            shifted.append(jnp.dot(shifts, buf[bi], preferred_element_type=F32))
            buf[bi, 0:tail, :] = buf[bi, L:L + tail, :]
        shifted = jnp.stack(shifted)
        acc = x_ref[...].astype(F32) * w_ref[CONV_W - 1:CONV_W, :]
        for j in range(CONV_W - 1):
            acc = acc + shifted[:, j * L:(j + 1) * L, :] * w_ref[j:j + 1, :]
        return acc

    def streams(x):
        return jnp.stack([x[bi, :, h * D:(h + 1) * D] for bi in range(nb) for h in range(M_HEADS)])

    q = streams(_silu(conv(qbuf, q_ref, cq_ref)) * (D ** -0.5))
    k = streams(_silu(conv(kbuf, k_ref, ck_ref)))
    qb16 = q.astype(BF16)
    kb16 = k.astype(BF16)
    v16 = [v_ref[bi, :, h * D:(h + 1) * D] for bi in range(nb) for h in range(M_HEADS)]

    g = g_ref[...]
    ig2 = jnp.concatenate([g[bi, 0:M_HEADS, :] for bi in range(nb)], axis=0)
    lf2 = _log_sigmoid(jnp.concatenate([g[bi, M_HEADS:2 * M_HEADS, :] for bi in range(nb)], axis=0))
    bcum2 = jnp.dot(lf2, upper, preferred_element_type=F32, precision=lax.Precision.HIGHEST)
    ig = jnp.stack([ig2[i:i + 1, :] for i in range(ns)])
    bcum = jnp.stack([bcum2[i:i + 1, :] for i in range(ns)])
    bcum_c = to_col(bcum)
    m_prev = m_ref[...].reshape(ns, 1, 1)

    logd = jnp.where(causal, bcum_c - bcum + ig, NEG_INF)
    inter = bcum_c + m_prev
    m_s = jnp.maximum(jnp.max(logd, axis=-1, keepdims=True), inter)
    nt = (((1,), (1,)), ((), ()))
    s = jnp.stack([lax.dot_general(qb16[i], kb16[i], nt, preferred_element_type=F32) for i in range(ns)])
    w_intra = s * jnp.exp(logd - m_s)
    w_inter = jnp.exp(inter - m_s)
    ct = ct_ref[...].reshape(ns, D, D)
    n_row = n_ref[...].reshape(ns, 1, D)
    wi16 = w_intra.astype(BF16)
    ct16 = ct.astype(BF16)
    num_intra = jnp.stack([jnp.dot(wi16[i], v16[i], preferred_element_type=F32) for i in range(ns)])
    num_inter = jnp.stack([jnp.dot(qb16[i], ct16[i], preferred_element_type=F32) for i in range(ns)])
    num = num_intra + w_inter * num_inter
    den = (jnp.sum(w_intra, axis=-1, keepdims=True)
           + w_inter * jnp.sum(q * n_row, axis=-1, keepdims=True))
    hs = num / jnp.maximum(jnp.abs(den), jnp.exp(-m_s))

    b_last = bcum[:, :, L - 1:L]
    log_wk = b_last - bcum + ig
    m_new = jnp.maximum(b_last + m_prev, jnp.max(log_wk, axis=-1, keepdims=True))
    wk = jnp.exp(log_wk - m_new)
    decay = jnp.exp(b_last + m_prev - m_new)
    kw = k * to_col(wk)
    kw16 = kw.astype(BF16)
    tn = (((0,), (0,)), ((), ()))
    upd = jnp.stack([lax.dot_general(kw16[i], v16[i], tn, preferred_element_type=F32) for i in range(ns)])
    ct_ref[...] = (decay * ct + upd).reshape(ct_ref.shape)
    n_ref[...] = (decay * n_row + jnp.sum(kw, axis=1, keepdims=True)).reshape(n_ref.shape)
    m_ref[...] = m_new.reshape(m_ref.shape)

    og = jnp.stack([o_ref[bi, :, h * D:(h + 1) * D] for bi in range(nb) for h in range(M_HEADS)]).astype(F32)
    nw = jnp.stack([nw_ref[:, h * D:(h + 1) * D] for _ in range(nb) for h in range(M_HEADS)])
    hm = _sigmoid(og) * hs
    y = (hm * lax.rsqrt(jnp.mean(hm * hm, axis=-1, keepdims=True) + EPS) * nw).astype(out_ref.dtype)
    for bi in range(nb):
        for h in range(M_HEADS):
            out_ref[bi, :, h * D:(h + 1) * D] = y[bi * M_HEADS + h]


def _mlstm(p_all, gates_t, conv_q, conv_k, norm_w, layer, batch, seq):
    n, cols = p_all.shape
    width = M_HEADS * M_HEAD_DIM
    nc = seq // CHUNK
    L = CHUNK
    nb = MLSTM_BATCH
    p3 = p_all.reshape(batch, seq, cols)

    def p_spec(colblk):
        return pl.BlockSpec((nb, L, width), lambda g, c: (g, c, colblk))

    out = pl.pallas_call(
        _mlstm_kernel,
        grid=(batch // nb, nc),
        in_specs=[
            p_spec(3), p_spec(4), p_spec(5), p_spec(6),
            pl.BlockSpec((nb, None, 2 * M_HEADS, L), lambda g, c: (g, c, 0, 0)),
            pl.BlockSpec((None, CONV_W, width), lambda g, c: (layer, 0, 0)),
            pl.BlockSpec((None, CONV_W, width), lambda g, c: (layer, 0, 0)),
            pl.BlockSpec((1, width), lambda g, c: (0, 0)),
        ],
        out_specs=pl.BlockSpec((nb, L, width), lambda g, c: (g, c, 0)),
        out_shape=jax.ShapeDtypeStruct((batch, seq, width), BF16),
        scratch_shapes=[
            pltpu.VMEM((nb, L + 16, width), BF16),
            pltpu.VMEM((nb, L + 16, width), BF16),
            pltpu.VMEM((nb, M_HEADS, M_HEAD_DIM, M_HEAD_DIM), F32),
            pltpu.VMEM((nb, M_HEADS, 1, M_HEAD_DIM), F32),
            pltpu.VMEM((nb, M_HEADS, 1, 1), F32),
        ],
        name="mlstm",
        compiler_params=_cparams(("arbitrary", "arbitrary")),
    )(p3, p3, p3, p3, gates_t, conv_q, conv_k, norm_w.reshape(1, width))
    return out.reshape(n, width)


def _tail_kernel(ya_ref, hm_ref, ga0_ref, ga1_ref, gm0_ref, gm1_ref, wa_ref, wm_ref, wo_ref, x_ref, mod_ref,
                 nw_ref, ws_ref, bs_ref, xo_ref, h_ref, s_ref, *, side_cols):
    a = jnp.dot(ya_ref[...], wa_ref[...], preferred_element_type=F32)
    m = jnp.dot(hm_ref[...], wm_ref[...], preferred_element_type=F32)
    ga = jnp.concatenate([ga0_ref[...], ga1_ref[...]], axis=1).astype(F32)
    gm = jnp.concatenate([gm0_ref[...], gm1_ref[...]], axis=1).astype(F32)
    merged = (_sigmoid(ga) * a + _sigmoid(gm) * m).astype(BF16)
    x = x_ref[...] + mod_ref[2:3, :] * jnp.dot(merged, wo_ref[...], preferred_element_type=F32)
    xo_ref[...] = x
    h = _rms_mod(x, nw_ref[...], mod_ref[4:5, :], mod_ref[3:4, :])
    h_ref[...] = _pack_bf16_pairs(h)
    s_ref[...] = _side_proj(h, ws_ref, bs_ref, side_cols, True)


def _mixer_tail(y_attn, h_m, p_all, w_ba16, w_bm16, w_out16, layer, x2, mod_l, norm_w, w_side, b_side, side_cols,
                seq, ga_col0, gm_col0, tm=256):
    n, ka = y_attn.shape
    km = h_m.shape[1]
    d = x2.shape[1]
    half = d // 2
    blocks_per_batch = seq // tm

    def gate_spec(col0, part):
        return pl.BlockSpec((tm, half), lambda i: (i, col0 // half + part))

    return pl.pallas_call(
        functools.partial(_tail_kernel, side_cols=side_cols),
        grid=(n // tm,),
        in_specs=[
            pl.BlockSpec((tm, ka), lambda i: (i, 0)),
            pl.BlockSpec((tm, km), lambda i: (i, 0)),
            gate_spec(ga_col0, 0), gate_spec(ga_col0, 1), gate_spec(gm_col0, 0), gate_spec(gm_col0, 1),
            pl.BlockSpec((None, ka, d), lambda i: (layer, 0, 0)),
            pl.BlockSpec((None, km, d), lambda i: (layer, 0, 0)),
            pl.BlockSpec((None, d, d), lambda i: (layer, 0, 0)),
            pl.BlockSpec((tm, d), lambda i: (i, 0)),
            pl.BlockSpec((None, 6, d), lambda i: (i // blocks_per_batch, 0, 0)),
            pl.BlockSpec((1, d), lambda i: (0, 0)),
            pl.BlockSpec((LANES, d), lambda i: (0, 0)),
            pl.BlockSpec((1, LANES), lambda i: (0, 0)),
        ],
        out_specs=[
            pl.BlockSpec((tm, d), lambda i: (i, 0)),
            pl.BlockSpec((tm, half), lambda i: (i, 0)),
            pl.BlockSpec((tm, LANES), lambda i: (i, 0)),
        ],
        out_shape=[jax.ShapeDtypeStruct((n, d), F32), jax.ShapeDtypeStruct((n, half), jnp.uint32),
                   jax.ShapeDtypeStruct((n, LANES), F32)],
        name="mixer_tail",
        compiler_params=_cparams(("arbitrary",)),
    )(y_attn, h_m, p_all, p_all, p_all, p_all, w_ba16, w_bm16, w_out16, x2, mod_l, norm_w.reshape(1, d),
      w_side, b_side)


def _moe_kernel(tok_ref, src_ref, be_ref, first_ref, nxt_ref, wslot_ref, nact_ref, h_hbm, wg_hbm, wu_hbm, wd_hbm,
                o_ref, xbuf, xb16, wg_st, wu_st, wd_st, xsem, wsem, *, layer):
    i = pl.program_id(0)
    nact = nact_ref[0]
    nbuf, blk = xbuf.shape[0], xbuf.shape[1]
    stages = ((wg_hbm, wg_st), (wu_hbm, wu_st), (wd_hbm, wd_st))

    def weight_copy(k, e, slot):
        return pltpu.make_async_copy(stages[k][0].at[layer, e], stages[k][1].at[slot], wsem.at[slot, k])

    weight_queue = 1

    def start_gather(j):
        base = src_ref[j]
        ring = j % nbuf
        dst = xbuf.at[ring]
        for r in range(blk):
            tok = tok_ref[base + r]
            pltpu.make_async_copy(h_hbm.at[pl.ds(tok, 1)], dst.at[pl.ds(r, 1)], xsem.at[ring]).start()

    def wait_gather(j):
        ring = j % nbuf
        pltpu.make_async_copy(h_hbm.at[pl.ds(0, blk)], xbuf.at[ring], xsem.at[ring]).wait()

    @pl.when(i == 0)
    def _():
        for k in range(3):
            weight_copy(k, be_ref[0], wslot_ref[0]).start(priority=weight_queue)
        for j in range(nbuf - 1):
            start_gather(j)

    @pl.when((i < nact) & (first_ref[i] == 1))
    def _():
        slot = wslot_ref[i]
        e_next = nxt_ref[i]

        @pl.when(e_next >= 0)
        def _():
            for k in range(3):
                weight_copy(k, e_next, 1 - slot).start(priority=weight_queue)

        for k in range(3):
            weight_copy(k, be_ref[i], slot).wait()

    def compute(slot):
        wait_gather(i)
        xb16[...] = _unpack_bf16_pairs(xbuf[i % nbuf]).astype(BF16)
        start_gather(i + nbuf - 1)
        x = xb16[...]
        g = jnp.dot(x, wg_st[slot].astype(BF16), preferred_element_type=F32)
        u = jnp.dot(x, wu_st[slot].astype(BF16), preferred_element_type=F32)
        a = (_silu(g) * u).astype(BF16)
        o_ref[...] = _pack_bf16_pairs(jnp.dot(a, wd_st[slot].astype(BF16), preferred_element_type=F32))

    for static_slot in range(2):
        @pl.when((i < nact) & (wslot_ref[i] == static_slot))
        def _(static_slot=static_slot):
            compute(static_slot)

    @pl.when(i >= nact)
    def _():
        @pl.when(i < nact + nbuf - 1)
        def _():
            wait_gather(i)

        o_ref[...] = jnp.zeros(o_ref.shape, o_ref.dtype)


def _moe_experts(h2, tok_src, blk_src, blk_expert, blk_first, blk_next, blk_wslot, n_active, w_gate, w_up, w_down,
                 layer):
    n, dh = h2.shape
    d = 2 * dh
    f = w_gate.shape[3]
    n_steps = blk_expert.shape[0]
    n_blocks = n_steps - (MOE_RING - 2)
    cap = n_blocks * MOE_BLK
    any_spec = pl.BlockSpec(memory_space=pl.ANY)
    grid_spec = pltpu.PrefetchScalarGridSpec(
        num_scalar_prefetch=7,
        grid=(n_steps,),
        in_specs=[any_spec, any_spec, any_spec, any_spec],
        out_specs=pl.BlockSpec((MOE_BLK, dh), lambda i, *_: (jnp.minimum(i, n_blocks - 1), 0)),
        scratch_shapes=[
            pltpu.VMEM((MOE_RING, MOE_BLK, dh), jnp.uint32),
            pltpu.VMEM((MOE_BLK, d), BF16),
            pltpu.VMEM((2, d, f), F32), pltpu.VMEM((2, d, f), F32), pltpu.VMEM((2, f, d), F32),
            pltpu.SemaphoreType.DMA((MOE_RING,)),
            pltpu.SemaphoreType.DMA((2, 3)),
        ],
    )
    return pl.pallas_call(
        functools.partial(_moe_kernel, layer=layer),
        grid_spec=grid_spec,
        out_shape=jax.ShapeDtypeStruct((cap, dh), jnp.uint32),
        name="moe_experts",
        compiler_params=_cparams(("arbitrary",)),
    )(tok_src, blk_src, blk_expert, blk_first, blk_next, blk_wslot, n_active, h2, w_gate, w_up, w_down)


def _combine_kernel(pos_ref, yb_hbm, x_ref, w_ref, mod_ref, nw_ref, nmod_ref, ws_ref, bs_ref, *out_and_scratch,
                    last, side_cols):
    if last:
        o_ref, buf, sem = out_and_scratch
    else:
        o_ref, h_ref, s_ref, buf, sem = out_and_scratch
    i = pl.program_id(0)
    nsteps = pl.num_programs(0)
    t = x_ref.shape[0]

    def start(j, slot):
        base = j * (t * TOP_K)
        dst = buf.at[slot]
        for r in range(t):
            for k in range(TOP_K):
                p = pos_ref[base + r * TOP_K + k]
                pltpu.make_async_copy(yb_hbm.at[pl.ds(p, 1)], dst.at[k, pl.ds(r, 1)], sem.at[slot]).start()

    def wait(slot):
        for k in range(TOP_K):
            pltpu.make_async_copy(yb_hbm.at[pl.ds(0, t)], buf.at[slot, k], sem.at[slot]).wait()

    nbuf = buf.shape[0]

    @pl.when(i == 0)
    def _():
        for j in range(nbuf - 1):
            start(j, j)

    slot = i % nbuf
    ahead = i + nbuf - 1

    @pl.when(ahead < nsteps)
    def _():
        start(ahead, ahead % nbuf)

    wait(slot)
    w = w_ref[...]
    y = w[:, 0:1] * _unpack_bf16_pairs(buf[slot, 0]) + w[:, 1:2] * _unpack_bf16_pairs(buf[slot, 1])
    x = x_ref[...] + mod_ref[5:6, :] * y
    if last:
        o_ref[...] = x * lax.rsqrt(jnp.mean(x * x, axis=-1, keepdims=True) + EPS) * nw_ref[...]
    else:
        o_ref[...] = x
        h = _rms_mod(x, nw_ref[...], nmod_ref[1:2, :], nmod_ref[0:1, :])
        h_ref[...] = h.astype(h_ref.dtype)
        s_ref[...] = _side_proj(h, ws_ref, bs_ref, side_cols, False)


def _combine(yb, pos, weights, x2, mod_l, seq, next_norm_w, next_mod, w_side, side_spec, b_side, side_cols, last,
             tm=256):
    n, d = x2.shape
    blocks_per_batch = seq // tm
    row_spec = pl.BlockSpec((tm, d), lambda i, *_: (i, 0))
    mod_spec = pl.BlockSpec((None, 6, d), lambda i, *_: (i // blocks_per_batch, 0, 0))
    if last:
        out_specs = row_spec
        out_shape = jax.ShapeDtypeStruct((n, d), F32)
    else:
        out_specs = [row_spec, row_spec, pl.BlockSpec((tm, LANES), lambda i, *_: (i, 0))]
        out_shape = [jax.ShapeDtypeStruct((n, d), F32), jax.ShapeDtypeStruct((n, d), BF16),
                     jax.ShapeDtypeStruct((n, LANES), F32)]
    grid_spec = pltpu.PrefetchScalarGridSpec(
        num_scalar_prefetch=1,
        grid=(n // tm,),
        in_specs=[
            pl.BlockSpec(memory_space=pl.ANY),
            row_spec,
            pl.BlockSpec((tm, TOP_K), lambda i, *_: (i, 0)),
            mod_spec,
            pl.BlockSpec((1, d), lambda i, *_: (0, 0)),
            mod_spec,
            side_spec,
            pl.BlockSpec((1, LANES), lambda i, *_: (0, 0)),
        ],
        out_specs=out_specs,
        scratch_shapes=[pltpu.VMEM((COMBINE_RING, TOP_K, tm, yb.shape[1]), yb.dtype),
                        pltpu.SemaphoreType.DMA((COMBINE_RING,))],
    )
    return pl.pallas_call(
        functools.partial(_combine_kernel, last=last, side_cols=side_cols),
        grid_spec=grid_spec,
        out_shape=out_shape,
        name="moe_combine",
        compiler_params=_cparams(("arbitrary",)),
    )(pos.reshape(-1), yb, x2, weights, mod_l, next_norm_w.reshape(1, d), next_mod, w_side, b_side)


def _route(logits):
    n_tok = logits.shape[0]
    rows = jnp.arange(n_tok)
    coarse = logits[:, :N_GROUPS]
    grp = jnp.argmax(coarse, axis=-1)
    p_grp = jax.nn.softmax(coarse, axis=-1)[rows, grp]
    fine = logits[:, N_GROUPS:N_GROUPS + N_EXPERTS].reshape(n_tok, N_GROUPS, EXPERTS_PER_GROUP)
    top_val, top_idx = lax.top_k(fine[rows, grp], TOP_K)
    weights = p_grp[:, None] * jax.nn.softmax(top_val, axis=-1)
    expert = grp[:, None] * EXPERTS_PER_GROUP + top_idx
    return expert.astype(jnp.int32), weights


def _dispatch(expert):
    n_tok = expert.shape[0]
    n_assign = n_tok * TOP_K
    cap = n_assign + N_EXPERTS * MOE_BLK
    n_blocks = cap // MOE_BLK + MOE_RING - 2
    e_flat = expert.reshape(-1)
    onehot = (e_flat[:, None] == jnp.arange(N_EXPERTS, dtype=jnp.int32)[None, :]).astype(jnp.int32)
    cum = jnp.cumsum(onehot, axis=0)
    counts = cum[-1]
    rank = jnp.sum(onehot * (cum - 1), axis=1)
    padded = ((counts + MOE_BLK - 1) // MOE_BLK) * MOE_BLK
    pad_ends = jnp.cumsum(padded)
    pad_starts = pad_ends - padded
    dest = (jnp.sum(onehot * pad_starts[None, :], axis=1) + rank).astype(jnp.int32)
    order = jnp.argsort(e_flat, stable=True)
    tok_src = jnp.concatenate([(order // TOP_K).astype(jnp.int32), jnp.zeros((MOE_BLK,), jnp.int32)])
    starts = jnp.cumsum(counts) - counts
    blk_start = jnp.arange(n_blocks, dtype=jnp.int32) * MOE_BLK
    blk_expert = jnp.minimum(jnp.sum((pad_ends[None, :] <= blk_start[:, None]).astype(jnp.int32), axis=1),
                             N_EXPERTS - 1).astype(jnp.int32)
    blk_src = jnp.clip(blk_start - (pad_starts - starts)[blk_expert], 0, n_assign).astype(jnp.int32)
    n_active = (pad_ends[-1] // MOE_BLK).astype(jnp.int32)
    prev = jnp.concatenate([jnp.full((1,), -1, jnp.int32), blk_expert[:-1]])
    blk_first = (blk_expert != prev).astype(jnp.int32)
    run_end = pad_ends[blk_expert] // MOE_BLK
    blk_next = jnp.where(run_end < n_active, blk_expert[jnp.minimum(run_end, n_blocks - 1)], -1).astype(jnp.int32)
    blk_wslot = ((jnp.cumsum((counts > 0).astype(jnp.int32)) - 1)[blk_expert] % 2).astype(jnp.int32)
    return (tok_src, blk_src, blk_expert, blk_first, blk_next, blk_wslot, n_active.reshape(1),
            dest.reshape(n_tok, TOP_K))


def kernel(x, c, ada_w, ada_b, norm1_w, norm2_w, w_in, conv_q, conv_k, igate_b, fgate_b, rel_bias,
           mlstm_norm_w, w_branch_attn, w_branch_mlstm, w_out, router_coarse_w, router_coarse_b,
           router_fine_w, router_fine_b, w_gate, w_up, w_down, final_norm_w):
    b, s, d = x.shape
    depth = ada_w.shape[0]
    n = b * s
    nc = s // CHUNK
    a_width = A_HEADS * A_HEAD_DIM
    m_width = M_HEADS * M_HEAD_DIM
    main_cols = 3 * a_width + 4 * m_width + 2 * d
    ga_col0 = 3 * a_width + 4 * m_width
    gm_col0 = ga_col0 + d

    mod = _ada_mod(c, ada_w, ada_b)
    x2 = x.reshape(n, d)

    w_in_t = jnp.swapaxes(w_in, 1, 2)
    w_ba16 = w_branch_attn.astype(BF16)
    w_bm16 = w_branch_mlstm.astype(BF16)
    w_out16 = w_out.astype(BF16)

    def gate_spec(l):
        return pl.BlockSpec((None, LANES, d), lambda i, *_: (l, main_cols // LANES, 0))

    def gate_bias(l):
        return jnp.zeros((1, LANES), F32).at[0, :M_HEADS].set(igate_b[l]).at[0, M_HEADS:2 * M_HEADS].set(fgate_b[l])

    h, gates = _norm_mod(x2, norm1_w[0], mod[0], w_in_t, gate_spec(0), 2 * M_HEADS, gate_bias(0), seq=s,
                         shift_row=0, scale_row=1, precise=False, out_dtype=BF16)
    out = None
    for l in range(depth):
        w_r = (jnp.zeros((LANES, d), F32).at[:N_GROUPS].set(router_coarse_w[l].T)
               .at[N_GROUPS:N_GROUPS + N_EXPERTS].set(router_fine_w[l].T))
        b_r = (jnp.zeros((1, LANES), F32).at[0, :N_GROUPS].set(router_coarse_b[l])
               .at[0, N_GROUPS:N_GROUPS + N_EXPERTS].set(router_fine_b[l]))

        p_all = _proj(h, w_in_t, l, main_cols)
        y_attn = _attention(p_all, _attn_bias(rel_bias[l]), b, s)
        gates_t = gates[:, :2 * M_HEADS].reshape(b, nc, CHUNK, 2 * M_HEADS).transpose(0, 1, 3, 2)
        h_m = _mlstm(p_all, gates_t, conv_q, conv_k, mlstm_norm_w[l], l, b, s)
        x2, h2, logits = _mixer_tail(y_attn, h_m, p_all, w_ba16, w_bm16, w_out16, l, x2, mod[l], norm2_w[l],
                                     w_r, b_r, N_GROUPS + N_EXPERTS, s, ga_col0, gm_col0)
        expert, weights = _route(logits)
        tok_src, blk_src, blk_expert, blk_first, blk_next, blk_wslot, n_active, pos = _dispatch(expert)
        yb = _moe_experts(h2, tok_src, blk_src, blk_expert, blk_first, blk_next, blk_wslot, n_active,
                          w_gate, w_up, w_down, l)
        if l + 1 < depth:
            x2, h, gates = _combine(yb, pos, weights, x2, mod[l], s, norm1_w[l + 1], mod[l + 1], w_in_t,
                                    gate_spec(l + 1), gate_bias(l + 1), 2 * M_HEADS, last=False)
        else:
            out = _combine(yb, pos, weights, x2, mod[l], s, final_norm_w, mod[l], w_in_t, gate_spec(l),
                           gate_bias(l), 2 * M_HEADS, last=True)

    return out.reshape(b, s, d)
```

```python
import functools

import jax
import jax.numpy as jnp
from jax import lax
from jax.experimental import pallas as pl
from jax.experimental.pallas import tpu as pltpu

F32 = jnp.float32
BF16 = jnp.bfloat16

EPS = 1e-6
NEG_INF = -1e30
LOG2E = 1.4426950408889634
CHUNK = 64
LEFT_CHUNKS = 8
REL_CLIP = 256
A_HEADS = 8
A_HEAD_DIM = 128
M_HEADS = 4
M_HEAD_DIM = 256
CONV_W = 4
N_GROUPS = 4
EXPERTS_PER_GROUP = 8
N_EXPERTS = N_GROUPS * EXPERTS_PER_GROUP
TOP_K = 2

LANES = 128
VMEM_LIMIT = 60 * 1024 * 1024

ATT_QBLK = 256
ATT_KBLKS = 3
MLSTM_BATCH = 4
MOE_RING = 3
MOE_BLK = 256


def _cparams(sem):
    return pltpu.CompilerParams(dimension_semantics=sem, vmem_limit_bytes=VMEM_LIMIT)


def _sigmoid(t):
    return 1.0 / (1.0 + jnp.exp(-t))


def _silu(t):
    return t * _sigmoid(t)


def _pack_bf16_pairs(x):
    c = x.shape[1] // 2
    lo = lax.bitcast_convert_type(x[:, :c].astype(BF16).astype(F32), jnp.uint32)
    hi = lax.bitcast_convert_type(x[:, c:].astype(BF16).astype(F32), jnp.uint32)
    return (lo >> 16) | hi


def _unpack_bf16_pairs(w):
    lo = lax.bitcast_convert_type(w << 16, F32)
    hi = lax.bitcast_convert_type(w & jnp.uint32(0xFFFF0000), F32)
    return jnp.concatenate([lo, hi], axis=1)


def _ada_kernel(c_ref, w_ref, b_ref, o_ref):
    w = w_ref[...].astype(BF16)
    r = jnp.dot(c_ref[...], w, preferred_element_type=F32)
    bp = o_ref.shape[0]
    o_ref[...] = r[:bp] + r[bp:] + b_ref[...]


def _ada_mod(c, ada_w, ada_b):
    depth, d, n6 = ada_w.shape
    b = c.shape[0]
    bp = 8
    c_pad = jnp.zeros((bp, d), F32).at[:b].set(c)
    c_hi = c_pad.astype(BF16)
    c_lo = (c_pad - c_hi.astype(F32)).astype(BF16)
    c2 = jnp.concatenate([c_hi, c_lo], axis=0)
    tn = 1024
    out = pl.pallas_call(
        _ada_kernel,
        grid=(depth, n6 // tn),
        in_specs=[
            pl.BlockSpec((2 * bp, d), lambda l, j: (0, 0)),
            pl.BlockSpec((None, d, tn), lambda l, j: (l, 0, j)),
            pl.BlockSpec((None, 1, tn), lambda l, j: (l, 0, j)),
        ],
        out_specs=pl.BlockSpec((None, bp, tn), lambda l, j: (l, 0, j)),
        out_shape=jax.ShapeDtypeStruct((depth, bp, n6), F32),
        name="ada_mod",
        compiler_params=_cparams(("arbitrary", "arbitrary")),
    )(c2, ada_w, ada_b.reshape(depth, 1, n6))
    return out[:, :b].reshape(depth, b, 6, d)


def _rms_mod(x, nw, scale, shift):
    y = x * lax.rsqrt(jnp.mean(x * x, axis=-1, keepdims=True) + EPS)
    return (y * nw) * (1.0 + scale) + shift


def _side_proj(h, ws_ref, bs_ref, side_cols, precise):
    wrow = lax.broadcasted_iota(jnp.int32, ws_ref.shape, 0)
    ws = jnp.where(wrow < side_cols, ws_ref[...], 0.0)
    nt = (((1,), (1,)), ((), ()))
    if precise:
        h_hi = h.astype(BF16)
        h_lo = (h - h_hi.astype(F32)).astype(BF16)
        w_hi = ws.astype(BF16)
        w_lo = (ws - w_hi.astype(F32)).astype(BF16)
        s = (lax.dot_general(h_hi, w_hi, nt, preferred_element_type=F32)
             + lax.dot_general(h_hi, w_lo, nt, preferred_element_type=F32)
             + lax.dot_general(h_lo, w_hi, nt, preferred_element_type=F32))
    else:
        s = lax.dot_general(h.astype(BF16), ws.astype(BF16), nt, preferred_element_type=F32)
    return s + bs_ref[...]


def _norm_kernel(x_ref, nw_ref, mod_ref, ws_ref, bs_ref, h_ref, s_ref, *, shift_row, scale_row, precise,
                 side_cols):
    h = _rms_mod(x_ref[...], nw_ref[...], mod_ref[scale_row:scale_row + 1, :], mod_ref[shift_row:shift_row + 1, :])
    h_ref[...] = _pack_bf16_pairs(h) if h_ref.dtype == jnp.uint32 else h.astype(h_ref.dtype)
    s_ref[...] = _side_proj(h, ws_ref, bs_ref, side_cols, precise)


def _norm_mod(x2, norm_w, mod_l, w_side, side_spec, side_cols, b_side, *, seq, shift_row, scale_row, precise,
              out_dtype, tm=512):
    n, d = x2.shape
    tm = min(tm, seq)
    blocks_per_batch = seq // tm
    dh = d // 2 if out_dtype == jnp.uint32 else d
    kern = functools.partial(_norm_kernel, shift_row=shift_row, scale_row=scale_row, precise=precise,
                             side_cols=side_cols)
    return pl.pallas_call(
        kern,
        grid=(n // tm,),
        in_specs=[
            pl.BlockSpec((tm, d), lambda i: (i, 0)),
            pl.BlockSpec((1, d), lambda i: (0, 0)),
            pl.BlockSpec((None, 6, d), lambda i: (i // blocks_per_batch, 0, 0)),
            side_spec,
            pl.BlockSpec((1, LANES), lambda i: (0, 0)),
        ],
        out_specs=[
            pl.BlockSpec((tm, dh), lambda i: (i, 0)),
            pl.BlockSpec((tm, LANES), lambda i: (i, 0)),
        ],
        out_shape=[jax.ShapeDtypeStruct((n, dh), out_dtype), jax.ShapeDtypeStruct((n, LANES), F32)],
        name="norm_mod",
        compiler_params=_cparams(("arbitrary",)),
    )(x2, norm_w.reshape(1, d), mod_l, w_side, b_side)


def _proj_kernel(a_ref, wt_ref, o_ref, wb_ref):
    @pl.when(pl.program_id(1) == 0)
    def _():
        wb_ref[...] = wt_ref[...].astype(BF16)

    o_ref[...] = lax.dot_general(a_ref[...], wb_ref[...], (((1,), (1,)), ((), ())),
                                 preferred_element_type=F32).astype(o_ref.dtype)


def _proj(a, wt_stack, layer, n_cols, tm=2048, tn=1024):
    m, k = a.shape
    tm = min(tm, m)
    return pl.pallas_call(
        _proj_kernel,
        grid=(n_cols // tn, m // tm),
        in_specs=[
            pl.BlockSpec((tm, k), lambda j, i: (i, 0)),
            pl.BlockSpec((None, tn, k), lambda j, i: (layer, j, 0)),
        ],
        out_specs=pl.BlockSpec((tm, tn), lambda j, i: (i, j)),
        out_shape=jax.ShapeDtypeStruct((m, n_cols), BF16),
        scratch_shapes=[pltpu.VMEM((tn, k), BF16)],
        name="in_proj",
        compiler_params=_cparams(("arbitrary", "arbitrary")),
    )(a, wt_stack)


def _attn_kernel(q_ref, k0_ref, k1_ref, k2_ref, v0_ref, v1_ref, v2_ref, bias_ref, o_ref):
    k_refs = (k0_ref, k1_ref, k2_ref)
    v_refs = (v0_ref, v1_ref, v2_ref)
    qb = q_ref.shape[0]
    half = qb // 2
    scale2 = (A_HEAD_DIM ** -0.5) * LOG2E
    heads = [slice(h * A_HEAD_DIM, (h + 1) * A_HEAD_DIM) for h in range(A_HEADS)]
    nt = (((1,), (1,)), ((), ()))
    for part in range(2):
        r0 = part * half
        c0 = part * half
        c1 = c0 + ATT_KBLKS * qb - half
        spans = [(max(c0, j * qb) - j * qb, min(c1, (j + 1) * qb) - j * qb) for j in range(ATT_KBLKS)]
        s = jnp.stack([
            jnp.concatenate([lax.dot_general(q_ref[r0:r0 + half, sl], k_refs[j][lo:hi, sl], nt,
                                             preferred_element_type=F32)
                             for j, (lo, hi) in enumerate(spans)], axis=1)
            for sl in heads])
        s = s * scale2 + bias_ref[:, r0:r0 + half, c0:c1]
        m = jnp.max(s, axis=-1, keepdims=True)
        e = jnp.exp2(s - m)
        denom = jnp.sum(e, axis=-1, keepdims=True)
        p = e.astype(BF16)
        for h, sl in enumerate(heads):
            acc = None
            off = 0
            for j, (lo, hi) in enumerate(spans):
                term = jnp.dot(p[h, :, off:off + hi - lo], v_refs[j][lo:hi, sl], preferred_element_type=F32)
                acc = term if acc is None else acc + term
                off += hi - lo
            o_ref[r0:r0 + half, sl] = (acc / denom[h]).astype(o_ref.dtype)


def _attn_bias(rel_table):
    qb, kw = ATT_QBLK, ATT_KBLKS * ATT_QBLK
    nh = rel_table.shape[0]
    qi = jnp.arange(qb)[:, None]
    kj = jnp.arange(kw)[None, :]
    off = kw - 1 - (ATT_KBLKS - 1) * qb
    glen = qb + kw
    n_lo = max(0, min(glen, off - REL_CLIP))
    n_lin = max(0, min(glen, off + REL_CLIP + 1) - n_lo)
    n_hi = glen - n_lo - n_lin
    lin0 = n_lo - off + REL_CLIP
    gr = jnp.concatenate([jnp.broadcast_to(rel_table[:, 2 * REL_CLIP:], (nh, n_hi)),
                          rel_table[:, lin0:lin0 + n_lin][:, ::-1],
                          jnp.broadcast_to(rel_table[:, :1], (nh, n_lo))], axis=1).astype(F32) * LOG2E
    c0 = glen - kw
    bias = jnp.tile(gr, (1, qb + 1))[:, c0:c0 + qb * (glen - 1)].reshape(nh, qb, glen - 1)[:, :, :kw]
    qc = qi // CHUNK + (ATT_KBLKS - 1) * (qb // CHUNK)
    kc = kj // CHUNK
    band = (kc <= qc) & (kc >= qc - LEFT_CHUNKS)
    tables = []
    for t in range(ATT_KBLKS):
        ok = band & (kj >= (ATT_KBLKS - 1 - t) * qb)
        tables.append(jnp.where(ok[None], bias, NEG_INF))
    return jnp.stack(tables)


def _attention(p_all, bias, batch, seq):
    n = p_all.shape[0]
    width = A_HEADS * A_HEAD_DIM
    qb = ATT_QBLK
    nb = seq // qb

    def kv_spec(back, colblk):
        return pl.BlockSpec((qb, width), lambda i, b: (b * nb + jnp.maximum(i - back, 0), colblk))

    return pl.pallas_call(
        _attn_kernel,
        grid=(nb, batch),
        in_specs=[
            pl.BlockSpec((qb, width), lambda i, b: (b * nb + i, 0)),
            kv_spec(2, 1), kv_spec(1, 1), kv_spec(0, 1),
            kv_spec(2, 2), kv_spec(1, 2), kv_spec(0, 2),
            pl.BlockSpec((None, A_HEADS, qb, ATT_KBLKS * qb), lambda i, b: (jnp.minimum(i, ATT_KBLKS - 1), 0, 0, 0)),
        ],
        out_specs=pl.BlockSpec((qb, width), lambda i, b: (b * nb + i, 0)),
        out_shape=jax.ShapeDtypeStruct((n, width), BF16),
        name="chunk_attn",
        compiler_params=_cparams(("arbitrary", "arbitrary")),
    )(p_all, p_all, p_all, p_all, p_all, p_all, p_all, bias)


def _log_sigmoid(t):
    return jnp.minimum(t, 0.0) - jnp.log(1.0 + jnp.exp(-jnp.abs(t)))


def _mlstm_kernel(q_ref, k_ref, v_ref, o_ref, g_ref, cq_ref, ck_ref, nw_ref, out_ref,
                  qbuf, kbuf, ct_ref, n_ref, m_ref):
    c = pl.program_id(1)
    L, D = CHUNK, M_HEAD_DIM
    nb = q_ref.shape[0]
    ns = nb * M_HEADS
    tail = 16

    @pl.when(c == 0)
    def _():
        qbuf[:, 0:tail, :] = jnp.zeros((nb, tail, qbuf.shape[2]), BF16)
        kbuf[:, 0:tail, :] = jnp.zeros((nb, tail, kbuf.shape[2]), BF16)
        ct_ref[...] = jnp.zeros(ct_ref.shape, F32)
        n_ref[...] = jnp.zeros(n_ref.shape, F32)
        m_ref[...] = jnp.zeros(m_ref.shape, F32)

    row = lax.broadcasted_iota(jnp.int32, (L, L), 0)
    colm = lax.broadcasted_iota(jnp.int32, (L, L), 1)
    causal = colm <= row
    eye = colm == row
    upper = (row <= colm).astype(F32)

    def to_col(r):
        return jnp.sum(jnp.where(eye, jnp.broadcast_to(r, (ns, L, L)), 0.0), axis=-1, keepdims=True)

    srow = lax.broadcasted_iota(jnp.int32, ((CONV_W - 1) * L, tail + L), 0)
    scol = lax.broadcasted_iota(jnp.int32, ((CONV_W - 1) * L, tail + L), 1)
    stap = srow // L
    shifts = (scol == srow - stap * L + stap + (tail - (CONV_W - 1))).astype(BF16)

    def conv(buf, x_ref, w_ref):
        shifted = []
        for bi in range(nb):
            buf[bi, tail:tail + L, :] = x_ref[bi]
            shifted.append(jnp.dot(shifts, buf[bi], preferred_element_type=F32))
            buf[bi, 0:tail, :] = buf[bi, L:L + tail, :]
        shifted = jnp.stack(shifted)
        acc = x_ref[...].astype(F32) * w_ref[CONV_W - 1:CONV_W, :]
        for j in range(CONV_W - 1):
            acc = acc + shifted[:, j * L:(j + 1) * L, :] * w_ref[j:j + 1, :]
        return acc

    def streams(x):
        return jnp.stack([x[bi, :, h * D:(h + 1) * D] for bi in range(nb) for h in range(M_HEADS)])

    q = streams(_silu(conv(qbuf, q_ref, cq_ref)) * (D ** -0.5))
    k = streams(_silu(conv(kbuf, k_ref, ck_ref)))
    qb16 = q.astype(BF16)
    kb16 = k.astype(BF16)
    v16 = [v_ref[bi, :, h * D:(h + 1) * D] for bi in range(nb) for h in range(M_HEADS)]

    g = g_ref[...]
    ig2 = jnp.concatenate([g[bi, 0:M_HEADS, :] for bi in range(nb)], axis=0)
    lf2 = _log_sigmoid(jnp.concatenate([g[bi, M_HEADS:2 * M_HEADS, :] for bi in range(nb)], axis=0))
    bcum2 = jnp.dot(lf2, upper, preferred_element_type=F32, precision=lax.Precision.HIGHEST)
    ig = jnp.stack([ig2[i:i + 1, :] for i in range(ns)])
    bcum = jnp.stack([bcum2[i:i + 1, :] for i in range(ns)])
    bcum_c = to_col(bcum)
    m_prev = m_ref[...].reshape(ns, 1, 1)

    logd = jnp.where(causal, bcum_c - bcum + ig, NEG_INF)
    inter = bcum_c + m_prev
    m_s = jnp.maximum(jnp.max(logd, axis=-1, keepdims=True), inter)
    nt = (((1,), (1,)), ((), ()))
    s = jnp.stack([lax.dot_general(qb16[i], kb16[i], nt, preferred_element_type=F32) for i in range(ns)])
    w_intra = s * jnp.exp(logd - m_s)
    w_inter = jnp.exp(inter - m_s)
    ct = ct_ref[...].reshape(ns, D, D)
    n_row = n_ref[...].reshape(ns, 1, D)
    wi16 = w_intra.astype(BF16)
    ct16 = ct.astype(BF16)
    num_intra = jnp.stack([jnp.dot(wi16[i], v16[i], preferred_element_type=F32) for i in range(ns)])
    num_inter = jnp.stack([jnp.dot(qb16[i], ct16[i], preferred_element_type=F32) for i in range(ns)])
    num = num_intra + w_inter * num_inter
    den = (jnp.sum(w_intra, axis=-1, keepdims=True)
           + w_inter * jnp.sum(q * n_row, axis=-1, keepdims=True))
    hs = num / jnp.maximum(jnp.abs(den), jnp.exp(-m_s))

    b_last = bcum[:, :, L - 1:L]
    log_wk = b_last - bcum + ig
    m_new = jnp.maximum(b_last + m_prev, jnp.max(log_wk, axis=-1, keepdims=True))
    wk = jnp.exp(log_wk - m_new)
    decay = jnp.exp(b_last + m_prev - m_new)
    kw = k * to_col(wk)
    kw16 = kw.astype(BF16)
    tn = (((0,), (0,)), ((), ()))
    upd = jnp.stack([lax.dot_general(kw16[i], v16[i], tn, preferred_element_type=F32) for i in range(ns)])
    ct_ref[...] = (decay * ct + upd).reshape(ct_ref.shape)
    n_ref[...] = (decay * n_row + jnp.sum(kw, axis=1, keepdims=True)).reshape(n_ref.shape)
    m_ref[...] = m_new.reshape(m_ref.shape)

    og = jnp.stack([o_ref[bi, :, h * D:(h + 1) * D] for bi in range(nb) for h in range(M_HEADS)]).astype(F32)
    nw = jnp.stack([nw_ref[:, h * D:(h + 1) * D] for _ in range(nb) for h in range(M_HEADS)])
    hm = _sigmoid(og) * hs
    y = (hm * lax.rsqrt(jnp.mean(hm * hm, axis=-1, keepdims=True) + EPS) * nw).astype(out_ref.dtype)
    for bi in range(nb):
        for h in range(M_HEADS):
            out_ref[bi, :, h * D:(h + 1) * D] = y[bi * M_HEADS + h]


def _mlstm(p_all, gates_t, conv_q, conv_k, norm_w, layer, batch, seq):
    n, cols = p_all.shape
    width = M_HEADS * M_HEAD_DIM
    nc = seq // CHUNK
    L = CHUNK
    nb = MLSTM_BATCH
    p3 = p_all.reshape(batch, seq, cols)

    def p_spec(colblk):
        return pl.BlockSpec((nb, L, width), lambda g, c: (g, c, colblk))

    out = pl.pallas_call(
        _mlstm_kernel,
        grid=(batch // nb, nc),
        in_specs=[
            p_spec(3), p_spec(4), p_spec(5), p_spec(6),
            pl.BlockSpec((nb, None, 2 * M_HEADS, L), lambda g, c: (g, c, 0, 0)),
            pl.BlockSpec((None, CONV_W, width), lambda g, c: (layer, 0, 0)),
            pl.BlockSpec((None, CONV_W, width), lambda g, c: (layer, 0, 0)),
            pl.BlockSpec((1, width), lambda g, c: (0, 0)),
        ],
        out_specs=pl.BlockSpec((nb, L, width), lambda g, c: (g, c, 0)),
        out_shape=jax.ShapeDtypeStruct((batch, seq, width), BF16),
        scratch_shapes=[
            pltpu.VMEM((nb, L + 16, width), BF16),
            pltpu.VMEM((nb, L + 16, width), BF16),
            pltpu.VMEM((nb, M_HEADS, M_HEAD_DIM, M_HEAD_DIM), F32),
            pltpu.VMEM((nb, M_HEADS, 1, M_HEAD_DIM), F32),
            pltpu.VMEM((nb, M_HEADS, 1, 1), F32),
        ],
        name="mlstm",
        compiler_params=_cparams(("arbitrary", "arbitrary")),
    )(p3, p3, p3, p3, gates_t, conv_q, conv_k, norm_w.reshape(1, width))
    return out.reshape(n, width)


def _tail_kernel(ya_ref, hm_ref, ga0_ref, ga1_ref, gm0_ref, gm1_ref, wa_ref, wm_ref, wo_ref, x_ref, mod_ref,
                 nw_ref, ws_ref, bs_ref, xo_ref, h_ref, s_ref, *, side_cols):
    a = jnp.dot(ya_ref[...], wa_ref[...], preferred_element_type=F32)
    m = jnp.dot(hm_ref[...], wm_ref[...], preferred_element_type=F32)
    ga = jnp.concatenate([ga0_ref[...], ga1_ref[...]], axis=1).astype(F32)
    gm = jnp.concatenate([gm0_ref[...], gm1_ref[...]], axis=1).astype(F32)
    merged = (_sigmoid(ga) * a + _sigmoid(gm) * m).astype(BF16)
    x = x_ref[...] + mod_ref[2:3, :] * jnp.dot(merged, wo_ref[...], preferred_element_type=F32)
    xo_ref[...] = x
    h = _rms_mod(x, nw_ref[...], mod_ref[4:5, :], mod_ref[3:4, :])
    h_ref[...] = _pack_bf16_pairs(h)
    s_ref[...] = _side_proj(h, ws_ref, bs_ref, side_cols, True)


def _mixer_tail(y_attn, h_m, p_all, w_ba16, w_bm16, w_out16, layer, x2, mod_l, norm_w, w_side, b_side, side_cols,
                seq, ga_col0, gm_col0, tm=256):
    n, ka = y_attn.shape
    km = h_m.shape[1]
    d = x2.shape[1]
    half = d // 2
    blocks_per_batch = seq // tm

    def gate_spec(col0, part):
        return pl.BlockSpec((tm, half), lambda i: (i, col0 // half + part))

    return pl.pallas_call(
        functools.partial(_tail_kernel, side_cols=side_cols),
        grid=(n // tm,),
        in_specs=[
            pl.BlockSpec((tm, ka), lambda i: (i, 0)),
            pl.BlockSpec((tm, km), lambda i: (i, 0)),
            gate_spec(ga_col0, 0), gate_spec(ga_col0, 1), gate_spec(gm_col0, 0), gate_spec(gm_col0, 1),
            pl.BlockSpec((None, ka, d), lambda i: (layer, 0, 0)),
            pl.BlockSpec((None, km, d), lambda i: (layer, 0, 0)),
            pl.BlockSpec((None, d, d), lambda i: (layer, 0, 0)),
            pl.BlockSpec((tm, d), lambda i: (i, 0)),
            pl.BlockSpec((None, 6, d), lambda i: (i // blocks_per_batch, 0, 0)),
            pl.BlockSpec((1, d), lambda i: (0, 0)),
            pl.BlockSpec((LANES, d), lambda i: (0, 0)),
            pl.BlockSpec((1, LANES), lambda i: (0, 0)),
        ],
        out_specs=[
            pl.BlockSpec((tm, d), lambda i: (i, 0)),
            pl.BlockSpec((tm, half), lambda i: (i, 0)),
            pl.BlockSpec((tm, LANES), lambda i: (i, 0)),
        ],
        out_shape=[jax.ShapeDtypeStruct((n, d), F32), jax.ShapeDtypeStruct((n, half), jnp.uint32),
                   jax.ShapeDtypeStruct((n, LANES), F32)],
        name="mixer_tail",
        compiler_params=_cparams(("arbitrary",)),
    )(y_attn, h_m, p_all, p_all, p_all, p_all, w_ba16, w_bm16, w_out16, x2, mod_l, norm_w.reshape(1, d),
      w_side, b_side)


def _moe_kernel(tok_ref, src_ref, be_ref, first_ref, nxt_ref, wslot_ref, nact_ref, h_hbm, wg_hbm, wu_hbm, wd_hbm,
                o_ref, xbuf, xb16, wg_st, wu_st, wd_st, xsem, wsem, *, layer):
    i = pl.program_id(0)
    nact = nact_ref[0]
    nbuf, blk = xbuf.shape[0], xbuf.shape[1]
    stages = ((wg_hbm, wg_st), (wu_hbm, wu_st), (wd_hbm, wd_st))

    def weight_copy(k, e, slot):
        return pltpu.make_async_copy(stages[k][0].at[layer, e], stages[k][1].at[slot], wsem.at[slot, k])

    weight_queue = 1

    def start_gather(j):
        base = src_ref[j]
        ring = j % nbuf
        dst = xbuf.at[ring]
        for r in range(blk):
            tok = tok_ref[base + r]
            pltpu.make_async_copy(h_hbm.at[pl.ds(tok, 1)], dst.at[pl.ds(r, 1)], xsem.at[ring]).start()

    def wait_gather(j):
        ring = j % nbuf
        pltpu.make_async_copy(h_hbm.at[pl.ds(0, blk)], xbuf.at[ring], xsem.at[ring]).wait()

    @pl.when(i == 0)
    def _():
        for k in range(3):
            weight_copy(k, be_ref[0], wslot_ref[0]).start(priority=weight_queue)
        for j in range(nbuf - 1):
            start_gather(j)

    @pl.when((i < nact) & (first_ref[i] == 1))
    def _():
        slot = wslot_ref[i]
        e_next = nxt_ref[i]

        @pl.when(e_next >= 0)
        def _():
            for k in range(3):
                weight_copy(k, e_next, 1 - slot).start(priority=weight_queue)

        for k in range(3):
            weight_copy(k, be_ref[i], slot).wait()

    def compute(slot):
        wait_gather(i)
        xb16[...] = _unpack_bf16_pairs(xbuf[i % nbuf]).astype(BF16)
        start_gather(i + nbuf - 1)
        x = xb16[...]
        g = jnp.dot(x, wg_st[slot].astype(BF16), preferred_element_type=F32)
        u = jnp.dot(x, wu_st[slot].astype(BF16), preferred_element_type=F32)
        a = (_silu(g) * u).astype(BF16)
        o_ref[...] = _pack_bf16_pairs(jnp.dot(a, wd_st[slot].astype(BF16), preferred_element_type=F32))

    for static_slot in range(2):
        @pl.when((i < nact) & (wslot_ref[i] == static_slot))
        def _(static_slot=static_slot):
            compute(static_slot)

    @pl.when(i >= nact)
    def _():
        @pl.when(i < nact + nbuf - 1)
        def _():
            wait_gather(i)

        o_ref[...] = jnp.zeros(o_ref.shape, o_ref.dtype)


def _moe_experts(h2, tok_src, blk_src, blk_expert, blk_first, blk_next, blk_wslot, n_active, w_gate, w_up, w_down,
                 layer):
    n, dh = h2.shape
    d = 2 * dh
    f = w_gate.shape[3]
    n_steps = blk_expert.shape[0]
    n_blocks = n_steps - (MOE_RING - 2)
    cap = n_blocks * MOE_BLK
    any_spec = pl.BlockSpec(memory_space=pl.ANY)
    grid_spec = pltpu.PrefetchScalarGridSpec(
        num_scalar_prefetch=7,
        grid=(n_steps,),
        in_specs=[any_spec, any_spec, any_spec, any_spec],
        out_specs=pl.BlockSpec((MOE_BLK, dh), lambda i, *_: (jnp.minimum(i, n_blocks - 1), 0)),
        scratch_shapes=[
            pltpu.VMEM((MOE_RING, MOE_BLK, dh), jnp.uint32),
            pltpu.VMEM((MOE_BLK, d), BF16),
            pltpu.VMEM((2, d, f), F32), pltpu.VMEM((2, d, f), F32), pltpu.VMEM((2, f, d), F32),
            pltpu.SemaphoreType.DMA((MOE_RING,)),
            pltpu.SemaphoreType.DMA((2, 3)),
        ],
    )
    return pl.pallas_call(
        functools.partial(_moe_kernel, layer=layer),
        grid_spec=grid_spec,
        out_shape=jax.ShapeDtypeStruct((cap, dh), jnp.uint32),
        name="moe_experts",
        compiler_params=_cparams(("arbitrary",)),
    )(tok_src, blk_src, blk_expert, blk_first, blk_next, blk_wslot, n_active, h2, w_gate, w_up, w_down)


def _combine_kernel(pos_ref, yb_hbm, x_ref, w_ref, mod_ref, nw_ref, nmod_ref, ws_ref, bs_ref, *out_and_scratch,
                    last, side_cols):
    if last:
        o_ref, buf, sem = out_and_scratch
    else:
        o_ref, h_ref, s_ref, buf, sem = out_and_scratch
    i = pl.program_id(0)
    nsteps = pl.num_programs(0)
    t = x_ref.shape[0]

    def start(j, slot):
        base = j * (t * TOP_K)
        dst = buf.at[slot]
        for r in range(t):
            for k in range(TOP_K):
                p = pos_ref[base + r * TOP_K + k]
                pltpu.make_async_copy(yb_hbm.at[pl.ds(p, 1)], dst.at[k, pl.ds(r, 1)], sem.at[slot]).start()

    def wait(slot):
        for k in range(TOP_K):
            pltpu.make_async_copy(yb_hbm.at[pl.ds(0, t)], buf.at[slot, k], sem.at[slot]).wait()

    @pl.when(i == 0)
    def _():
        start(0, 0)

    slot = i % 2

    @pl.when(i + 1 < nsteps)
    def _():
        start(i + 1, 1 - slot)

    wait(slot)
    w = w_ref[...]
    y = w[:, 0:1] * _unpack_bf16_pairs(buf[slot, 0]) + w[:, 1:2] * _unpack_bf16_pairs(buf[slot, 1])
    x = x_ref[...] + mod_ref[5:6, :] * y
    if last:
        o_ref[...] = x * lax.rsqrt(jnp.mean(x * x, axis=-1, keepdims=True) + EPS) * nw_ref[...]
    else:
        o_ref[...] = x
        h = _rms_mod(x, nw_ref[...], nmod_ref[1:2, :], nmod_ref[0:1, :])
        h_ref[...] = h.astype(h_ref.dtype)
        s_ref[...] = _side_proj(h, ws_ref, bs_ref, side_cols, False)


def _combine(yb, pos, weights, x2, mod_l, seq, next_norm_w, next_mod, w_side, side_spec, b_side, side_cols, last,
             tm=256):
    n, d = x2.shape
    blocks_per_batch = seq // tm
    row_spec = pl.BlockSpec((tm, d), lambda i, *_: (i, 0))
    mod_spec = pl.BlockSpec((None, 6, d), lambda i, *_: (i // blocks_per_batch, 0, 0))
    if last:
        out_specs = row_spec
        out_shape = jax.ShapeDtypeStruct((n, d), F32)
    else:
        out_specs = [row_spec, row_spec, pl.BlockSpec((tm, LANES), lambda i, *_: (i, 0))]
        out_shape = [jax.ShapeDtypeStruct((n, d), F32), jax.ShapeDtypeStruct((n, d), BF16),
                     jax.ShapeDtypeStruct((n, LANES), F32)]
    grid_spec = pltpu.PrefetchScalarGridSpec(
        num_scalar_prefetch=1,
        grid=(n // tm,),
        in_specs=[
            pl.BlockSpec(memory_space=pl.ANY),
            row_spec,
            pl.BlockSpec((tm, TOP_K), lambda i, *_: (i, 0)),
            mod_spec,
            pl.BlockSpec((1, d), lambda i, *_: (0, 0)),
            mod_spec,
            side_spec,
            pl.BlockSpec((1, LANES), lambda i, *_: (0, 0)),
        ],
        out_specs=out_specs,
        scratch_shapes=[pltpu.VMEM((2, TOP_K, tm, yb.shape[1]), yb.dtype), pltpu.SemaphoreType.DMA((2,))],
    )
    return pl.pallas_call(
        functools.partial(_combine_kernel, last=last, side_cols=side_cols),
        grid_spec=grid_spec,
        out_shape=out_shape,
        name="moe_combine",
        compiler_params=_cparams(("arbitrary",)),
    )(pos.reshape(-1), yb, x2, weights, mod_l, next_norm_w.reshape(1, d), next_mod, w_side, b_side)


def _route_kernel(lg_ref, oi_ref, ow_ref, cnt_ref, carry):
    @pl.when(pl.program_id(0) == 0)
    def _():
        carry[...] = jnp.zeros(carry.shape, F32)

    lg = lg_ref[...]
    t = lg.shape[0]
    lane = lax.broadcasted_iota(jnp.int32, lg.shape, 1)
    big = jnp.int32(1 << 30)

    def first_max(vals):
        top = jnp.max(vals, axis=-1, keepdims=True)
        return top, jnp.min(jnp.where(vals == top, lane, big), axis=-1, keepdims=True)

    coarse = lane < N_GROUPS
    gmax, grp = first_max(jnp.where(coarse, lg, NEG_INF))
    p_grp = 1.0 / jnp.sum(jnp.where(coarse, jnp.exp(lg - gmax), 0.0), axis=-1, keepdims=True)
    lo = N_GROUPS + grp * EXPERTS_PER_GROUP
    fine = jnp.where((lane >= lo) & (lane < lo + EXPERTS_PER_GROUP), lg, NEG_INF)
    v1, i1 = first_max(fine)
    v2, i2 = first_max(jnp.where(lane == i1, NEG_INF, fine))
    r = jnp.exp(v2 - v1)
    w1 = p_grp / (1.0 + r)
    w2 = w1 * r
    e1 = i1 - N_GROUPS
    e2 = i2 - N_GROUPS

    hit1 = lane == e1
    hit2 = lane == e2
    picks = (hit1 | hit2).astype(BF16)
    row = lax.broadcasted_iota(jnp.int32, (t, t), 0)
    col = lax.broadcasted_iota(jnp.int32, (t, t), 1)
    before = (col < row).astype(BF16)
    seen = carry[...] + jnp.dot(before, picks, preferred_element_type=F32)
    rank1 = jnp.sum(jnp.where(hit1, seen, 0.0), axis=-1, keepdims=True).astype(jnp.int32)
    rank2 = jnp.sum(jnp.where(hit2, seen, 0.0), axis=-1, keepdims=True).astype(jnp.int32)
    carry[...] = carry[...] + jnp.sum(picks.astype(F32), axis=0, keepdims=True)

    oi_ref[...] = jnp.where(lane == 0, e1, jnp.where(lane == 1, e2, jnp.where(lane == 2, rank1, rank2)))
    ow_ref[...] = jnp.where(lane == 0, w1, w2)
    cnt_ref[...] = carry[...]


def _route(logits, tm=512):
    n = logits.shape[0]
    tm = min(tm, n)
    oi, ow, cnt = pl.pallas_call(
        _route_kernel,
        grid=(n // tm,),
        in_specs=[pl.BlockSpec((tm, LANES), lambda i: (i, 0))],
        out_specs=[pl.BlockSpec((tm, LANES), lambda i: (i, 0)), pl.BlockSpec((tm, LANES), lambda i: (i, 0)),
                   pl.BlockSpec((1, LANES), lambda i: (0, 0))],
        out_shape=[jax.ShapeDtypeStruct((n, LANES), jnp.int32), jax.ShapeDtypeStruct((n, LANES), F32),
                   jax.ShapeDtypeStruct((1, LANES), F32)],
        scratch_shapes=[pltpu.VMEM((1, LANES), F32)],
        name="route",
        compiler_params=_cparams(("arbitrary",)),
    )(logits)
    expert = oi[:, 0:TOP_K]
    rank = oi[:, TOP_K:2 * TOP_K]
    weights = ow[:, 0:TOP_K]
    counts = cnt[0, :N_EXPERTS].astype(jnp.int32)
    return expert, rank, weights, counts


def _dispatch(expert, rank, counts):
    n_tok = expert.shape[0]
    n_assign = n_tok * TOP_K
    cap = n_assign + N_EXPERTS * MOE_BLK
    n_blocks = cap // MOE_BLK + MOE_RING - 2
    e_flat = expert.reshape(-1)
    padded = ((counts + MOE_BLK - 1) // MOE_BLK) * MOE_BLK
    pad_ends = jnp.cumsum(padded)
    pad_starts = pad_ends - padded
    dest = (pad_starts[e_flat] + rank.reshape(-1)).astype(jnp.int32)
    order = jnp.argsort(e_flat, stable=True)
    tok_src = jnp.concatenate([(order // TOP_K).astype(jnp.int32), jnp.zeros((MOE_BLK,), jnp.int32)])
    starts = jnp.cumsum(counts) - counts
    blk_start = jnp.arange(n_blocks, dtype=jnp.int32) * MOE_BLK
    blk_expert = jnp.minimum(jnp.sum((pad_ends[None, :] <= blk_start[:, None]).astype(jnp.int32), axis=1),
                             N_EXPERTS - 1).astype(jnp.int32)
    blk_src = jnp.clip(blk_start - (pad_starts - starts)[blk_expert], 0, n_assign).astype(jnp.int32)
    n_active = (pad_ends[-1] // MOE_BLK).astype(jnp.int32)
    prev = jnp.concatenate([jnp.full((1,), -1, jnp.int32), blk_expert[:-1]])
    blk_first = (blk_expert != prev).astype(jnp.int32)
    run_end = pad_ends[blk_expert] // MOE_BLK
    blk_next = jnp.where(run_end < n_active, blk_expert[jnp.minimum(run_end, n_blocks - 1)], -1).astype(jnp.int32)
    blk_wslot = ((jnp.cumsum((counts > 0).astype(jnp.int32)) - 1)[blk_expert] % 2).astype(jnp.int32)
    return (tok_src, blk_src, blk_expert, blk_first, blk_next, blk_wslot, n_active.reshape(1),
            dest.reshape(n_tok, TOP_K))


def kernel(x, c, ada_w, ada_b, norm1_w, norm2_w, w_in, conv_q, conv_k, igate_b, fgate_b, rel_bias,
           mlstm_norm_w, w_branch_attn, w_branch_mlstm, w_out, router_coarse_w, router_coarse_b,
           router_fine_w, router_fine_b, w_gate, w_up, w_down, final_norm_w):
    b, s, d = x.shape
    depth = ada_w.shape[0]
    n = b * s
    nc = s // CHUNK
    a_width = A_HEADS * A_HEAD_DIM
    m_width = M_HEADS * M_HEAD_DIM
    main_cols = 3 * a_width + 4 * m_width + 2 * d
    ga_col0 = 3 * a_width + 4 * m_width
    gm_col0 = ga_col0 + d

    mod = _ada_mod(c, ada_w, ada_b)
    x2 = x.reshape(n, d)

    w_in_t = jnp.swapaxes(w_in, 1, 2)
    w_ba16 = w_branch_attn.astype(BF16)
    w_bm16 = w_branch_mlstm.astype(BF16)
    w_out16 = w_out.astype(BF16)

    def gate_spec(l):
        return pl.BlockSpec((None, LANES, d), lambda i, *_: (l, main_cols // LANES, 0))

    def gate_bias(l):
        return jnp.zeros((1, LANES), F32).at[0, :M_HEADS].set(igate_b[l]).at[0, M_HEADS:2 * M_HEADS].set(fgate_b[l])

    h, gates = _norm_mod(x2, norm1_w[0], mod[0], w_in_t, gate_spec(0), 2 * M_HEADS, gate_bias(0), seq=s,
                         shift_row=0, scale_row=1, precise=False, out_dtype=BF16)
    out = None
    for l in range(depth):
        w_r = (jnp.zeros((LANES, d), F32).at[:N_GROUPS].set(router_coarse_w[l].T)
               .at[N_GROUPS:N_GROUPS + N_EXPERTS].set(router_fine_w[l].T))
        b_r = (jnp.zeros((1, LANES), F32).at[0, :N_GROUPS].set(router_coarse_b[l])
               .at[0, N_GROUPS:N_GROUPS + N_EXPERTS].set(router_fine_b[l]))

        p_all = _proj(h, w_in_t, l, main_cols)
        y_attn = _attention(p_all, _attn_bias(rel_bias[l]), b, s)
        gates_t = gates[:, :2 * M_HEADS].reshape(b, nc, CHUNK, 2 * M_HEADS).transpose(0, 1, 3, 2)
        h_m = _mlstm(p_all, gates_t, conv_q, conv_k, mlstm_norm_w[l], l, b, s)
        x2, h2, logits = _mixer_tail(y_attn, h_m, p_all, w_ba16, w_bm16, w_out16, l, x2, mod[l], norm2_w[l],
                                     w_r, b_r, N_GROUPS + N_EXPERTS, s, ga_col0, gm_col0)
        expert, rank, weights, counts = _route(logits)
        tok_src, blk_src, blk_expert, blk_first, blk_next, blk_wslot, n_active, pos = _dispatch(expert, rank, counts)
        yb = _moe_experts(h2, tok_src, blk_src, blk_expert, blk_first, blk_next, blk_wslot, n_active,
                          w_gate, w_up, w_down, l)
        if l + 1 < depth:
            x2, h, gates = _combine(yb, pos, weights, x2, mod[l], s, norm1_w[l + 1], mod[l + 1], w_in_t,
                                    gate_spec(l + 1), gate_bias(l + 1), 2 * M_HEADS, last=False)
        else:
            out = _combine(yb, pos, weights, x2, mod[l], s, final_norm_w, mod[l], w_in_t, gate_spec(l),
                           gate_bias(l), 2 * M_HEADS, last=True)

    return out.reshape(b, s, d)
```

```python
import functools

import jax
import jax.numpy as jnp
from jax import lax
from jax.experimental import pallas as pl
from jax.experimental.pallas import tpu as pltpu

F32 = jnp.float32
BF16 = jnp.bfloat16

EPS = 1e-6
NEG_INF = -1e30
LOG2E = 1.4426950408889634
CHUNK = 64
LEFT_CHUNKS = 8
REL_CLIP = 256
A_HEADS = 8
A_HEAD_DIM = 128
M_HEADS = 4
M_HEAD_DIM = 256
CONV_W = 4
N_GROUPS = 4
EXPERTS_PER_GROUP = 8
N_EXPERTS = N_GROUPS * EXPERTS_PER_GROUP
TOP_K = 2

LANES = 128
VMEM_LIMIT = 60 * 1024 * 1024

ATT_QBLK = 256
ATT_KBLKS = 3
MLSTM_BATCH = 4
MOE_RING = 3
MOE_BLK = 256


def _cparams(sem):
    return pltpu.CompilerParams(dimension_semantics=sem, vmem_limit_bytes=VMEM_LIMIT)


def _sigmoid(t):
    return 1.0 / (1.0 + jnp.exp(-t))


def _silu(t):
    return t * _sigmoid(t)


def _pack_bf16_pairs(x):
    c = x.shape[1] // 2
    lo = lax.bitcast_convert_type(x[:, :c].astype(BF16).astype(F32), jnp.uint32)
    hi = lax.bitcast_convert_type(x[:, c:].astype(BF16).astype(F32), jnp.uint32)
    return (lo >> 16) | hi


def _unpack_bf16_pairs(w):
    lo = lax.bitcast_convert_type(w << 16, F32)
    hi = lax.bitcast_convert_type(w & jnp.uint32(0xFFFF0000), F32)
    return jnp.concatenate([lo, hi], axis=1)


def _ada_kernel(c_ref, w_ref, b_ref, o_ref):
    w = w_ref[...].astype(BF16)
    r = jnp.dot(c_ref[...], w, preferred_element_type=F32)
    bp = o_ref.shape[0]
    o_ref[...] = r[:bp] + r[bp:] + b_ref[...]


def _ada_mod(c, ada_w, ada_b):
    depth, d, n6 = ada_w.shape
    b = c.shape[0]
    bp = 8
    c_pad = jnp.zeros((bp, d), F32).at[:b].set(c)
    c_hi = c_pad.astype(BF16)
    c_lo = (c_pad - c_hi.astype(F32)).astype(BF16)
    c2 = jnp.concatenate([c_hi, c_lo], axis=0)
    tn = 1024
    out = pl.pallas_call(
        _ada_kernel,
        grid=(depth, n6 // tn),
        in_specs=[
            pl.BlockSpec((2 * bp, d), lambda l, j: (0, 0)),
            pl.BlockSpec((None, d, tn), lambda l, j: (l, 0, j)),
            pl.BlockSpec((None, 1, tn), lambda l, j: (l, 0, j)),
        ],
        out_specs=pl.BlockSpec((None, bp, tn), lambda l, j: (l, 0, j)),
        out_shape=jax.ShapeDtypeStruct((depth, bp, n6), F32),
        name="ada_mod",
        compiler_params=_cparams(("arbitrary", "arbitrary")),
    )(c2, ada_w, ada_b.reshape(depth, 1, n6))
    return out[:, :b].reshape(depth, b, 6, d)


def _rms_mod(x, nw, scale, shift):
    y = x * lax.rsqrt(jnp.mean(x * x, axis=-1, keepdims=True) + EPS)
    return (y * nw) * (1.0 + scale) + shift


def _side_proj(h, ws_ref, bs_ref, side_cols, precise):
    wrow = lax.broadcasted_iota(jnp.int32, ws_ref.shape, 0)
    ws = jnp.where(wrow < side_cols, ws_ref[...], 0.0)
    nt = (((1,), (1,)), ((), ()))
    if precise:
        h_hi = h.astype(BF16)
        h_lo = (h - h_hi.astype(F32)).astype(BF16)
        w_hi = ws.astype(BF16)
        w_lo = (ws - w_hi.astype(F32)).astype(BF16)
        s = (lax.dot_general(h_hi, w_hi, nt, preferred_element_type=F32)
             + lax.dot_general(h_hi, w_lo, nt, preferred_element_type=F32)
             + lax.dot_general(h_lo, w_hi, nt, preferred_element_type=F32))
    else:
        s = lax.dot_general(h.astype(BF16), ws.astype(BF16), nt, preferred_element_type=F32)
    return s + bs_ref[...]


def _norm_kernel(x_ref, nw_ref, mod_ref, ws_ref, bs_ref, h_ref, s_ref, *, shift_row, scale_row, precise,
                 side_cols):
    h = _rms_mod(x_ref[...], nw_ref[...], mod_ref[scale_row:scale_row + 1, :], mod_ref[shift_row:shift_row + 1, :])
    h_ref[...] = _pack_bf16_pairs(h) if h_ref.dtype == jnp.uint32 else h.astype(h_ref.dtype)
    s_ref[...] = _side_proj(h, ws_ref, bs_ref, side_cols, precise)


def _norm_mod(x2, norm_w, mod_l, w_side, side_spec, side_cols, b_side, *, seq, shift_row, scale_row, precise,
              out_dtype, tm=512):
    n, d = x2.shape
    tm = min(tm, seq)
    blocks_per_batch = seq // tm
    dh = d // 2 if out_dtype == jnp.uint32 else d
    kern = functools.partial(_norm_kernel, shift_row=shift_row, scale_row=scale_row, precise=precise,
                             side_cols=side_cols)
    return pl.pallas_call(
        kern,
        grid=(n // tm,),
        in_specs=[
            pl.BlockSpec((tm, d), lambda i: (i, 0)),
            pl.BlockSpec((1, d), lambda i: (0, 0)),
            pl.BlockSpec((None, 6, d), lambda i: (i // blocks_per_batch, 0, 0)),
            side_spec,
            pl.BlockSpec((1, LANES), lambda i: (0, 0)),
        ],
        out_specs=[
            pl.BlockSpec((tm, dh), lambda i: (i, 0)),
            pl.BlockSpec((tm, LANES), lambda i: (i, 0)),
        ],
        out_shape=[jax.ShapeDtypeStruct((n, dh), out_dtype), jax.ShapeDtypeStruct((n, LANES), F32)],
        name="norm_mod",
        compiler_params=_cparams(("arbitrary",)),
    )(x2, norm_w.reshape(1, d), mod_l, w_side, b_side)


def _proj_kernel(a_ref, wt_ref, o_ref, wb_ref):
    @pl.when(pl.program_id(1) == 0)
    def _():
        wb_ref[...] = wt_ref[...].astype(BF16)

    o_ref[...] = lax.dot_general(a_ref[...], wb_ref[...], (((1,), (1,)), ((), ())),
                                 preferred_element_type=F32).astype(o_ref.dtype)


def _proj(a, wt_stack, layer, n_cols, tm=2048, tn=1024):
    m, k = a.shape
    tm = min(tm, m)
    return pl.pallas_call(
        _proj_kernel,
        grid=(n_cols // tn, m // tm),
        in_specs=[
            pl.BlockSpec((tm, k), lambda j, i: (i, 0)),
            pl.BlockSpec((None, tn, k), lambda j, i: (layer, j, 0)),
        ],
        out_specs=pl.BlockSpec((tm, tn), lambda j, i: (i, j)),
        out_shape=jax.ShapeDtypeStruct((m, n_cols), BF16),
        scratch_shapes=[pltpu.VMEM((tn, k), BF16)],
        name="in_proj",
        compiler_params=_cparams(("arbitrary", "arbitrary")),
    )(a, wt_stack)


def _attn_kernel(q_ref, k0_ref, k1_ref, k2_ref, v0_ref, v1_ref, v2_ref, bias_ref, o_ref):
    k_refs = (k0_ref, k1_ref, k2_ref)
    v_refs = (v0_ref, v1_ref, v2_ref)
    qb = q_ref.shape[0]
    half = qb // 2
    scale2 = (A_HEAD_DIM ** -0.5) * LOG2E
    heads = [slice(h * A_HEAD_DIM, (h + 1) * A_HEAD_DIM) for h in range(A_HEADS)]
    nt = (((1,), (1,)), ((), ()))
    for part in range(2):
        r0 = part * half
        c0 = part * half
        c1 = c0 + ATT_KBLKS * qb - half
        spans = [(max(c0, j * qb) - j * qb, min(c1, (j + 1) * qb) - j * qb) for j in range(ATT_KBLKS)]
        s = jnp.stack([
            jnp.concatenate([lax.dot_general(q_ref[r0:r0 + half, sl], k_refs[j][lo:hi, sl], nt,
                                             preferred_element_type=F32)
                             for j, (lo, hi) in enumerate(spans)], axis=1)
            for sl in heads])
        s = s * scale2 + bias_ref[:, r0:r0 + half, c0:c1]
        m = jnp.max(s, axis=-1, keepdims=True)
        e = jnp.exp2(s - m)
        denom = jnp.sum(e, axis=-1, keepdims=True)
        p = e.astype(BF16)
        for h, sl in enumerate(heads):
            acc = None
            off = 0
            for j, (lo, hi) in enumerate(spans):
                term = jnp.dot(p[h, :, off:off + hi - lo], v_refs[j][lo:hi, sl], preferred_element_type=F32)
                acc = term if acc is None else acc + term
                off += hi - lo
            o_ref[r0:r0 + half, sl] = (acc / denom[h]).astype(o_ref.dtype)


def _attn_bias(rel_table):
    qb, kw = ATT_QBLK, ATT_KBLKS * ATT_QBLK
    nh = rel_table.shape[0]
    qi = jnp.arange(qb)[:, None]
    kj = jnp.arange(kw)[None, :]
    off = kw - 1 - (ATT_KBLKS - 1) * qb
    glen = qb + kw
    n_lo = max(0, min(glen, off - REL_CLIP))
    n_lin = max(0, min(glen, off + REL_CLIP + 1) - n_lo)
    n_hi = glen - n_lo - n_lin
    lin0 = n_lo - off + REL_CLIP
    gr = jnp.concatenate([jnp.broadcast_to(rel_table[:, 2 * REL_CLIP:], (nh, n_hi)),
                          rel_table[:, lin0:lin0 + n_lin][:, ::-1],
                          jnp.broadcast_to(rel_table[:, :1], (nh, n_lo))], axis=1).astype(F32) * LOG2E
    c0 = glen - kw
    bias = jnp.tile(gr, (1, qb + 1))[:, c0:c0 + qb * (glen - 1)].reshape(nh, qb, glen - 1)[:, :, :kw]
    qc = qi // CHUNK + (ATT_KBLKS - 1) * (qb // CHUNK)
    kc = kj // CHUNK
    band = (kc <= qc) & (kc >= qc - LEFT_CHUNKS)
    tables = []
    for t in range(ATT_KBLKS):
        ok = band & (kj >= (ATT_KBLKS - 1 - t) * qb)
        tables.append(jnp.where(ok[None], bias, NEG_INF))
    return jnp.stack(tables)


def _attention(p_all, bias, batch, seq):
    n = p_all.shape[0]
    width = A_HEADS * A_HEAD_DIM
    qb = ATT_QBLK
    nb = seq // qb

    def kv_spec(back, colblk):
        return pl.BlockSpec((qb, width), lambda i, b: (b * nb + jnp.maximum(i - back, 0), colblk))

    return pl.pallas_call(
        _attn_kernel,
        grid=(nb, batch),
        in_specs=[
            pl.BlockSpec((qb, width), lambda i, b: (b * nb + i, 0)),
            kv_spec(2, 1), kv_spec(1, 1), kv_spec(0, 1),
            kv_spec(2, 2), kv_spec(1, 2), kv_spec(0, 2),
            pl.BlockSpec((None, A_HEADS, qb, ATT_KBLKS * qb), lambda i, b: (jnp.minimum(i, ATT_KBLKS - 1), 0, 0, 0)),
        ],
        out_specs=pl.BlockSpec((qb, width), lambda i, b: (b * nb + i, 0)),
        out_shape=jax.ShapeDtypeStruct((n, width), BF16),
        name="chunk_attn",
        compiler_params=_cparams(("arbitrary", "arbitrary")),
    )(p_all, p_all, p_all, p_all, p_all, p_all, p_all, bias)


def _log_sigmoid(t):
    return jnp.minimum(t, 0.0) - jnp.log(1.0 + jnp.exp(-jnp.abs(t)))


def _mlstm_kernel(q_ref, k_ref, v_ref, o_ref, g_ref, cq_ref, ck_ref, nw_ref, out_ref,
                  qbuf, kbuf, ct_ref, n_ref, m_ref):
    c = pl.program_id(1)
    L, D = CHUNK, M_HEAD_DIM
    nb = q_ref.shape[0]
    ns = nb * M_HEADS
    tail = 16

    @pl.when(c == 0)
    def _():
        qbuf[:, 0:tail, :] = jnp.zeros((nb, tail, qbuf.shape[2]), BF16)
        kbuf[:, 0:tail, :] = jnp.zeros((nb, tail, kbuf.shape[2]), BF16)
        ct_ref[...] = jnp.zeros(ct_ref.shape, F32)
        n_ref[...] = jnp.zeros(n_ref.shape, F32)
        m_ref[...] = jnp.zeros(m_ref.shape, F32)

    row = lax.broadcasted_iota(jnp.int32, (L, L), 0)
    colm = lax.broadcasted_iota(jnp.int32, (L, L), 1)
    causal = colm <= row
    eye = colm == row
    upper = (row <= colm).astype(F32)

    def to_col(r):
        return jnp.sum(jnp.where(eye, jnp.broadcast_to(r, (ns, L, L)), 0.0), axis=-1, keepdims=True)

    srow = lax.broadcasted_iota(jnp.int32, ((CONV_W - 1) * L, tail + L), 0)
    scol = lax.broadcasted_iota(jnp.int32, ((CONV_W - 1) * L, tail + L), 1)
    stap = srow // L
    shifts = (scol == srow - stap * L + stap + (tail - (CONV_W - 1))).astype(BF16)

    def conv(buf, x_ref, w_ref):
        shifted = []
        for bi in range(nb):
            buf[bi, tail:tail + L, :] = x_ref[bi]
            shifted.append(jnp.dot(shifts, buf[bi], preferred_element_type=F32))
            buf[bi, 0:tail, :] = buf[bi, L:L + tail, :]
        shifted = jnp.stack(shifted)
        acc = x_ref[...].astype(F32) * w_ref[CONV_W - 1:CONV_W, :]
        for j in range(CONV_W - 1):
            acc = acc + shifted[:, j * L:(j + 1) * L, :] * w_ref[j:j + 1, :]
        return acc

    def streams(x):
        return jnp.stack([x[bi, :, h * D:(h + 1) * D] for bi in range(nb) for h in range(M_HEADS)])

    q = streams(_silu(conv(qbuf, q_ref, cq_ref)) * (D ** -0.5))
    k = streams(_silu(conv(kbuf, k_ref, ck_ref)))
    qb16 = q.astype(BF16)
    kb16 = k.astype(BF16)
    v16 = [v_ref[bi, :, h * D:(h + 1) * D] for bi in range(nb) for h in range(M_HEADS)]

    g = g_ref[...]
    ig2 = jnp.concatenate([g[bi, 0:M_HEADS, :] for bi in range(nb)], axis=0)
    lf2 = _log_sigmoid(jnp.concatenate([g[bi, M_HEADS:2 * M_HEADS, :] for bi in range(nb)], axis=0))
    bcum2 = jnp.dot(lf2, upper, preferred_element_type=F32, precision=lax.Precision.HIGHEST)
    ig = jnp.stack([ig2[i:i + 1, :] for i in range(ns)])
    bcum = jnp.stack([bcum2[i:i + 1, :] for i in range(ns)])
    bcum_c = to_col(bcum)
    m_prev = m_ref[...].reshape(ns, 1, 1)

    logd = jnp.where(causal, bcum_c - bcum + ig, NEG_INF)
    inter = bcum_c + m_prev
    m_s = jnp.maximum(jnp.max(logd, axis=-1, keepdims=True), inter)
    nt = (((1,), (1,)), ((), ()))
    s = jnp.stack([lax.dot_general(qb16[i], kb16[i], nt, preferred_element_type=F32) for i in range(ns)])
    w_intra = s * jnp.exp(logd - m_s)
    w_inter = jnp.exp(inter - m_s)
    ct = ct_ref[...].reshape(ns, D, D)
    n_row = n_ref[...].reshape(ns, 1, D)
    wi16 = w_intra.astype(BF16)
    ct16 = ct.astype(BF16)
    num_intra = jnp.stack([jnp.dot(wi16[i], v16[i], preferred_element_type=F32) for i in range(ns)])
    num_inter = jnp.stack([jnp.dot(qb16[i], ct16[i], preferred_element_type=F32) for i in range(ns)])
    num = num_intra + w_inter * num_inter
    den = (jnp.sum(w_intra, axis=-1, keepdims=True)
           + w_inter * jnp.sum(q * n_row, axis=-1, keepdims=True))
    hs = num / jnp.maximum(jnp.abs(den), jnp.exp(-m_s))

    b_last = bcum[:, :, L - 1:L]
    log_wk = b_last - bcum + ig
    m_new = jnp.maximum(b_last + m_prev, jnp.max(log_wk, axis=-1, keepdims=True))
    wk = jnp.exp(log_wk - m_new)
    decay = jnp.exp(b_last + m_prev - m_new)
    kw = k * to_col(wk)
    kw16 = kw.astype(BF16)
    tn = (((0,), (0,)), ((), ()))
    upd = jnp.stack([lax.dot_general(kw16[i], v16[i], tn, preferred_element_type=F32) for i in range(ns)])
    ct_ref[...] = (decay * ct + upd).reshape(ct_ref.shape)
    n_ref[...] = (decay * n_row + jnp.sum(kw, axis=1, keepdims=True)).reshape(n_ref.shape)
    m_ref[...] = m_new.reshape(m_ref.shape)

    og = jnp.stack([o_ref[bi, :, h * D:(h + 1) * D] for bi in range(nb) for h in range(M_HEADS)]).astype(F32)
    nw = jnp.stack([nw_ref[:, h * D:(h + 1) * D] for _ in range(nb) for h in range(M_HEADS)])
    hm = _sigmoid(og) * hs
    y = (hm * lax.rsqrt(jnp.mean(hm * hm, axis=-1, keepdims=True) + EPS) * nw).astype(out_ref.dtype)
    for bi in range(nb):
        for h in range(M_HEADS):
            out_ref[bi, :, h * D:(h + 1) * D] = y[bi * M_HEADS + h]


def _mlstm(p_all, gates_t, conv_q, conv_k, norm_w, layer, batch, seq):
    n, cols = p_all.shape
    width = M_HEADS * M_HEAD_DIM
    nc = seq // CHUNK
    L = CHUNK
    nb = MLSTM_BATCH
    p3 = p_all.reshape(batch, seq, cols)

    def p_spec(colblk):
        return pl.BlockSpec((nb, L, width), lambda g, c: (g, c, colblk))

    out = pl.pallas_call(
        _mlstm_kernel,
        grid=(batch // nb, nc),
        in_specs=[
            p_spec(3), p_spec(4), p_spec(5), p_spec(6),
            pl.BlockSpec((nb, None, 2 * M_HEADS, L), lambda g, c: (g, c, 0, 0)),
            pl.BlockSpec((None, CONV_W, width), lambda g, c: (layer, 0, 0)),
            pl.BlockSpec((None, CONV_W, width), lambda g, c: (layer, 0, 0)),
            pl.BlockSpec((1, width), lambda g, c: (0, 0)),
        ],
        out_specs=pl.BlockSpec((nb, L, width), lambda g, c: (g, c, 0)),
        out_shape=jax.ShapeDtypeStruct((batch, seq, width), BF16),
        scratch_shapes=[
            pltpu.VMEM((nb, L + 16, width), BF16),
            pltpu.VMEM((nb, L + 16, width), BF16),
            pltpu.VMEM((nb, M_HEADS, M_HEAD_DIM, M_HEAD_DIM), F32),
            pltpu.VMEM((nb, M_HEADS, 1, M_HEAD_DIM), F32),
            pltpu.VMEM((nb, M_HEADS, 1, 1), F32),
        ],
        name="mlstm",
        compiler_params=_cparams(("arbitrary", "arbitrary")),
    )(p3, p3, p3, p3, gates_t, conv_q, conv_k, norm_w.reshape(1, width))
    return out.reshape(n, width)


def _tail_kernel(ya_ref, hm_ref, ga0_ref, ga1_ref, gm0_ref, gm1_ref, wa_ref, wm_ref, wo_ref, x_ref, mod_ref,
                 nw_ref, ws_ref, bs_ref, xo_ref, h_ref, s_ref, *, side_cols):
    a = jnp.dot(ya_ref[...], wa_ref[...], preferred_element_type=F32)
    m = jnp.dot(hm_ref[...], wm_ref[...], preferred_element_type=F32)
    ga = jnp.concatenate([ga0_ref[...], ga1_ref[...]], axis=1).astype(F32)
    gm = jnp.concatenate([gm0_ref[...], gm1_ref[...]], axis=1).astype(F32)
    merged = (_sigmoid(ga) * a + _sigmoid(gm) * m).astype(BF16)
    x = x_ref[...] + mod_ref[2:3, :] * jnp.dot(merged, wo_ref[...], preferred_element_type=F32)
    xo_ref[...] = x
    h = _rms_mod(x, nw_ref[...], mod_ref[4:5, :], mod_ref[3:4, :])
    h_ref[...] = _pack_bf16_pairs(h)
    s_ref[...] = _side_proj(h, ws_ref, bs_ref, side_cols, True)


def _mixer_tail(y_attn, h_m, p_all, w_ba16, w_bm16, w_out16, layer, x2, mod_l, norm_w, w_side, b_side, side_cols,
                seq, ga_col0, gm_col0, tm=256):
    n, ka = y_attn.shape
    km = h_m.shape[1]
    d = x2.shape[1]
    half = d // 2
    blocks_per_batch = seq // tm

    def gate_spec(col0, part):
        return pl.BlockSpec((tm, half), lambda i: (i, col0 // half + part))

    return pl.pallas_call(
        functools.partial(_tail_kernel, side_cols=side_cols),
        grid=(n // tm,),
        in_specs=[
            pl.BlockSpec((tm, ka), lambda i: (i, 0)),
            pl.BlockSpec((tm, km), lambda i: (i, 0)),
            gate_spec(ga_col0, 0), gate_spec(ga_col0, 1), gate_spec(gm_col0, 0), gate_spec(gm_col0, 1),
            pl.BlockSpec((None, ka, d), lambda i: (layer, 0, 0)),
            pl.BlockSpec((None, km, d), lambda i: (layer, 0, 0)),
            pl.BlockSpec((None, d, d), lambda i: (layer, 0, 0)),
            pl.BlockSpec((tm, d), lambda i: (i, 0)),
            pl.BlockSpec((None, 6, d), lambda i: (i // blocks_per_batch, 0, 0)),
            pl.BlockSpec((1, d), lambda i: (0, 0)),
            pl.BlockSpec((LANES, d), lambda i: (0, 0)),
            pl.BlockSpec((1, LANES), lambda i: (0, 0)),
        ],
        out_specs=[
            pl.BlockSpec((tm, d), lambda i: (i, 0)),
            pl.BlockSpec((tm, half), lambda i: (i, 0)),
            pl.BlockSpec((tm, LANES), lambda i: (i, 0)),
        ],
        out_shape=[jax.ShapeDtypeStruct((n, d), F32), jax.ShapeDtypeStruct((n, half), jnp.uint32),
                   jax.ShapeDtypeStruct((n, LANES), F32)],
        name="mixer_tail",
        compiler_params=_cparams(("arbitrary",)),
    )(y_attn, h_m, p_all, p_all, p_all, p_all, w_ba16, w_bm16, w_out16, x2, mod_l, norm_w.reshape(1, d),
      w_side, b_side)


def _moe_kernel(tok_ref, src_ref, be_ref, first_ref, nxt_ref, wslot_ref, nact_ref, h_hbm, wg_hbm, wu_hbm, wd_hbm,
                o_ref, xbuf, xb16, wg_st, wu_st, wd_st, xsem, wsem, *, layer):
    i = pl.program_id(0)
    nact = nact_ref[0]
    nbuf, blk = xbuf.shape[0], xbuf.shape[1]
    stages = ((wg_hbm, wg_st), (wu_hbm, wu_st), (wd_hbm, wd_st))

    def weight_copy(k, e, slot):
        return pltpu.make_async_copy(stages[k][0].at[layer, e], stages[k][1].at[slot], wsem.at[slot, k])

    weight_queue = 1

    def start_gather(j):
        base = src_ref[j]
        ring = j % nbuf
        dst = xbuf.at[ring]
        for r in range(blk):
            tok = tok_ref[base + r]
            pltpu.make_async_copy(h_hbm.at[pl.ds(tok, 1)], dst.at[pl.ds(r, 1)], xsem.at[ring]).start()

    def wait_gather(j):
        ring = j % nbuf
        pltpu.make_async_copy(h_hbm.at[pl.ds(0, blk)], xbuf.at[ring], xsem.at[ring]).wait()

    @pl.when(i == 0)
    def _():
        for k in range(3):
            weight_copy(k, be_ref[0], wslot_ref[0]).start(priority=weight_queue)
        for j in range(nbuf - 1):
            start_gather(j)

    @pl.when((i < nact) & (first_ref[i] == 1))
    def _():
        slot = wslot_ref[i]
        e_next = nxt_ref[i]

        @pl.when(e_next >= 0)
        def _():
            for k in range(3):
                weight_copy(k, e_next, 1 - slot).start(priority=weight_queue)

        for k in range(3):
            weight_copy(k, be_ref[i], slot).wait()

    def compute(slot):
        wait_gather(i)
        xb16[...] = _unpack_bf16_pairs(xbuf[i % nbuf]).astype(BF16)
        start_gather(i + nbuf - 1)
        x = xb16[...]
        g = jnp.dot(x, wg_st[slot].astype(BF16), preferred_element_type=F32)
        u = jnp.dot(x, wu_st[slot].astype(BF16), preferred_element_type=F32)
        a = (_silu(g) * u).astype(BF16)
        o_ref[...] = _pack_bf16_pairs(jnp.dot(a, wd_st[slot].astype(BF16), preferred_element_type=F32))

    for static_slot in range(2):
        @pl.when((i < nact) & (wslot_ref[i] == static_slot))
        def _(static_slot=static_slot):
            compute(static_slot)

    @pl.when(i >= nact)
    def _():
        @pl.when(i < nact + nbuf - 1)
        def _():
            wait_gather(i)

        o_ref[...] = jnp.zeros(o_ref.shape, o_ref.dtype)


def _moe_experts(h2, tok_src, blk_src, blk_expert, blk_first, blk_next, blk_wslot, n_active, w_gate, w_up, w_down,
                 layer):
    n, dh = h2.shape
    d = 2 * dh
    f = w_gate.shape[3]
    n_steps = blk_expert.shape[0]
    n_blocks = n_steps - (MOE_RING - 2)
    cap = n_blocks * MOE_BLK
    any_spec = pl.BlockSpec(memory_space=pl.ANY)
    grid_spec = pltpu.PrefetchScalarGridSpec(
        num_scalar_prefetch=7,
        grid=(n_steps,),
        in_specs=[any_spec, any_spec, any_spec, any_spec],
        out_specs=pl.BlockSpec((MOE_BLK, dh), lambda i, *_: (jnp.minimum(i, n_blocks - 1), 0)),
        scratch_shapes=[
            pltpu.VMEM((MOE_RING, MOE_BLK, dh), jnp.uint32),
            pltpu.VMEM((MOE_BLK, d), BF16),
            pltpu.VMEM((2, d, f), F32), pltpu.VMEM((2, d, f), F32), pltpu.VMEM((2, f, d), F32),
            pltpu.SemaphoreType.DMA((MOE_RING,)),
            pltpu.SemaphoreType.DMA((2, 3)),
        ],
    )
    return pl.pallas_call(
        functools.partial(_moe_kernel, layer=layer),
        grid_spec=grid_spec,
        out_shape=jax.ShapeDtypeStruct((cap, dh), jnp.uint32),
        name="moe_experts",
        compiler_params=_cparams(("arbitrary",)),
    )(tok_src, blk_src, blk_expert, blk_first, blk_next, blk_wslot, n_active, h2, w_gate, w_up, w_down)


def _combine_kernel(pos_ref, yb_hbm, x_ref, w_ref, mod_ref, nw_ref, nmod_ref, ws_ref, bs_ref, *out_and_scratch,
                    last, side_cols):
    if last:
        o_ref, buf0, buf1, sem = out_and_scratch
    else:
        o_ref, h_ref, s_ref, buf0, buf1, sem = out_and_scratch
    bufs = (buf0, buf1)
    i = pl.program_id(0)
    nsteps = pl.num_programs(0)
    t = x_ref.shape[0]

    def start(j, slot):
        base = j * (t * TOP_K)
        dst = bufs[slot]
        for r in range(t):
            for k in range(TOP_K):
                p = pos_ref[base + r * TOP_K + k]
                pltpu.make_async_copy(yb_hbm.at[pl.ds(p, 1)], dst.at[k, pl.ds(r, 1)], sem.at[slot]).start()

    def wait(slot):
        for k in range(TOP_K):
            pltpu.make_async_copy(yb_hbm.at[pl.ds(0, t)], bufs[slot].at[k], sem.at[slot]).wait()

    @pl.when(i == 0)
    def _():
        start(0, 0)

    def step(slot):
        wait(slot)
        start(jnp.where(i + 1 < nsteps, i + 1, 0), 1 - slot)
        w = w_ref[...]
        y = w[:, 0:1] * _unpack_bf16_pairs(bufs[slot][0]) + w[:, 1:2] * _unpack_bf16_pairs(bufs[slot][1])
        x = x_ref[...] + mod_ref[5:6, :] * y
        if last:
            o_ref[...] = x * lax.rsqrt(jnp.mean(x * x, axis=-1, keepdims=True) + EPS) * nw_ref[...]
        else:
            o_ref[...] = x
            h = _rms_mod(x, nw_ref[...], nmod_ref[1:2, :], nmod_ref[0:1, :])
            h_ref[...] = h.astype(h_ref.dtype)
            s_ref[...] = _side_proj(h, ws_ref, bs_ref, side_cols, False)

    for static_slot in range(2):
        @pl.when(i % 2 == static_slot)
        def _(static_slot=static_slot):
            step(static_slot)

    for static_slot in range(2):
        @pl.when((i == nsteps - 1) & (nsteps % 2 == static_slot))
        def _(static_slot=static_slot):
            wait(static_slot)


def _combine(yb, pos, weights, x2, mod_l, seq, next_norm_w, next_mod, w_side, side_spec, b_side, side_cols, last,
             tm=256):
    n, d = x2.shape
    blocks_per_batch = seq // tm
    row_spec = pl.BlockSpec((tm, d), lambda i, *_: (i, 0))
    mod_spec = pl.BlockSpec((None, 6, d), lambda i, *_: (i // blocks_per_batch, 0, 0))
    if last:
        out_specs = row_spec
        out_shape = jax.ShapeDtypeStruct((n, d), F32)
    else:
        out_specs = [row_spec, row_spec, pl.BlockSpec((tm, LANES), lambda i, *_: (i, 0))]
        out_shape = [jax.ShapeDtypeStruct((n, d), F32), jax.ShapeDtypeStruct((n, d), BF16),
                     jax.ShapeDtypeStruct((n, LANES), F32)]
    grid_spec = pltpu.PrefetchScalarGridSpec(
        num_scalar_prefetch=1,
        grid=(n // tm,),
        in_specs=[
            pl.BlockSpec(memory_space=pl.ANY),
            row_spec,
            pl.BlockSpec((tm, TOP_K), lambda i, *_: (i, 0)),
            mod_spec,
            pl.BlockSpec((1, d), lambda i, *_: (0, 0)),
            mod_spec,
            side_spec,
            pl.BlockSpec((1, LANES), lambda i, *_: (0, 0)),
        ],
        out_specs=out_specs,
        scratch_shapes=[pltpu.VMEM((TOP_K, tm, yb.shape[1]), yb.dtype), pltpu.VMEM((TOP_K, tm, yb.shape[1]), yb.dtype),
                        pltpu.SemaphoreType.DMA((2,))],
    )
    return pl.pallas_call(
        functools.partial(_combine_kernel, last=last, side_cols=side_cols),
        grid_spec=grid_spec,
        out_shape=out_shape,
        name="moe_combine",
        compiler_params=_cparams(("arbitrary",)),
    )(pos.reshape(-1), yb, x2, weights, mod_l, next_norm_w.reshape(1, d), next_mod, w_side, b_side)


def _route_kernel(lg_ref, oi_ref, ow_ref, cnt_ref, carry):
    @pl.when(pl.program_id(0) == 0)
    def _():
        carry[...] = jnp.zeros(carry.shape, F32)

    lg = lg_ref[...]
    t = lg.shape[0]
    lane = lax.broadcasted_iota(jnp.int32, lg.shape, 1)
    big = jnp.int32(1 << 30)

    def first_max(vals):
        top = jnp.max(vals, axis=-1, keepdims=True)
        return top, jnp.min(jnp.where(vals == top, lane, big), axis=-1, keepdims=True)

    coarse = lane < N_GROUPS
    gmax, grp = first_max(jnp.where(coarse, lg, NEG_INF))
    p_grp = 1.0 / jnp.sum(jnp.where(coarse, jnp.exp(lg - gmax), 0.0), axis=-1, keepdims=True)
    lo = N_GROUPS + grp * EXPERTS_PER_GROUP
    fine = jnp.where((lane >= lo) & (lane < lo + EXPERTS_PER_GROUP), lg, NEG_INF)
    v1, i1 = first_max(fine)
    v2, i2 = first_max(jnp.where(lane == i1, NEG_INF, fine))
    r = jnp.exp(v2 - v1)
    w1 = p_grp / (1.0 + r)
    w2 = w1 * r
    e1 = i1 - N_GROUPS
    e2 = i2 - N_GROUPS

    hit1 = lane == e1
    hit2 = lane == e2
    picks = (hit1 | hit2).astype(BF16)
    row = lax.broadcasted_iota(jnp.int32, (t, t), 0)
    col = lax.broadcasted_iota(jnp.int32, (t, t), 1)
    before = (col < row).astype(BF16)
    seen = carry[...] + jnp.dot(before, picks, preferred_element_type=F32)
    rank1 = jnp.sum(jnp.where(hit1, seen, 0.0), axis=-1, keepdims=True).astype(jnp.int32)
    rank2 = jnp.sum(jnp.where(hit2, seen, 0.0), axis=-1, keepdims=True).astype(jnp.int32)
    carry[...] = carry[...] + jnp.sum(picks.astype(F32), axis=0, keepdims=True)

    oi_ref[...] = jnp.where(lane == 0, e1, jnp.where(lane == 1, e2, jnp.where(lane == 2, rank1, rank2)))
    ow_ref[...] = jnp.where(lane == 0, w1, w2)
    cnt_ref[...] = carry[...]


def _route(logits, tm=512):
    n = logits.shape[0]
    tm = min(tm, n)
    oi, ow, cnt = pl.pallas_call(
        _route_kernel,
        grid=(n // tm,),
        in_specs=[pl.BlockSpec((tm, LANES), lambda i: (i, 0))],
        out_specs=[pl.BlockSpec((tm, LANES), lambda i: (i, 0)), pl.BlockSpec((tm, LANES), lambda i: (i, 0)),
                   pl.BlockSpec((1, LANES), lambda i: (0, 0))],
        out_shape=[jax.ShapeDtypeStruct((n, LANES), jnp.int32), jax.ShapeDtypeStruct((n, LANES), F32),
                   jax.ShapeDtypeStruct((1, LANES), F32)],
        scratch_shapes=[pltpu.VMEM((1, LANES), F32)],
        name="route",
        compiler_params=_cparams(("arbitrary",)),
    )(logits)
    expert = oi[:, 0:TOP_K]
    rank = oi[:, TOP_K:2 * TOP_K]
    weights = ow[:, 0:TOP_K]
    counts = cnt[0, :N_EXPERTS].astype(jnp.int32)
    return expert, rank, weights, counts


def _dispatch(expert, rank, counts):
    n_tok = expert.shape[0]
    n_assign = n_tok * TOP_K
    cap = n_assign + N_EXPERTS * MOE_BLK
    n_blocks = cap // MOE_BLK + MOE_RING - 2
    e_flat = expert.reshape(-1)
    padded = ((counts + MOE_BLK - 1) // MOE_BLK) * MOE_BLK
    pad_ends = jnp.cumsum(padded)
    pad_starts = pad_ends - padded
    dest = (pad_starts[e_flat] + rank.reshape(-1)).astype(jnp.int32)
    order = jnp.argsort(e_flat, stable=True)
    tok_src = jnp.concatenate([(order // TOP_K).astype(jnp.int32), jnp.zeros((MOE_BLK,), jnp.int32)])
    starts = jnp.cumsum(counts) - counts
    blk_start = jnp.arange(n_blocks, dtype=jnp.int32) * MOE_BLK
    blk_expert = jnp.minimum(jnp.sum((pad_ends[None, :] <= blk_start[:, None]).astype(jnp.int32), axis=1),
                             N_EXPERTS - 1).astype(jnp.int32)
    blk_src = jnp.clip(blk_start - (pad_starts - starts)[blk_expert], 0, n_assign).astype(jnp.int32)
    n_active = (pad_ends[-1] // MOE_BLK).astype(jnp.int32)
    prev = jnp.concatenate([jnp.full((1,), -1, jnp.int32), blk_expert[:-1]])
    blk_first = (blk_expert != prev).astype(jnp.int32)
    run_end = pad_ends[blk_expert] // MOE_BLK
    blk_next = jnp.where(run_end < n_active, blk_expert[jnp.minimum(run_end, n_blocks - 1)], -1).astype(jnp.int32)
    blk_wslot = ((jnp.cumsum((counts > 0).astype(jnp.int32)) - 1)[blk_expert] % 2).astype(jnp.int32)
    return (tok_src, blk_src, blk_expert, blk_first, blk_next, blk_wslot, n_active.reshape(1),
            dest.reshape(n_tok, TOP_K))


def kernel(x, c, ada_w, ada_b, norm1_w, norm2_w, w_in, conv_q, conv_k, igate_b, fgate_b, rel_bias,
           mlstm_norm_w, w_branch_attn, w_branch_mlstm, w_out, router_coarse_w, router_coarse_b,
           router_fine_w, router_fine_b, w_gate, w_up, w_down, final_norm_w):
    b, s, d = x.shape
    depth = ada_w.shape[0]
    n = b * s
    nc = s // CHUNK
    a_width = A_HEADS * A_HEAD_DIM
    m_width = M_HEADS * M_HEAD_DIM
    main_cols = 3 * a_width + 4 * m_width + 2 * d
    ga_col0 = 3 * a_width + 4 * m_width
    gm_col0 = ga_col0 + d

    mod = _ada_mod(c, ada_w, ada_b)
    x2 = x.reshape(n, d)

    w_in_t = jnp.swapaxes(w_in, 1, 2)
    w_ba16 = w_branch_attn.astype(BF16)
    w_bm16 = w_branch_mlstm.astype(BF16)
    w_out16 = w_out.astype(BF16)

    def gate_spec(l):
        return pl.BlockSpec((None, LANES, d), lambda i, *_: (l, main_cols // LANES, 0))

    def gate_bias(l):
        return jnp.zeros((1, LANES), F32).at[0, :M_HEADS].set(igate_b[l]).at[0, M_HEADS:2 * M_HEADS].set(fgate_b[l])

    h, gates = _norm_mod(x2, norm1_w[0], mod[0], w_in_t, gate_spec(0), 2 * M_HEADS, gate_bias(0), seq=s,
                         shift_row=0, scale_row=1, precise=False, out_dtype=BF16)
    out = None
    for l in range(depth):
        w_r = (jnp.zeros((LANES, d), F32).at[:N_GROUPS].set(router_coarse_w[l].T)
               .at[N_GROUPS:N_GROUPS + N_EXPERTS].set(router_fine_w[l].T))
        b_r = (jnp.zeros((1, LANES), F32).at[0, :N_GROUPS].set(router_coarse_b[l])
               .at[0, N_GROUPS:N_GROUPS + N_EXPERTS].set(router_fine_b[l]))

        p_all = _proj(h, w_in_t, l, main_cols)
        y_attn = _attention(p_all, _attn_bias(rel_bias[l]), b, s)
        gates_t = gates[:, :2 * M_HEADS].reshape(b, nc, CHUNK, 2 * M_HEADS).transpose(0, 1, 3, 2)
        h_m = _mlstm(p_all, gates_t, conv_q, conv_k, mlstm_norm_w[l], l, b, s)
        x2, h2, logits = _mixer_tail(y_attn, h_m, p_all, w_ba16, w_bm16, w_out16, l, x2, mod[l], norm2_w[l],
                                     w_r, b_r, N_GROUPS + N_EXPERTS, s, ga_col0, gm_col0)
        expert, rank, weights, counts = _route(logits)
        tok_src, blk_src, blk_expert, blk_first, blk_next, blk_wslot, n_active, pos = _dispatch(expert, rank, counts)
        yb = _moe_experts(h2, tok_src, blk_src, blk_expert, blk_first, blk_next, blk_wslot, n_active,
                          w_gate, w_up, w_down, l)
        if l + 1 < depth:
            x2, h, gates = _combine(yb, pos, weights, x2, mod[l], s, norm1_w[l + 1], mod[l + 1], w_in_t,
                                    gate_spec(l + 1), gate_bias(l + 1), 2 * M_HEADS, last=False)
        else:
            out = _combine(yb, pos, weights, x2, mod[l], s, final_norm_w, mod[l], w_in_t, gate_spec(l),
                           gate_bias(l), 2 * M_HEADS, last=True)

    return out.reshape(b, s, d)
```

```python
import functools

import jax
import jax.numpy as jnp
from jax import lax
from jax.experimental import pallas as pl
from jax.experimental.pallas import tpu as pltpu

F32 = jnp.float32
BF16 = jnp.bfloat16

EPS = 1e-6
NEG_INF = -1e30
LOG2E = 1.4426950408889634
CHUNK = 64
LEFT_CHUNKS = 8
REL_CLIP = 256
A_HEADS = 8
A_HEAD_DIM = 128
M_HEADS = 4
M_HEAD_DIM = 256
CONV_W = 4
N_GROUPS = 4
EXPERTS_PER_GROUP = 8
N_EXPERTS = N_GROUPS * EXPERTS_PER_GROUP
TOP_K = 2

LANES = 128
VMEM_LIMIT = 60 * 1024 * 1024

ATT_QBLK = 256
ATT_HEAD_GROUP = 4
ATT_KBLKS = 3
MLSTM_BATCH = 4
MOE_RING = 3
MOE_BLK = 256


def _cparams(sem):
    return pltpu.CompilerParams(dimension_semantics=sem, vmem_limit_bytes=VMEM_LIMIT)


def _sigmoid(t):
    return 1.0 / (1.0 + jnp.exp(-t))


def _silu(t):
    return t * _sigmoid(t)


def _pack_bf16_pairs(x):
    c = x.shape[1] // 2
    lo = lax.bitcast_convert_type(x[:, :c].astype(BF16).astype(F32), jnp.uint32)
    hi = lax.bitcast_convert_type(x[:, c:].astype(BF16).astype(F32), jnp.uint32)
    return (lo >> 16) | hi


def _unpack_bf16_pairs(w):
    lo = lax.bitcast_convert_type(w << 16, F32)
    hi = lax.bitcast_convert_type(w & jnp.uint32(0xFFFF0000), F32)
    return jnp.concatenate([lo, hi], axis=1)


def _ada_kernel(c_ref, w_ref, b_ref, o_ref):
    w = w_ref[...].astype(BF16)
    r = jnp.dot(c_ref[...], w, preferred_element_type=F32)
    bp = o_ref.shape[0]
    o_ref[...] = r[:bp] + r[bp:] + b_ref[...]


def _ada_mod(c, ada_w, ada_b):
    depth, d, n6 = ada_w.shape
    b = c.shape[0]
    bp = 8
    c_pad = jnp.zeros((bp, d), F32).at[:b].set(c)
    c_hi = c_pad.astype(BF16)
    c_lo = (c_pad - c_hi.astype(F32)).astype(BF16)
    c2 = jnp.concatenate([c_hi, c_lo], axis=0)
    tn = 1024
    out = pl.pallas_call(
        _ada_kernel,
        grid=(depth, n6 // tn),
        in_specs=[
            pl.BlockSpec((2 * bp, d), lambda l, j: (0, 0)),
            pl.BlockSpec((None, d, tn), lambda l, j: (l, 0, j)),
            pl.BlockSpec((None, 1, tn), lambda l, j: (l, 0, j)),
        ],
        out_specs=pl.BlockSpec((None, bp, tn), lambda l, j: (l, 0, j)),
        out_shape=jax.ShapeDtypeStruct((depth, bp, n6), F32),
        name="ada_mod",
        compiler_params=_cparams(("arbitrary", "arbitrary")),
    )(c2, ada_w, ada_b.reshape(depth, 1, n6))
    return out[:, :b].reshape(depth, b, 6, d)


def _rms_mod(x, nw, scale, shift):
    y = x * lax.rsqrt(jnp.mean(x * x, axis=-1, keepdims=True) + EPS)
    return (y * nw) * (1.0 + scale) + shift


def _side_proj(h, ws_ref, bs_ref, side_cols, precise):
    wrow = lax.broadcasted_iota(jnp.int32, ws_ref.shape, 0)
    ws = jnp.where(wrow < side_cols, ws_ref[...], 0.0)
    nt = (((1,), (1,)), ((), ()))
    if precise:
        h_hi = h.astype(BF16)
        h_lo = (h - h_hi.astype(F32)).astype(BF16)
        w_hi = ws.astype(BF16)
        s = (lax.dot_general(h_hi, w_hi, nt, preferred_element_type=F32)
             + lax.dot_general(h_lo, w_hi, nt, preferred_element_type=F32))
    else:
        s = lax.dot_general(h.astype(BF16), ws.astype(BF16), nt, preferred_element_type=F32)
    return s + bs_ref[...]


def _norm_kernel(x_ref, nw_ref, mod_ref, ws_ref, bs_ref, h_ref, s_ref, *, shift_row, scale_row, precise,
                 side_cols):
    h = _rms_mod(x_ref[...], nw_ref[...], mod_ref[scale_row:scale_row + 1, :], mod_ref[shift_row:shift_row + 1, :])
    h_ref[...] = _pack_bf16_pairs(h) if h_ref.dtype == jnp.uint32 else h.astype(h_ref.dtype)
    s_ref[...] = _side_proj(h, ws_ref, bs_ref, side_cols, precise)


def _norm_mod(x2, norm_w, mod_l, w_side, side_spec, side_cols, b_side, *, seq, shift_row, scale_row, precise,
              out_dtype, tm=512):
    n, d = x2.shape
    tm = min(tm, seq)
    blocks_per_batch = seq // tm
    dh = d // 2 if out_dtype == jnp.uint32 else d
    kern = functools.partial(_norm_kernel, shift_row=shift_row, scale_row=scale_row, precise=precise,
                             side_cols=side_cols)
    return pl.pallas_call(
        kern,
        grid=(n // tm,),
        in_specs=[
            pl.BlockSpec((tm, d), lambda i: (i, 0)),
            pl.BlockSpec((1, d), lambda i: (0, 0)),
            pl.BlockSpec((None, 6, d), lambda i: (i // blocks_per_batch, 0, 0)),
            side_spec,
            pl.BlockSpec((1, LANES), lambda i: (0, 0)),
        ],
        out_specs=[
            pl.BlockSpec((tm, dh), lambda i: (i, 0)),
            pl.BlockSpec((tm, LANES), lambda i: (i, 0)),
        ],
        out_shape=[jax.ShapeDtypeStruct((n, dh), out_dtype), jax.ShapeDtypeStruct((n, LANES), F32)],
        name="norm_mod",
        compiler_params=_cparams(("arbitrary",)),
    )(x2, norm_w.reshape(1, d), mod_l, w_side, b_side)


def _proj_kernel(a_ref, wt_ref, o_ref, wb_ref):
    @pl.when(pl.program_id(1) == 0)
    def _():
        wb_ref[...] = wt_ref[...].astype(BF16)

    o_ref[...] = lax.dot_general(a_ref[...], wb_ref[...], (((1,), (1,)), ((), ())),
                                 preferred_element_type=F32).astype(o_ref.dtype)


def _proj(a, wt_stack, layer, n_cols, tm=2048, tn=1024):
    m, k = a.shape
    tm = min(tm, m)
    return pl.pallas_call(
        _proj_kernel,
        grid=(n_cols // tn, m // tm),
        in_specs=[
            pl.BlockSpec((tm, k), lambda j, i: (i, 0)),
            pl.BlockSpec((None, tn, k), lambda j, i: (layer, j, 0)),
        ],
        out_specs=pl.BlockSpec((tm, tn), lambda j, i: (i, j)),
        out_shape=jax.ShapeDtypeStruct((m, n_cols), BF16),
        scratch_shapes=[pltpu.VMEM((tn, k), BF16)],
        name="in_proj",
        compiler_params=_cparams(("arbitrary", "arbitrary")),
    )(a, wt_stack)


def _attn_kernel(q_ref, k0_ref, k1_ref, k2_ref, v0_ref, v1_ref, v2_ref, bias_ref, o_ref):
    k_refs = (k0_ref, k1_ref, k2_ref)
    v_refs = (v0_ref, v1_ref, v2_ref)
    qb = q_ref.shape[0]
    half = qb // 2
    scale2 = (A_HEAD_DIM ** -0.5) * LOG2E
    heads = [slice(h * A_HEAD_DIM, (h + 1) * A_HEAD_DIM) for h in range(A_HEADS)]
    nt = (((1,), (1,)), ((), ()))
    for part in range(2):
        r0 = part * half
        c0 = part * half
        c1 = c0 + ATT_KBLKS * qb - half
        spans = [(max(c0, j * qb) - j * qb, min(c1, (j + 1) * qb) - j * qb) for j in range(ATT_KBLKS)]
        for h0 in range(0, A_HEADS, ATT_HEAD_GROUP):
            grp = heads[h0:h0 + ATT_HEAD_GROUP]
            s = jnp.stack([
                jnp.concatenate([lax.dot_general(q_ref[r0:r0 + half, sl], k_refs[j][lo:hi, sl], nt,
                                                 preferred_element_type=F32)
                                 for j, (lo, hi) in enumerate(spans)], axis=1)
                for sl in grp])
            s = s * scale2 + bias_ref[h0:h0 + ATT_HEAD_GROUP, r0:r0 + half, c0:c1]
            m = jnp.max(s, axis=-1, keepdims=True)
            e = jnp.exp2(s - m)
            denom = jnp.sum(e, axis=-1, keepdims=True)
            p = e.astype(BF16)
            for h, sl in enumerate(grp):
                acc = None
                off = 0
                for j, (lo, hi) in enumerate(spans):
                    term = jnp.dot(p[h, :, off:off + hi - lo], v_refs[j][lo:hi, sl], preferred_element_type=F32)
                    acc = term if acc is None else acc + term
                    off += hi - lo
                o_ref[r0:r0 + half, sl] = (acc / denom[h]).astype(o_ref.dtype)


def _attn_bias(rel_table):
    qb, kw = ATT_QBLK, ATT_KBLKS * ATT_QBLK
    nh = rel_table.shape[0]
    qi = jnp.arange(qb)[:, None]
    kj = jnp.arange(kw)[None, :]
    off = kw - 1 - (ATT_KBLKS - 1) * qb
    glen = qb + kw
    n_lo = max(0, min(glen, off - REL_CLIP))
    n_lin = max(0, min(glen, off + REL_CLIP + 1) - n_lo)
    n_hi = glen - n_lo - n_lin
    lin0 = n_lo - off + REL_CLIP
    gr = jnp.concatenate([jnp.broadcast_to(rel_table[:, 2 * REL_CLIP:], (nh, n_hi)),
                          rel_table[:, lin0:lin0 + n_lin][:, ::-1],
                          jnp.broadcast_to(rel_table[:, :1], (nh, n_lo))], axis=1).astype(F32) * LOG2E
    c0 = glen - kw
    bias = jnp.tile(gr, (1, qb + 1))[:, c0:c0 + qb * (glen - 1)].reshape(nh, qb, glen - 1)[:, :, :kw]
    qc = qi // CHUNK + (ATT_KBLKS - 1) * (qb // CHUNK)
    kc = kj // CHUNK
    band = (kc <= qc) & (kc >= qc - LEFT_CHUNKS)
    tables = []
    for t in range(ATT_KBLKS):
        ok = band & (kj >= (ATT_KBLKS - 1 - t) * qb)
        tables.append(jnp.where(ok[None], bias, NEG_INF))
    return jnp.stack(tables)


def _attention(p_all, bias, batch, seq):
    n = p_all.shape[0]
    width = A_HEADS * A_HEAD_DIM
    qb = ATT_QBLK
    nb = seq // qb

    def kv_spec(back, colblk):
        return pl.BlockSpec((qb, width), lambda i, b: (b * nb + jnp.maximum(i - back, 0), colblk))

    return pl.pallas_call(
        _attn_kernel,
        grid=(nb, batch),
        in_specs=[
            pl.BlockSpec((qb, width), lambda i, b: (b * nb + i, 0)),
            kv_spec(2, 1), kv_spec(1, 1), kv_spec(0, 1),
            kv_spec(2, 2), kv_spec(1, 2), kv_spec(0, 2),
            pl.BlockSpec((None, A_HEADS, qb, ATT_KBLKS * qb), lambda i, b: (jnp.minimum(i, ATT_KBLKS - 1), 0, 0, 0)),
        ],
        out_specs=pl.BlockSpec((qb, width), lambda i, b: (b * nb + i, 0)),
        out_shape=jax.ShapeDtypeStruct((n, width), BF16),
        name="chunk_attn",
        compiler_params=_cparams(("arbitrary", "arbitrary")),
    )(p_all, p_all, p_all, p_all, p_all, p_all, p_all, bias)


def _log_sigmoid(t):
    return jnp.minimum(t, 0.0) - jnp.log(1.0 + jnp.exp(-jnp.abs(t)))


def _mlstm_kernel(q_ref, k_ref, v_ref, o_ref, g_ref, cq_ref, ck_ref, nw_ref, out_ref,
                  qbuf, kbuf, ct_ref, n_ref, m_ref):
    c = pl.program_id(1)
    L, D = CHUNK, M_HEAD_DIM
    nb = q_ref.shape[0]
    ns = nb * M_HEADS
    tail = 16

    @pl.when(c == 0)
    def _():
        qbuf[:, 0:tail, :] = jnp.zeros((nb, tail, qbuf.shape[2]), BF16)
        kbuf[:, 0:tail, :] = jnp.zeros((nb, tail, kbuf.shape[2]), BF16)
        ct_ref[...] = jnp.zeros(ct_ref.shape, F32)
        n_ref[...] = jnp.zeros(n_ref.shape, F32)
        m_ref[...] = jnp.zeros(m_ref.shape, F32)

    row = lax.broadcasted_iota(jnp.int32, (L, L), 0)
    colm = lax.broadcasted_iota(jnp.int32, (L, L), 1)
    causal = colm <= row
    eye = colm == row
    upper = (row <= colm).astype(F32)

    def to_col(r):
        return jnp.sum(jnp.where(eye, jnp.broadcast_to(r, (ns, L, L)), 0.0), axis=-1, keepdims=True)

    srow = lax.broadcasted_iota(jnp.int32, ((CONV_W - 1) * L, tail + L), 0)
    scol = lax.broadcasted_iota(jnp.int32, ((CONV_W - 1) * L, tail + L), 1)
    stap = srow // L
    shifts = (scol == srow - stap * L + stap + (tail - (CONV_W - 1))).astype(BF16)

    def conv(buf, x_ref, w_ref):
        shifted = []
        for bi in range(nb):
            buf[bi, tail:tail + L, :] = x_ref[bi]
            shifted.append(jnp.dot(shifts, buf[bi], preferred_element_type=F32))
            buf[bi, 0:tail, :] = buf[bi, L:L + tail, :]
        shifted = jnp.stack(shifted)
        acc = x_ref[...].astype(F32) * w_ref[CONV_W - 1:CONV_W, :]
        for j in range(CONV_W - 1):
            acc = acc + shifted[:, j * L:(j + 1) * L, :] * w_ref[j:j + 1, :]
        return acc

    def streams(x):
        return jnp.stack([x[bi, :, h * D:(h + 1) * D] for bi in range(nb) for h in range(M_HEADS)])

    q = streams(_silu(conv(qbuf, q_ref, cq_ref)) * (D ** -0.5))
    k = streams(_silu(conv(kbuf, k_ref, ck_ref)))
    qb16 = q.astype(BF16)
    kb16 = k.astype(BF16)
    v16 = [v_ref[bi, :, h * D:(h + 1) * D] for bi in range(nb) for h in range(M_HEADS)]

    g = g_ref[...]
    ig2 = jnp.concatenate([g[bi, 0:M_HEADS, :] for bi in range(nb)], axis=0)
    lf2 = _log_sigmoid(jnp.concatenate([g[bi, M_HEADS:2 * M_HEADS, :] for bi in range(nb)], axis=0))
    bcum2 = jnp.dot(lf2, upper, preferred_element_type=F32, precision=lax.Precision.HIGHEST)
    ig = jnp.stack([ig2[i:i + 1, :] for i in range(ns)])
    bcum = jnp.stack([bcum2[i:i + 1, :] for i in range(ns)])
    bcum_c = to_col(bcum)
    m_prev = m_ref[...].reshape(ns, 1, 1)

    logd = jnp.where(causal, bcum_c - bcum + ig, NEG_INF)
    inter = bcum_c + m_prev
    m_s = jnp.maximum(jnp.max(logd, axis=-1, keepdims=True), inter)
    nt = (((1,), (1,)), ((), ()))
    s = jnp.stack([lax.dot_general(qb16[i], kb16[i], nt, preferred_element_type=F32) for i in range(ns)])
    w_intra = s * jnp.exp(logd - m_s)
    w_inter = jnp.exp(inter - m_s)
    ct = ct_ref[...].reshape(ns, D, D)
    n_row = n_ref[...].reshape(ns, 1, D)
    wi16 = w_intra.astype(BF16)
    ct16 = ct.astype(BF16)
    num_intra = jnp.stack([jnp.dot(wi16[i], v16[i], preferred_element_type=F32) for i in range(ns)])
    num_inter = jnp.stack([jnp.dot(qb16[i], ct16[i], preferred_element_type=F32) for i in range(ns)])
    num = num_intra + w_inter * num_inter
    den = (jnp.sum(w_intra, axis=-1, keepdims=True)
           + w_inter * jnp.sum(q * n_row, axis=-1, keepdims=True))
    hs = num / jnp.maximum(jnp.abs(den), jnp.exp(-m_s))

    b_last = bcum[:, :, L - 1:L]
    log_wk = b_last - bcum + ig
    m_new = jnp.maximum(b_last + m_prev, jnp.max(log_wk, axis=-1, keepdims=True))
    wk = jnp.exp(log_wk - m_new)
    decay = jnp.exp(b_last + m_prev - m_new)
    kw = k * to_col(wk)
    kw16 = kw.astype(BF16)
    tn = (((0,), (0,)), ((), ()))
    upd = jnp.stack([lax.dot_general(kw16[i], v16[i], tn, preferred_element_type=F32) for i in range(ns)])
    ct_ref[...] = (decay * ct + upd).reshape(ct_ref.shape)
    n_ref[...] = (decay * n_row + jnp.sum(kw, axis=1, keepdims=True)).reshape(n_ref.shape)
    m_ref[...] = m_new.reshape(m_ref.shape)

    og = jnp.stack([o_ref[bi, :, h * D:(h + 1) * D] for bi in range(nb) for h in range(M_HEADS)]).astype(F32)
    nw = jnp.stack([nw_ref[:, h * D:(h + 1) * D] for _ in range(nb) for h in range(M_HEADS)])
    hm = _sigmoid(og) * hs
    y = (hm * lax.rsqrt(jnp.mean(hm * hm, axis=-1, keepdims=True) + EPS) * nw).astype(out_ref.dtype)
    for bi in range(nb):
        for h in range(M_HEADS):
            out_ref[bi, :, h * D:(h + 1) * D] = y[bi * M_HEADS + h]


def _mlstm(p_all, gates_t, conv_q, conv_k, norm_w, layer, batch, seq):
    n, cols = p_all.shape
    width = M_HEADS * M_HEAD_DIM
    nc = seq // CHUNK
    L = CHUNK
    nb = MLSTM_BATCH
    p3 = p_all.reshape(batch, seq, cols)

    def p_spec(colblk):
        return pl.BlockSpec((nb, L, width), lambda g, c: (g, c, colblk))

    out = pl.pallas_call(
        _mlstm_kernel,
        grid=(batch // nb, nc),
        in_specs=[
            p_spec(3), p_spec(4), p_spec(5), p_spec(6),
            pl.BlockSpec((nb, None, 2 * M_HEADS, L), lambda g, c: (g, c, 0, 0)),
            pl.BlockSpec((None, CONV_W, width), lambda g, c: (layer, 0, 0)),
            pl.BlockSpec((None, CONV_W, width), lambda g, c: (layer, 0, 0)),
            pl.BlockSpec((1, width), lambda g, c: (0, 0)),
        ],
        out_specs=pl.BlockSpec((nb, L, width), lambda g, c: (g, c, 0)),
        out_shape=jax.ShapeDtypeStruct((batch, seq, width), BF16),
        scratch_shapes=[
            pltpu.VMEM((nb, L + 16, width), BF16),
            pltpu.VMEM((nb, L + 16, width), BF16),
            pltpu.VMEM((nb, M_HEADS, M_HEAD_DIM, M_HEAD_DIM), F32),
            pltpu.VMEM((nb, M_HEADS, 1, M_HEAD_DIM), F32),
            pltpu.VMEM((nb, M_HEADS, 1, 1), F32),
        ],
        name="mlstm",
        compiler_params=_cparams(("arbitrary", "arbitrary")),
    )(p3, p3, p3, p3, gates_t, conv_q, conv_k, norm_w.reshape(1, width))
    return out.reshape(n, width)


def _tail_kernel(ya_ref, hm_ref, ga0_ref, ga1_ref, gm0_ref, gm1_ref, wa_ref, wm_ref, wo_ref, x_ref, mod_ref,
                 nw_ref, ws_ref, bs_ref, xo_ref, h_ref, s_ref, *, side_cols):
    a = jnp.dot(ya_ref[...], wa_ref[...], preferred_element_type=F32)
    m = jnp.dot(hm_ref[...], wm_ref[...], preferred_element_type=F32)
    ga = jnp.concatenate([ga0_ref[...], ga1_ref[...]], axis=1).astype(F32)
    gm = jnp.concatenate([gm0_ref[...], gm1_ref[...]], axis=1).astype(F32)
    merged = (_sigmoid(ga) * a + _sigmoid(gm) * m).astype(BF16)
    x = x_ref[...] + mod_ref[2:3, :] * jnp.dot(merged, wo_ref[...], preferred_element_type=F32)
    xo_ref[...] = x
    h = _rms_mod(x, nw_ref[...], mod_ref[4:5, :], mod_ref[3:4, :])
    h_ref[...] = _pack_bf16_pairs(h)
    s_ref[...] = _side_proj(h, ws_ref, bs_ref, side_cols, True)


def _mixer_tail(y_attn, h_m, p_all, w_ba16, w_bm16, w_out16, layer, x2, mod_l, norm_w, w_side, b_side, side_cols,
                seq, ga_col0, gm_col0, tm=256):
    n, ka = y_attn.shape
    km = h_m.shape[1]
    d = x2.shape[1]
    half = d // 2
    blocks_per_batch = seq // tm

    def gate_spec(col0, part):
        return pl.BlockSpec((tm, half), lambda i: (i, col0 // half + part))

    return pl.pallas_call(
        functools.partial(_tail_kernel, side_cols=side_cols),
        grid=(n // tm,),
        in_specs=[
            pl.BlockSpec((tm, ka), lambda i: (i, 0)),
            pl.BlockSpec((tm, km), lambda i: (i, 0)),
            gate_spec(ga_col0, 0), gate_spec(ga_col0, 1), gate_spec(gm_col0, 0), gate_spec(gm_col0, 1),
            pl.BlockSpec((None, ka, d), lambda i: (layer, 0, 0)),
            pl.BlockSpec((None, km, d), lambda i: (layer, 0, 0)),
            pl.BlockSpec((None, d, d), lambda i: (layer, 0, 0)),
            pl.BlockSpec((tm, d), lambda i: (i, 0)),
            pl.BlockSpec((None, 6, d), lambda i: (i // blocks_per_batch, 0, 0)),
            pl.BlockSpec((1, d), lambda i: (0, 0)),
            pl.BlockSpec((LANES, d), lambda i: (0, 0)),
            pl.BlockSpec((1, LANES), lambda i: (0, 0)),
        ],
        out_specs=[
            pl.BlockSpec((tm, d), lambda i: (i, 0)),
            pl.BlockSpec((tm, half), lambda i: (i, 0)),
            pl.BlockSpec((tm, LANES), lambda i: (i, 0)),
        ],
        out_shape=[jax.ShapeDtypeStruct((n, d), F32), jax.ShapeDtypeStruct((n, half), jnp.uint32),
                   jax.ShapeDtypeStruct((n, LANES), F32)],
        name="mixer_tail",
        compiler_params=_cparams(("arbitrary",)),
    )(y_attn, h_m, p_all, p_all, p_all, p_all, w_ba16, w_bm16, w_out16, x2, mod_l, norm_w.reshape(1, d),
      w_side, b_side)


def _moe_kernel(tok_ref, src_ref, be_ref, first_ref, nxt_ref, wslot_ref, nact_ref, h_hbm, wg_hbm, wu_hbm, wd_hbm,
                o_ref, xbuf, xb16, wg_st, wu_st, wd_st, xsem, wsem, *, layer):
    i = pl.program_id(0)
    nact = nact_ref[0]
    nbuf, blk = xbuf.shape[0], xbuf.shape[1]
    stages = ((wg_hbm, wg_st), (wu_hbm, wu_st), (wd_hbm, wd_st))

    def weight_copy(k, e, slot):
        return pltpu.make_async_copy(stages[k][0].at[layer, e], stages[k][1].at[slot], wsem.at[slot, k])

    weight_queue = 1

    def start_gather(j):
        base = src_ref[j]
        ring = j % nbuf
        dst = xbuf.at[ring]
        for r in range(blk):
            tok = tok_ref[base + r]
            pltpu.make_async_copy(h_hbm.at[pl.ds(tok, 1)], dst.at[pl.ds(r, 1)], xsem.at[ring]).start()

    def wait_gather(j):
        ring = j % nbuf
        pltpu.make_async_copy(h_hbm.at[pl.ds(0, blk)], xbuf.at[ring], xsem.at[ring]).wait()

    @pl.when(i == 0)
    def _():
        for k in range(3):
            weight_copy(k, be_ref[0], wslot_ref[0]).start(priority=weight_queue)
        for j in range(nbuf - 1):
            start_gather(j)

    @pl.when((i < nact) & (first_ref[i] == 1))
    def _():
        slot = wslot_ref[i]
        e_next = nxt_ref[i]

        @pl.when(e_next >= 0)
        def _():
            for k in range(3):
                weight_copy(k, e_next, 1 - slot).start(priority=weight_queue)

        for k in range(3):
            weight_copy(k, be_ref[i], slot).wait()

    def compute(slot):
        wait_gather(i)
        xb16[...] = _unpack_bf16_pairs(xbuf[i % nbuf]).astype(BF16)
        start_gather(i + nbuf - 1)
        x = xb16[...]
        g = jnp.dot(x, wg_st[slot].astype(BF16), preferred_element_type=F32)
        u = jnp.dot(x, wu_st[slot].astype(BF16), preferred_element_type=F32)
        a = (_silu(g) * u).astype(BF16)
        o_ref[...] = _pack_bf16_pairs(jnp.dot(a, wd_st[slot].astype(BF16), preferred_element_type=F32))

    for static_slot in range(2):
        @pl.when((i < nact) & (wslot_ref[i] == static_slot))
        def _(static_slot=static_slot):
            compute(static_slot)

    @pl.when(i >= nact)
    def _():
        @pl.when(i < nact + nbuf - 1)
        def _():
            wait_gather(i)

        o_ref[...] = jnp.zeros(o_ref.shape, o_ref.dtype)


def _moe_experts(h2, tok_src, blk_src, blk_expert, blk_first, blk_next, blk_wslot, n_active, w_gate, w_up, w_down,
                 layer):
    n, dh = h2.shape
    d = 2 * dh
    f = w_gate.shape[3]
    n_steps = blk_expert.shape[0]
    n_blocks = n_steps - (MOE_RING - 2)
    cap = n_blocks * MOE_BLK
    any_spec = pl.BlockSpec(memory_space=pl.ANY)
    grid_spec = pltpu.PrefetchScalarGridSpec(
        num_scalar_prefetch=7,
        grid=(n_steps,),
        in_specs=[any_spec, any_spec, any_spec, any_spec],
        out_specs=pl.BlockSpec((MOE_BLK, dh), lambda i, *_: (jnp.minimum(i, n_blocks - 1), 0)),
        scratch_shapes=[
            pltpu.VMEM((MOE_RING, MOE_BLK, dh), jnp.uint32),
            pltpu.VMEM((MOE_BLK, d), BF16),
            pltpu.VMEM((2, d, f), F32), pltpu.VMEM((2, d, f), F32), pltpu.VMEM((2, f, d), F32),
            pltpu.SemaphoreType.DMA((MOE_RING,)),
            pltpu.SemaphoreType.DMA((2, 3)),
        ],
    )
    return pl.pallas_call(
        functools.partial(_moe_kernel, layer=layer),
        grid_spec=grid_spec,
        out_shape=jax.ShapeDtypeStruct((cap, dh), jnp.uint32),
        name="moe_experts",
        compiler_params=_cparams(("arbitrary",)),
    )(tok_src, blk_src, blk_expert, blk_first, blk_next, blk_wslot, n_active, h2, w_gate, w_up, w_down)


def _combine_kernel(pos_ref, yb_hbm, x_ref, w_ref, mod_ref, nw_ref, nmod_ref, ws_ref, bs_ref, *out_and_scratch,
                    last, side_cols):
    if last:
        o_ref, buf, sem = out_and_scratch
    else:
        o_ref, h_ref, s_ref, buf, sem = out_and_scratch
    i = pl.program_id(0)
    nsteps = pl.num_programs(0)
    t = x_ref.shape[0]

    def start(j, slot):
        base = j * (t * TOP_K)
        dst = buf.at[slot]
        for r in range(t):
            for k in range(TOP_K):
                p = pos_ref[base + r * TOP_K + k]
                pltpu.make_async_copy(yb_hbm.at[pl.ds(p, 1)], dst.at[k, pl.ds(r, 1)], sem.at[slot]).start()

    def wait(slot):
        for k in range(TOP_K):
            pltpu.make_async_copy(yb_hbm.at[pl.ds(0, t)], buf.at[slot, k], sem.at[slot]).wait()

    @pl.when(i == 0)
    def _():
        start(0, 0)

    slot = i % 2

    @pl.when(i + 1 < nsteps)
    def _():
        start(i + 1, 1 - slot)

    wait(slot)
    w = w_ref[...]
    y = w[:, 0:1] * _unpack_bf16_pairs(buf[slot, 0]) + w[:, 1:2] * _unpack_bf16_pairs(buf[slot, 1])
    x = x_ref[...] + mod_ref[5:6, :] * y
    if last:
        o_ref[...] = x * lax.rsqrt(jnp.mean(x * x, axis=-1, keepdims=True) + EPS) * nw_ref[...]
    else:
        o_ref[...] = x
        h = _rms_mod(x, nw_ref[...], nmod_ref[1:2, :], nmod_ref[0:1, :])
        h_ref[...] = h.astype(h_ref.dtype)
        s_ref[...] = _side_proj(h, ws_ref, bs_ref, side_cols, False)


def _combine(yb, pos, weights, x2, mod_l, seq, next_norm_w, next_mod, w_side, side_spec, b_side, side_cols, last,
             tm=256):
    n, d = x2.shape
    blocks_per_batch = seq // tm
    row_spec = pl.BlockSpec((tm, d), lambda i, *_: (i, 0))
    mod_spec = pl.BlockSpec((None, 6, d), lambda i, *_: (i // blocks_per_batch, 0, 0))
    if last:
        out_specs = row_spec
        out_shape = jax.ShapeDtypeStruct((n, d), F32)
    else:
        out_specs = [row_spec, row_spec, pl.BlockSpec((tm, LANES), lambda i, *_: (i, 0))]
        out_shape = [jax.ShapeDtypeStruct((n, d), F32), jax.ShapeDtypeStruct((n, d), BF16),
                     jax.ShapeDtypeStruct((n, LANES), F32)]
    grid_spec = pltpu.PrefetchScalarGridSpec(
        num_scalar_prefetch=1,
        grid=(n // tm,),
        in_specs=[
            pl.BlockSpec(memory_space=pl.ANY),
            row_spec,
            pl.BlockSpec((tm, TOP_K), lambda i, *_: (i, 0)),
            mod_spec,
            pl.BlockSpec((1, d), lambda i, *_: (0, 0)),
            mod_spec,
            side_spec,
            pl.BlockSpec((1, LANES), lambda i, *_: (0, 0)),
        ],
        out_specs=out_specs,
        scratch_shapes=[pltpu.VMEM((2, TOP_K, tm, yb.shape[1]), yb.dtype), pltpu.SemaphoreType.DMA((2,))],
    )
    return pl.pallas_call(
        functools.partial(_combine_kernel, last=last, side_cols=side_cols),
        grid_spec=grid_spec,
        out_shape=out_shape,
        name="moe_combine",
        compiler_params=_cparams(("arbitrary",)),
    )(pos.reshape(-1), yb, x2, weights, mod_l, next_norm_w.reshape(1, d), next_mod, w_side, b_side)


def _route_kernel(lg_ref, oi_ref, ow_ref, cnt_ref, carry):
    @pl.when(pl.program_id(0) == 0)
    def _():
        carry[...] = jnp.zeros(carry.shape, F32)

    lg = lg_ref[...]
    t = lg.shape[0]
    lane = lax.broadcasted_iota(jnp.int32, lg.shape, 1)
    big = jnp.int32(1 << 30)

    def first_max(vals):
        top = jnp.max(vals, axis=-1, keepdims=True)
        return top, jnp.min(jnp.where(vals == top, lane, big), axis=-1, keepdims=True)

    coarse = lane < N_GROUPS
    gmax, grp = first_max(jnp.where(coarse, lg, NEG_INF))
    p_grp = 1.0 / jnp.sum(jnp.where(coarse, jnp.exp(lg - gmax), 0.0), axis=-1, keepdims=True)
    lo = N_GROUPS + grp * EXPERTS_PER_GROUP
    fine = jnp.where((lane >= lo) & (lane < lo + EXPERTS_PER_GROUP), lg, NEG_INF)
    v1, i1 = first_max(fine)
    v2, i2 = first_max(jnp.where(lane == i1, NEG_INF, fine))
    r = jnp.exp(v2 - v1)
    w1 = p_grp / (1.0 + r)
    w2 = w1 * r
    e1 = i1 - N_GROUPS
    e2 = i2 - N_GROUPS

    hit1 = lane == e1
    hit2 = lane == e2
    picks = (hit1 | hit2).astype(BF16)
    row = lax.broadcasted_iota(jnp.int32, (t, t), 0)
    col = lax.broadcasted_iota(jnp.int32, (t, t), 1)
    before = (col < row).astype(BF16)
    seen = carry[...] + jnp.dot(before, picks, preferred_element_type=F32)
    rank1 = jnp.sum(jnp.where(hit1, seen, 0.0), axis=-1, keepdims=True).astype(jnp.int32)
    rank2 = jnp.sum(jnp.where(hit2, seen, 0.0), axis=-1, keepdims=True).astype(jnp.int32)
    carry[...] = carry[...] + jnp.sum(picks.astype(F32), axis=0, keepdims=True)

    oi_ref[...] = jnp.where(lane == 0, e1, jnp.where(lane == 1, e2, jnp.where(lane == 2, rank1, rank2)))
    ow_ref[...] = jnp.where(lane == 0, w1, w2)
    cnt_ref[...] = carry[...]


def _route(logits, tm=512):
    n = logits.shape[0]
    tm = min(tm, n)
    oi, ow, cnt = pl.pallas_call(
        _route_kernel,
        grid=(n // tm,),
        in_specs=[pl.BlockSpec((tm, LANES), lambda i: (i, 0))],
        out_specs=[pl.BlockSpec((tm, LANES), lambda i: (i, 0)), pl.BlockSpec((tm, LANES), lambda i: (i, 0)),
                   pl.BlockSpec((1, LANES), lambda i: (0, 0))],
        out_shape=[jax.ShapeDtypeStruct((n, LANES), jnp.int32), jax.ShapeDtypeStruct((n, LANES), F32),
                   jax.ShapeDtypeStruct((1, LANES), F32)],
        scratch_shapes=[pltpu.VMEM((1, LANES), F32)],
        name="route",
        compiler_params=_cparams(("arbitrary",)),
    )(logits)
    expert = oi[:, 0:TOP_K]
    rank = oi[:, TOP_K:2 * TOP_K]
    weights = ow[:, 0:TOP_K]
    counts = cnt[0, :N_EXPERTS].astype(jnp.int32)
    return expert, rank, weights, counts


def _dispatch(expert, rank, counts):
    n_tok = expert.shape[0]
    n_assign = n_tok * TOP_K
    cap = n_assign + N_EXPERTS * MOE_BLK
    n_blocks = cap // MOE_BLK + MOE_RING - 2
    e_flat = expert.reshape(-1)
    padded = ((counts + MOE_BLK - 1) // MOE_BLK) * MOE_BLK
    pad_ends = jnp.cumsum(padded)
    pad_starts = pad_ends - padded
    dest = (pad_starts[e_flat] + rank.reshape(-1)).astype(jnp.int32)
    order = jnp.argsort(e_flat, stable=True)
    tok_src = jnp.concatenate([(order // TOP_K).astype(jnp.int32), jnp.zeros((MOE_BLK,), jnp.int32)])
    starts = jnp.cumsum(counts) - counts
    blk_start = jnp.arange(n_blocks, dtype=jnp.int32) * MOE_BLK
    blk_expert = jnp.minimum(jnp.sum((pad_ends[None, :] <= blk_start[:, None]).astype(jnp.int32), axis=1),
                             N_EXPERTS - 1).astype(jnp.int32)
    blk_src = jnp.clip(blk_start - (pad_starts - starts)[blk_expert], 0, n_assign).astype(jnp.int32)
    n_active = (pad_ends[-1] // MOE_BLK).astype(jnp.int32)
    prev = jnp.concatenate([jnp.full((1,), -1, jnp.int32), blk_expert[:-1]])
    blk_first = (blk_expert != prev).astype(jnp.int32)
    run_end = pad_ends[blk_expert] // MOE_BLK
    blk_next = jnp.where(run_end < n_active, blk_expert[jnp.minimum(run_end, n_blocks - 1)], -1).astype(jnp.int32)
    blk_wslot = ((jnp.cumsum((counts > 0).astype(jnp.int32)) - 1)[blk_expert] % 2).astype(jnp.int32)
    return (tok_src, blk_src, blk_expert, blk_first, blk_next, blk_wslot, n_active.reshape(1),
            dest.reshape(n_tok, TOP_K))


def kernel(x, c, ada_w, ada_b, norm1_w, norm2_w, w_in, conv_q, conv_k, igate_b, fgate_b, rel_bias,
           mlstm_norm_w, w_branch_attn, w_branch_mlstm, w_out, router_coarse_w, router_coarse_b,
           router_fine_w, router_fine_b, w_gate, w_up, w_down, final_norm_w):
    b, s, d = x.shape
    depth = ada_w.shape[0]
    n = b * s
    nc = s // CHUNK
    a_width = A_HEADS * A_HEAD_DIM
    m_width = M_HEADS * M_HEAD_DIM
    main_cols = 3 * a_width + 4 * m_width + 2 * d
    ga_col0 = 3 * a_width + 4 * m_width
    gm_col0 = ga_col0 + d

    mod = _ada_mod(c, ada_w, ada_b)
    x2 = x.reshape(n, d)

    w_in_t = jnp.swapaxes(w_in, 1, 2)
    w_ba16 = w_branch_attn.astype(BF16)
    w_bm16 = w_branch_mlstm.astype(BF16)
    w_out16 = w_out.astype(BF16)

    def gate_spec(l):
        return pl.BlockSpec((None, LANES, d), lambda i, *_: (l, main_cols // LANES, 0))

    def gate_bias(l):
        return jnp.zeros((1, LANES), F32).at[0, :M_HEADS].set(igate_b[l]).at[0, M_HEADS:2 * M_HEADS].set(fgate_b[l])

    h, gates = _norm_mod(x2, norm1_w[0], mod[0], w_in_t, gate_spec(0), 2 * M_HEADS, gate_bias(0), seq=s,
                         shift_row=0, scale_row=1, precise=False, out_dtype=BF16)
    out = None
    for l in range(depth):
        w_r = (jnp.zeros((LANES, d), F32).at[:N_GROUPS].set(router_coarse_w[l].T)
               .at[N_GROUPS:N_GROUPS + N_EXPERTS].set(router_fine_w[l].T))
        b_r = (jnp.zeros((1, LANES), F32).at[0, :N_GROUPS].set(router_coarse_b[l])
               .at[0, N_GROUPS:N_GROUPS + N_EXPERTS].set(router_fine_b[l]))

        p_all = _proj(h, w_in_t, l, main_cols)
        y_attn = _attention(p_all, _attn_bias(rel_bias[l]), b, s)
        gates_t = gates[:, :2 * M_HEADS].reshape(b, nc, CHUNK, 2 * M_HEADS).transpose(0, 1, 3, 2)
        h_m = _mlstm(p_all, gates_t, conv_q, conv_k, mlstm_norm_w[l], l, b, s)
        x2, h2, logits = _mixer_tail(y_attn, h_m, p_all, w_ba16, w_bm16, w_out16, l, x2, mod[l], norm2_w[l],
                                     w_r, b_r, N_GROUPS + N_EXPERTS, s, ga_col0, gm_col0)
        expert, rank, weights, counts = _route(logits)
        tok_src, blk_src, blk_expert, blk_first, blk_next, blk_wslot, n_active, pos = _dispatch(expert, rank, counts)
        yb = _moe_experts(h2, tok_src, blk_src, blk_expert, blk_first, blk_next, blk_wslot, n_active,
                          w_gate, w_up, w_down, l)
        if l + 1 < depth:
            x2, h, gates = _combine(yb, pos, weights, x2, mod[l], s, norm1_w[l + 1], mod[l + 1], w_in_t,
                                    gate_spec(l + 1), gate_bias(l + 1), 2 * M_HEADS, last=False)
        else:
            out = _combine(yb, pos, weights, x2, mod[l], s, final_norm_w, mod[l], w_in_t, gate_spec(l),
                           gate_bias(l), 2 * M_HEADS, last=True)

    return out.reshape(b, s, d)
```

```python
import functools

import jax
import jax.numpy as jnp
from jax import lax
from jax.experimental import pallas as pl
from jax.experimental.pallas import tpu as pltpu

F32 = jnp.float32
BF16 = jnp.bfloat16

EPS = 1e-6
NEG_INF = -1e30
LOG2E = 1.4426950408889634
CHUNK = 64
LEFT_CHUNKS = 8
REL_CLIP = 256
A_HEADS = 8
A_HEAD_DIM = 128
M_HEADS = 4
M_HEAD_DIM = 256
CONV_W = 4
N_GROUPS = 4
EXPERTS_PER_GROUP = 8
N_EXPERTS = N_GROUPS * EXPERTS_PER_GROUP
TOP_K = 2

LANES = 128
VMEM_LIMIT = 60 * 1024 * 1024

ATT_QBLK = 256
ATT_HEAD_GROUP = 4
ATT_KBLKS = 3
MLSTM_BATCH = 4
MOE_RING = 3
MOE_BLK = 256


def _cparams(sem):
    return pltpu.CompilerParams(dimension_semantics=sem, vmem_limit_bytes=VMEM_LIMIT)


def _sigmoid(t):
    return 0.5 * jnp.tanh(0.5 * t) + 0.5


def _silu(t):
    return t * _sigmoid(t)


def _pack_bf16_pairs(x):
    c = x.shape[1] // 2
    lo = lax.bitcast_convert_type(x[:, :c].astype(BF16).astype(F32), jnp.uint32)
    hi = lax.bitcast_convert_type(x[:, c:].astype(BF16).astype(F32), jnp.uint32)
    return (lo >> 16) | hi


def _unpack_bf16_pairs(w):
    lo = lax.bitcast_convert_type(w << 16, F32)
    hi = lax.bitcast_convert_type(w & jnp.uint32(0xFFFF0000), F32)
    return jnp.concatenate([lo, hi], axis=1)


def _ada_kernel(c_ref, w_ref, b_ref, o_ref):
    w = w_ref[...].astype(BF16)
    r = jnp.dot(c_ref[...], w, preferred_element_type=F32)
    bp = o_ref.shape[0]
    o_ref[...] = r[:bp] + r[bp:] + b_ref[...]


def _ada_mod(c, ada_w, ada_b):
    depth, d, n6 = ada_w.shape
    b = c.shape[0]
    bp = 8
    c_pad = jnp.zeros((bp, d), F32).at[:b].set(c)
    c_hi = c_pad.astype(BF16)
    c_lo = (c_pad - c_hi.astype(F32)).astype(BF16)
    c2 = jnp.concatenate([c_hi, c_lo], axis=0)
    tn = 1024
    out = pl.pallas_call(
        _ada_kernel,
        grid=(depth, n6 // tn),
        in_specs=[
            pl.BlockSpec((2 * bp, d), lambda l, j: (0, 0)),
            pl.BlockSpec((None, d, tn), lambda l, j: (l, 0, j)),
            pl.BlockSpec((None, 1, tn), lambda l, j: (l, 0, j)),
        ],
        out_specs=pl.BlockSpec((None, bp, tn), lambda l, j: (l, 0, j)),
        out_shape=jax.ShapeDtypeStruct((depth, bp, n6), F32),
        name="ada_mod",
        compiler_params=_cparams(("arbitrary", "arbitrary")),
    )(c2, ada_w, ada_b.reshape(depth, 1, n6))
    return out[:, :b].reshape(depth, b, 6, d)


def _rms_mod(x, nw, scale, shift):
    y = x * lax.rsqrt(jnp.mean(x * x, axis=-1, keepdims=True) + EPS)
    return (y * nw) * (1.0 + scale) + shift


def _side_proj(h, ws_ref, bs_ref, side_cols, precise):
    wrow = lax.broadcasted_iota(jnp.int32, ws_ref.shape, 0)
    ws = jnp.where(wrow < side_cols, ws_ref[...], 0.0)
    nt = (((1,), (1,)), ((), ()))
    if precise:
        h_hi = h.astype(BF16)
        h_lo = (h - h_hi.astype(F32)).astype(BF16)
        w_hi = ws.astype(BF16)
        s = (lax.dot_general(h_hi, w_hi, nt, preferred_element_type=F32)
             + lax.dot_general(h_lo, w_hi, nt, preferred_element_type=F32))
    else:
        s = lax.dot_general(h.astype(BF16), ws.astype(BF16), nt, preferred_element_type=F32)
    return s + bs_ref[...]


def _norm_kernel(x_ref, nw_ref, mod_ref, ws_ref, bs_ref, h_ref, s_ref, *, shift_row, scale_row, precise,
                 side_cols):
    h = _rms_mod(x_ref[...], nw_ref[...], mod_ref[scale_row:scale_row + 1, :], mod_ref[shift_row:shift_row + 1, :])
    h_ref[...] = _pack_bf16_pairs(h) if h_ref.dtype == jnp.uint32 else h.astype(h_ref.dtype)
    s_ref[...] = _side_proj(h, ws_ref, bs_ref, side_cols, precise)


def _norm_mod(x2, norm_w, mod_l, w_side, side_spec, side_cols, b_side, *, seq, shift_row, scale_row, precise,
              out_dtype, tm=512):
    n, d = x2.shape
    tm = min(tm, seq)
    blocks_per_batch = seq // tm
    dh = d // 2 if out_dtype == jnp.uint32 else d
    kern = functools.partial(_norm_kernel, shift_row=shift_row, scale_row=scale_row, precise=precise,
                             side_cols=side_cols)
    return pl.pallas_call(
        kern,
        grid=(n // tm,),
        in_specs=[
            pl.BlockSpec((tm, d), lambda i: (i, 0)),
            pl.BlockSpec((1, d), lambda i: (0, 0)),
            pl.BlockSpec((None, 6, d), lambda i: (i // blocks_per_batch, 0, 0)),
            side_spec,
            pl.BlockSpec((1, LANES), lambda i: (0, 0)),
        ],
        out_specs=[
            pl.BlockSpec((tm, dh), lambda i: (i, 0)),
            pl.BlockSpec((tm, LANES), lambda i: (i, 0)),
        ],
        out_shape=[jax.ShapeDtypeStruct((n, dh), out_dtype), jax.ShapeDtypeStruct((n, LANES), F32)],
        name="norm_mod",
        compiler_params=_cparams(("arbitrary",)),
    )(x2, norm_w.reshape(1, d), mod_l, w_side, b_side)


def _proj_kernel(a_ref, wt_ref, o_ref, wb_ref):
    @pl.when(pl.program_id(1) == 0)
    def _():
        wb_ref[...] = wt_ref[...].astype(BF16)

    o_ref[...] = lax.dot_general(a_ref[...], wb_ref[...], (((1,), (1,)), ((), ())),
                                 preferred_element_type=F32).astype(o_ref.dtype)


def _proj(a, wt_stack, layer, n_cols, tm=2048, tn=1024):
    m, k = a.shape
    tm = min(tm, m)
    return pl.pallas_call(
        _proj_kernel,
        grid=(n_cols // tn, m // tm),
        in_specs=[
            pl.BlockSpec((tm, k), lambda j, i: (i, 0)),
            pl.BlockSpec((None, tn, k), lambda j, i: (layer, j, 0)),
        ],
        out_specs=pl.BlockSpec((tm, tn), lambda j, i: (i, j)),
        out_shape=jax.ShapeDtypeStruct((m, n_cols), BF16),
        scratch_shapes=[pltpu.VMEM((tn, k), BF16)],
        name="in_proj",
        compiler_params=_cparams(("arbitrary", "arbitrary")),
    )(a, wt_stack)


def _attn_kernel(q_ref, k0_ref, k1_ref, k2_ref, v0_ref, v1_ref, v2_ref, bias_ref, o_ref):
    k_refs = (k0_ref, k1_ref, k2_ref)
    v_refs = (v0_ref, v1_ref, v2_ref)
    qb = q_ref.shape[0]
    half = qb // 2
    scale2 = (A_HEAD_DIM ** -0.5) * LOG2E
    heads = [slice(h * A_HEAD_DIM, (h + 1) * A_HEAD_DIM) for h in range(A_HEADS)]
    nt = (((1,), (1,)), ((), ()))
    for part in range(2):
        r0 = part * half
        c0 = part * half
        c1 = c0 + ATT_KBLKS * qb - half
        spans = [(max(c0, j * qb) - j * qb, min(c1, (j + 1) * qb) - j * qb) for j in range(ATT_KBLKS)]
        for h0 in range(0, A_HEADS, ATT_HEAD_GROUP):
            grp = heads[h0:h0 + ATT_HEAD_GROUP]
            s = jnp.stack([
                jnp.concatenate([lax.dot_general(q_ref[r0:r0 + half, sl], k_refs[j][lo:hi, sl], nt,
                                                 preferred_element_type=F32)
                                 for j, (lo, hi) in enumerate(spans)], axis=1)
                for sl in grp])
            s = s * scale2 + bias_ref[h0:h0 + ATT_HEAD_GROUP, r0:r0 + half, c0:c1]
            m = jnp.max(s, axis=-1, keepdims=True)
            e = jnp.exp2(s - m)
            denom = jnp.sum(e, axis=-1, keepdims=True)
            p = e.astype(BF16)
            for h, sl in enumerate(grp):
                acc = None
                off = 0
                for j, (lo, hi) in enumerate(spans):
                    term = jnp.dot(p[h, :, off:off + hi - lo], v_refs[j][lo:hi, sl], preferred_element_type=F32)
                    acc = term if acc is None else acc + term
                    off += hi - lo
                o_ref[r0:r0 + half, sl] = (acc / denom[h]).astype(o_ref.dtype)


def _attn_bias(rel_table):
    qb, kw = ATT_QBLK, ATT_KBLKS * ATT_QBLK
    nh = rel_table.shape[0]
    qi = jnp.arange(qb)[:, None]
    kj = jnp.arange(kw)[None, :]
    off = kw - 1 - (ATT_KBLKS - 1) * qb
    glen = qb + kw
    n_lo = max(0, min(glen, off - REL_CLIP))
    n_lin = max(0, min(glen, off + REL_CLIP + 1) - n_lo)
    n_hi = glen - n_lo - n_lin
    lin0 = n_lo - off + REL_CLIP
    gr = jnp.concatenate([jnp.broadcast_to(rel_table[:, 2 * REL_CLIP:], (nh, n_hi)),
                          rel_table[:, lin0:lin0 + n_lin][:, ::-1],
                          jnp.broadcast_to(rel_table[:, :1], (nh, n_lo))], axis=1).astype(F32) * LOG2E
    c0 = glen - kw
    bias = jnp.tile(gr, (1, qb + 1))[:, c0:c0 + qb * (glen - 1)].reshape(nh, qb, glen - 1)[:, :, :kw]
    qc = qi // CHUNK + (ATT_KBLKS - 1) * (qb // CHUNK)
    kc = kj // CHUNK
    band = (kc <= qc) & (kc >= qc - LEFT_CHUNKS)
    tables = []
    for t in range(ATT_KBLKS):
        ok = band & (kj >= (ATT_KBLKS - 1 - t) * qb)
        tables.append(jnp.where(ok[None], bias, NEG_INF))
    return jnp.stack(tables)


def _attention(p_all, bias, batch, seq):
    n = p_all.shape[0]
    width = A_HEADS * A_HEAD_DIM
    qb = ATT_QBLK
    nb = seq // qb

    def kv_spec(back, colblk):
        return pl.BlockSpec((qb, width), lambda i, b: (b * nb + jnp.maximum(i - back, 0), colblk))

    return pl.pallas_call(
        _attn_kernel,
        grid=(nb, batch),
        in_specs=[
            pl.BlockSpec((qb, width), lambda i, b: (b * nb + i, 0)),
            kv_spec(2, 1), kv_spec(1, 1), kv_spec(0, 1),
            kv_spec(2, 2), kv_spec(1, 2), kv_spec(0, 2),
            pl.BlockSpec((None, A_HEADS, qb, ATT_KBLKS * qb), lambda i, b: (jnp.minimum(i, ATT_KBLKS - 1), 0, 0, 0)),
        ],
        out_specs=pl.BlockSpec((qb, width), lambda i, b: (b * nb + i, 0)),
        out_shape=jax.ShapeDtypeStruct((n, width), BF16),
        name="chunk_attn",
        compiler_params=_cparams(("arbitrary", "arbitrary")),
    )(p_all, p_all, p_all, p_all, p_all, p_all, p_all, bias)


def _log_sigmoid(t):
    return jnp.minimum(t, 0.0) - jnp.log(1.0 + jnp.exp(-jnp.abs(t)))


def _mlstm_kernel(q_ref, k_ref, v_ref, o_ref, g_ref, cq_ref, ck_ref, nw_ref, out_ref,
                  qbuf, kbuf, ct_ref, n_ref, m_ref):
    c = pl.program_id(1)
    L, D = CHUNK, M_HEAD_DIM
    nb = q_ref.shape[0]
    ns = nb * M_HEADS
    tail = 16

    @pl.when(c == 0)
    def _():
        qbuf[:, 0:tail, :] = jnp.zeros((nb, tail, qbuf.shape[2]), BF16)
        kbuf[:, 0:tail, :] = jnp.zeros((nb, tail, kbuf.shape[2]), BF16)
        ct_ref[...] = jnp.zeros(ct_ref.shape, F32)
        n_ref[...] = jnp.zeros(n_ref.shape, F32)
        m_ref[...] = jnp.zeros(m_ref.shape, F32)

    row = lax.broadcasted_iota(jnp.int32, (L, L), 0)
    colm = lax.broadcasted_iota(jnp.int32, (L, L), 1)
    causal = colm <= row
    eye = colm == row
    upper = (row <= colm).astype(F32)

    def to_col(r):
        return jnp.sum(jnp.where(eye, jnp.broadcast_to(r, (ns, L, L)), 0.0), axis=-1, keepdims=True)

    srow = lax.broadcasted_iota(jnp.int32, ((CONV_W - 1) * L, tail + L), 0)
    scol = lax.broadcasted_iota(jnp.int32, ((CONV_W - 1) * L, tail + L), 1)
    stap = srow // L
    shifts = (scol == srow - stap * L + stap + (tail - (CONV_W - 1))).astype(BF16)

    def conv(buf, x_ref, w_ref):
        shifted = []
        for bi in range(nb):
            buf[bi, tail:tail + L, :] = x_ref[bi]
            shifted.append(jnp.dot(shifts, buf[bi], preferred_element_type=F32))
            buf[bi, 0:tail, :] = buf[bi, L:L + tail, :]
        shifted = jnp.stack(shifted)
        acc = x_ref[...].astype(F32) * w_ref[CONV_W - 1:CONV_W, :]
        for j in range(CONV_W - 1):
            acc = acc + shifted[:, j * L:(j + 1) * L, :] * w_ref[j:j + 1, :]
        return acc

    def streams(x):
        return jnp.stack([x[bi, :, h * D:(h + 1) * D] for bi in range(nb) for h in range(M_HEADS)])

    q = streams(_silu(conv(qbuf, q_ref, cq_ref)) * (D ** -0.5))
    k = streams(_silu(conv(kbuf, k_ref, ck_ref)))
    qb16 = q.astype(BF16)
    kb16 = k.astype(BF16)
    v16 = [v_ref[bi, :, h * D:(h + 1) * D] for bi in range(nb) for h in range(M_HEADS)]

    g = g_ref[...]
    ig2 = jnp.concatenate([g[bi, 0:M_HEADS, :] for bi in range(nb)], axis=0)
    lf2 = _log_sigmoid(jnp.concatenate([g[bi, M_HEADS:2 * M_HEADS, :] for bi in range(nb)], axis=0))
    bcum2 = jnp.dot(lf2, upper, preferred_element_type=F32, precision=lax.Precision.HIGHEST)
    ig = jnp.stack([ig2[i:i + 1, :] for i in range(ns)])
    bcum = jnp.stack([bcum2[i:i + 1, :] for i in range(ns)])
    bcum_c = to_col(bcum)
    m_prev = m_ref[...].reshape(ns, 1, 1)

    logd = jnp.where(causal, bcum_c - bcum + ig, NEG_INF)
    inter = bcum_c + m_prev
    m_s = jnp.maximum(jnp.max(logd, axis=-1, keepdims=True), inter)
    nt = (((1,), (1,)), ((), ()))
    s = jnp.stack([lax.dot_general(qb16[i], kb16[i], nt, preferred_element_type=F32) for i in range(ns)])
    w_intra = s * jnp.exp(logd - m_s)
    w_inter = jnp.exp(inter - m_s)
    ct = ct_ref[...].reshape(ns, D, D)
    n_row = n_ref[...].reshape(ns, 1, D)
    wi16 = w_intra.astype(BF16)
    ct16 = ct.astype(BF16)
    num_intra = jnp.stack([jnp.dot(wi16[i], v16[i], preferred_element_type=F32) for i in range(ns)])
    num_inter = jnp.stack([jnp.dot(qb16[i], ct16[i], preferred_element_type=F32) for i in range(ns)])
    num = num_intra + w_inter * num_inter
    den = (jnp.sum(w_intra, axis=-1, keepdims=True)
           + w_inter * jnp.sum(q * n_row, axis=-1, keepdims=True))
    hs = num / jnp.maximum(jnp.abs(den), jnp.exp(-m_s))

    b_last = bcum[:, :, L - 1:L]
    log_wk = b_last - bcum + ig
    m_new = jnp.maximum(b_last + m_prev, jnp.max(log_wk, axis=-1, keepdims=True))
    wk = jnp.exp(log_wk - m_new)
    decay = jnp.exp(b_last + m_prev - m_new)
    kw = k * to_col(wk)
    kw16 = kw.astype(BF16)
    tn = (((0,), (0,)), ((), ()))
    upd = jnp.stack([lax.dot_general(kw16[i], v16[i], tn, preferred_element_type=F32) for i in range(ns)])
    ct_ref[...] = (decay * ct + upd).reshape(ct_ref.shape)
    n_ref[...] = (decay * n_row + jnp.sum(kw, axis=1, keepdims=True)).reshape(n_ref.shape)
    m_ref[...] = m_new.reshape(m_ref.shape)

    og = jnp.stack([o_ref[bi, :, h * D:(h + 1) * D] for bi in range(nb) for h in range(M_HEADS)]).astype(F32)
    nw = jnp.stack([nw_ref[:, h * D:(h + 1) * D] for _ in range(nb) for h in range(M_HEADS)])
    hm = _sigmoid(og) * hs
    y = (hm * lax.rsqrt(jnp.mean(hm * hm, axis=-1, keepdims=True) + EPS) * nw).astype(out_ref.dtype)
    for bi in range(nb):
        for h in range(M_HEADS):
            out_ref[bi, :, h * D:(h + 1) * D] = y[bi * M_HEADS + h]


def _mlstm(p_all, gates_t, conv_q, conv_k, norm_w, layer, batch, seq):
    n, cols = p_all.shape
    width = M_HEADS * M_HEAD_DIM
    nc = seq // CHUNK
    L = CHUNK
    nb = MLSTM_BATCH
    p3 = p_all.reshape(batch, seq, cols)

    def p_spec(colblk):
        return pl.BlockSpec((nb, L, width), lambda g, c: (g, c, colblk))

    out = pl.pallas_call(
        _mlstm_kernel,
        grid=(batch // nb, nc),
        in_specs=[
            p_spec(3), p_spec(4), p_spec(5), p_spec(6),
            pl.BlockSpec((nb, None, 2 * M_HEADS, L), lambda g, c: (g, c, 0, 0)),
            pl.BlockSpec((None, CONV_W, width), lambda g, c: (layer, 0, 0)),
            pl.BlockSpec((None, CONV_W, width), lambda g, c: (layer, 0, 0)),
            pl.BlockSpec((1, width), lambda g, c: (0, 0)),
        ],
        out_specs=pl.BlockSpec((nb, L, width), lambda g, c: (g, c, 0)),
        out_shape=jax.ShapeDtypeStruct((batch, seq, width), BF16),
        scratch_shapes=[
            pltpu.VMEM((nb, L + 16, width), BF16),
            pltpu.VMEM((nb, L + 16, width), BF16),
            pltpu.VMEM((nb, M_HEADS, M_HEAD_DIM, M_HEAD_DIM), F32),
            pltpu.VMEM((nb, M_HEADS, 1, M_HEAD_DIM), F32),
            pltpu.VMEM((nb, M_HEADS, 1, 1), F32),
        ],
        name="mlstm",
        compiler_params=_cparams(("arbitrary", "arbitrary")),
    )(p3, p3, p3, p3, gates_t, conv_q, conv_k, norm_w.reshape(1, width))
    return out.reshape(n, width)


def _tail_kernel(ya_ref, hm_ref, ga0_ref, ga1_ref, gm0_ref, gm1_ref, wa_ref, wm_ref, wo_ref, x_ref, mod_ref,
                 nw_ref, ws_ref, bs_ref, xo_ref, h_ref, s_ref, *, side_cols):
    a = jnp.dot(ya_ref[...], wa_ref[...], preferred_element_type=F32)
    m = jnp.dot(hm_ref[...], wm_ref[...], preferred_element_type=F32)
    ga = jnp.concatenate([ga0_ref[...], ga1_ref[...]], axis=1).astype(F32)
    gm = jnp.concatenate([gm0_ref[...], gm1_ref[...]], axis=1).astype(F32)
    merged = (_sigmoid(ga) * a + _sigmoid(gm) * m).astype(BF16)
    x = x_ref[...] + mod_ref[2:3, :] * jnp.dot(merged, wo_ref[...], preferred_element_type=F32)
    xo_ref[...] = x
    h = _rms_mod(x, nw_ref[...], mod_ref[4:5, :], mod_ref[3:4, :])
    h_ref[...] = _pack_bf16_pairs(h)
    s_ref[...] = _side_proj(h, ws_ref, bs_ref, side_cols, True)


def _mixer_tail(y_attn, h_m, p_all, w_ba16, w_bm16, w_out16, layer, x2, mod_l, norm_w, w_side, b_side, side_cols,
                seq, ga_col0, gm_col0, tm=256):
    n, ka = y_attn.shape
    km = h_m.shape[1]
    d = x2.shape[1]
    half = d // 2
    blocks_per_batch = seq // tm

    def gate_spec(col0, part):
        return pl.BlockSpec((tm, half), lambda i: (i, col0 // half + part))

    return pl.pallas_call(
        functools.partial(_tail_kernel, side_cols=side_cols),
        grid=(n // tm,),
        in_specs=[
            pl.BlockSpec((tm, ka), lambda i: (i, 0)),
            pl.BlockSpec((tm, km), lambda i: (i, 0)),
            gate_spec(ga_col0, 0), gate_spec(ga_col0, 1), gate_spec(gm_col0, 0), gate_spec(gm_col0, 1),
            pl.BlockSpec((None, ka, d), lambda i: (layer, 0, 0)),
            pl.BlockSpec((None, km, d), lambda i: (layer, 0, 0)),
            pl.BlockSpec((None, d, d), lambda i: (layer, 0, 0)),
            pl.BlockSpec((tm, d), lambda i: (i, 0)),
            pl.BlockSpec((None, 6, d), lambda i: (i // blocks_per_batch, 0, 0)),
            pl.BlockSpec((1, d), lambda i: (0, 0)),
            pl.BlockSpec((LANES, d), lambda i: (0, 0)),
            pl.BlockSpec((1, LANES), lambda i: (0, 0)),
        ],
        out_specs=[
            pl.BlockSpec((tm, d), lambda i: (i, 0)),
            pl.BlockSpec((tm, half), lambda i: (i, 0)),
            pl.BlockSpec((tm, LANES), lambda i: (i, 0)),
        ],
        out_shape=[jax.ShapeDtypeStruct((n, d), F32), jax.ShapeDtypeStruct((n, half), jnp.uint32),
                   jax.ShapeDtypeStruct((n, LANES), F32)],
        name="mixer_tail",
        compiler_params=_cparams(("arbitrary",)),
    )(y_attn, h_m, p_all, p_all, p_all, p_all, w_ba16, w_bm16, w_out16, x2, mod_l, norm_w.reshape(1, d),
      w_side, b_side)


def _moe_kernel(tok_ref, src_ref, be_ref, first_ref, nxt_ref, wslot_ref, nact_ref, h_hbm, wg_hbm, wu_hbm, wd_hbm,
                o_ref, xbuf, xb16, wg_st, wu_st, wd_st, xsem, wsem, *, layer):
    i = pl.program_id(0)
    nact = nact_ref[0]
    nbuf, blk = xbuf.shape[0], xbuf.shape[1]
    stages = ((wg_hbm, wg_st), (wu_hbm, wu_st), (wd_hbm, wd_st))

    def weight_copy(k, e, slot):
        return pltpu.make_async_copy(stages[k][0].at[layer, e], stages[k][1].at[slot], wsem.at[slot, k])

    weight_queue = 1

    def start_gather(j):
        base = src_ref[j]
        ring = j % nbuf
        dst = xbuf.at[ring]
        for r in range(blk):
            tok = tok_ref[base + r]
            pltpu.make_async_copy(h_hbm.at[pl.ds(tok, 1)], dst.at[pl.ds(r, 1)], xsem.at[ring]).start()

    def wait_gather(j):
        ring = j % nbuf
        pltpu.make_async_copy(h_hbm.at[pl.ds(0, blk)], xbuf.at[ring], xsem.at[ring]).wait()

    @pl.when(i == 0)
    def _():
        for k in range(3):
            weight_copy(k, be_ref[0], wslot_ref[0]).start(priority=weight_queue)
        for j in range(nbuf - 1):
            start_gather(j)

    @pl.when((i < nact) & (first_ref[i] == 1))
    def _():
        slot = wslot_ref[i]
        e_next = nxt_ref[i]

        @pl.when(e_next >= 0)
        def _():
            for k in range(3):
                weight_copy(k, e_next, 1 - slot).start(priority=weight_queue)

        for k in range(3):
            weight_copy(k, be_ref[i], slot).wait()

    def compute(slot):
        wait_gather(i)
        xb16[...] = _unpack_bf16_pairs(xbuf[i % nbuf]).astype(BF16)
        start_gather(i + nbuf - 1)
        x = xb16[...]
        g = jnp.dot(x, wg_st[slot].astype(BF16), preferred_element_type=F32)
        u = jnp.dot(x, wu_st[slot].astype(BF16), preferred_element_type=F32)
        a = (_silu(g) * u).astype(BF16)
        o_ref[...] = _pack_bf16_pairs(jnp.dot(a, wd_st[slot].astype(BF16), preferred_element_type=F32))

    for static_slot in range(2):
        @pl.when((i < nact) & (wslot_ref[i] == static_slot))
        def _(static_slot=static_slot):
            compute(static_slot)

    @pl.when(i >= nact)
    def _():
        @pl.when(i < nact + nbuf - 1)
        def _():
            wait_gather(i)

        o_ref[...] = jnp.zeros(o_ref.shape, o_ref.dtype)


def _moe_experts(h2, tok_src, blk_src, blk_expert, blk_first, blk_next, blk_wslot, n_active, w_gate, w_up, w_down,
                 layer):
    n, dh = h2.shape
    d = 2 * dh
    f = w_gate.shape[3]
    n_steps = blk_expert.shape[0]
    n_blocks = n_steps - (MOE_RING - 2)
    cap = n_blocks * MOE_BLK
    any_spec = pl.BlockSpec(memory_space=pl.ANY)
    grid_spec = pltpu.PrefetchScalarGridSpec(
        num_scalar_prefetch=7,
        grid=(n_steps,),
        in_specs=[any_spec, any_spec, any_spec, any_spec],
        out_specs=pl.BlockSpec((MOE_BLK, dh), lambda i, *_: (jnp.minimum(i, n_blocks - 1), 0)),
        scratch_shapes=[
            pltpu.VMEM((MOE_RING, MOE_BLK, dh), jnp.uint32),
            pltpu.VMEM((MOE_BLK, d), BF16),
            pltpu.VMEM((2, d, f), F32), pltpu.VMEM((2, d, f), F32), pltpu.VMEM((2, f, d), F32),
            pltpu.SemaphoreType.DMA((MOE_RING,)),
            pltpu.SemaphoreType.DMA((2, 3)),
        ],
    )
    return pl.pallas_call(
        functools.partial(_moe_kernel, layer=layer),
        grid_spec=grid_spec,
        out_shape=jax.ShapeDtypeStruct((cap, dh), jnp.uint32),
        name="moe_experts",
        compiler_params=_cparams(("arbitrary",)),
    )(tok_src, blk_src, blk_expert, blk_first, blk_next, blk_wslot, n_active, h2, w_gate, w_up, w_down)


def _combine_kernel(pos_ref, yb_hbm, x_ref, w_ref, mod_ref, nw_ref, nmod_ref, ws_ref, bs_ref, *out_and_scratch,
                    last, side_cols):
    if last:
        o_ref, buf, sem = out_and_scratch
    else:
        o_ref, h_ref, s_ref, buf, sem = out_and_scratch
    i = pl.program_id(0)
    nsteps = pl.num_programs(0)
    t = x_ref.shape[0]

    def start(j, slot):
        base = j * (t * TOP_K)
        dst = buf.at[slot]
        for r in range(t):
            for k in range(TOP_K):
                p = pos_ref[base + r * TOP_K + k]
                pltpu.make_async_copy(yb_hbm.at[pl.ds(p, 1)], dst.at[k, pl.ds(r, 1)], sem.at[slot]).start()

    def wait(slot):
        for k in range(TOP_K):
            pltpu.make_async_copy(yb_hbm.at[pl.ds(0, t)], buf.at[slot, k], sem.at[slot]).wait()

    @pl.when(i == 0)
    def _():
        start(0, 0)

    slot = i % 2

    @pl.when(i + 1 < nsteps)
    def _():
        start(i + 1, 1 - slot)

    wait(slot)
    w = w_ref[...]
    y = w[:, 0:1] * _unpack_bf16_pairs(buf[slot, 0]) + w[:, 1:2] * _unpack_bf16_pairs(buf[slot, 1])
    x = x_ref[...] + mod_ref[5:6, :] * y
    if last:
        o_ref[...] = x * lax.rsqrt(jnp.mean(x * x, axis=-1, keepdims=True) + EPS) * nw_ref[...]
    else:
        o_ref[...] = x
        h = _rms_mod(x, nw_ref[...], nmod_ref[1:2, :], nmod_ref[0:1, :])
        h_ref[...] = h.astype(h_ref.dtype)
        s_ref[...] = _side_proj(h, ws_ref, bs_ref, side_cols, False)


def _combine(yb, pos, weights, x2, mod_l, seq, next_norm_w, next_mod, w_side, side_spec, b_side, side_cols, last,
             tm=256):
    n, d = x2.shape
    blocks_per_batch = seq // tm
    row_spec = pl.BlockSpec((tm, d), lambda i, *_: (i, 0))
    mod_spec = pl.BlockSpec((None, 6, d), lambda i, *_: (i // blocks_per_batch, 0, 0))
    if last:
        out_specs = row_spec
        out_shape = jax.ShapeDtypeStruct((n, d), F32)
    else:
        out_specs = [row_spec, row_spec, pl.BlockSpec((tm, LANES), lambda i, *_: (i, 0))]
        out_shape = [jax.ShapeDtypeStruct((n, d), F32), jax.ShapeDtypeStruct((n, d), BF16),
                     jax.ShapeDtypeStruct((n, LANES), F32)]
    grid_spec = pltpu.PrefetchScalarGridSpec(
        num_scalar_prefetch=1,
        grid=(n // tm,),
        in_specs=[
            pl.BlockSpec(memory_space=pl.ANY),
            row_spec,
            pl.BlockSpec((tm, TOP_K), lambda i, *_: (i, 0)),
            mod_spec,
            pl.BlockSpec((1, d), lambda i, *_: (0, 0)),
            mod_spec,
            side_spec,
            pl.BlockSpec((1, LANES), lambda i, *_: (0, 0)),
        ],
        out_specs=out_specs,
        scratch_shapes=[pltpu.VMEM((2, TOP_K, tm, yb.shape[1]), yb.dtype), pltpu.SemaphoreType.DMA((2,))],
    )
    return pl.pallas_call(
        functools.partial(_combine_kernel, last=last, side_cols=side_cols),
        grid_spec=grid_spec,
        out_shape=out_shape,
        name="moe_combine",
        compiler_params=_cparams(("arbitrary",)),
    )(pos.reshape(-1), yb, x2, weights, mod_l, next_norm_w.reshape(1, d), next_mod, w_side, b_side)


def _route_kernel(lg_ref, oi_ref, ow_ref, cnt_ref, carry):
    @pl.when(pl.program_id(0) == 0)
    def _():
        carry[...] = jnp.zeros(carry.shape, F32)

    lg = lg_ref[...]
    t = lg.shape[0]
    lane = lax.broadcasted_iota(jnp.int32, lg.shape, 1)
    big = jnp.int32(1 << 30)

    def first_max(vals):
        top = jnp.max(vals, axis=-1, keepdims=True)
        return top, jnp.min(jnp.where(vals == top, lane, big), axis=-1, keepdims=True)

    coarse = lane < N_GROUPS
    gmax, grp = first_max(jnp.where(coarse, lg, NEG_INF))
    p_grp = 1.0 / jnp.sum(jnp.where(coarse, jnp.exp(lg - gmax), 0.0), axis=-1, keepdims=True)
    lo = N_GROUPS + grp * EXPERTS_PER_GROUP
    fine = jnp.where((lane >= lo) & (lane < lo + EXPERTS_PER_GROUP), lg, NEG_INF)
    v1, i1 = first_max(fine)
    v2, i2 = first_max(jnp.where(lane == i1, NEG_INF, fine))
    r = jnp.exp(v2 - v1)
    w1 = p_grp / (1.0 + r)
    w2 = w1 * r
    e1 = i1 - N_GROUPS
    e2 = i2 - N_GROUPS

    hit1 = lane == e1
    hit2 = lane == e2
    picks = (hit1 | hit2).astype(BF16)
    row = lax.broadcasted_iota(jnp.int32, (t, t), 0)
    col = lax.broadcasted_iota(jnp.int32, (t, t), 1)
    before = (col < row).astype(BF16)
    seen = carry[...] + jnp.dot(before, picks, preferred_element_type=F32)
    rank1 = jnp.sum(jnp.where(hit1, seen, 0.0), axis=-1, keepdims=True).astype(jnp.int32)
    rank2 = jnp.sum(jnp.where(hit2, seen, 0.0), axis=-1, keepdims=True).astype(jnp.int32)
    carry[...] = carry[...] + jnp.sum(picks.astype(F32), axis=0, keepdims=True)

    oi_ref[...] = jnp.where(lane == 0, e1, jnp.where(lane == 1, e2, jnp.where(lane == 2, rank1, rank2)))
    ow_ref[...] = jnp.where(lane == 0, w1, w2)
    cnt_ref[...] = carry[...]


def _route(logits, tm=1024):
    n = logits.shape[0]
    tm = min(tm, n)
    oi, ow, cnt = pl.pallas_call(
        _route_kernel,
        grid=(n // tm,),
        in_specs=[pl.BlockSpec((tm, LANES), lambda i: (i, 0))],
        out_specs=[pl.BlockSpec((tm, LANES), lambda i: (i, 0)), pl.BlockSpec((tm, LANES), lambda i: (i, 0)),
                   pl.BlockSpec((1, LANES), lambda i: (0, 0))],
        out_shape=[jax.ShapeDtypeStruct((n, LANES), jnp.int32), jax.ShapeDtypeStruct((n, LANES), F32),
                   jax.ShapeDtypeStruct((1, LANES), F32)],
        scratch_shapes=[pltpu.VMEM((1, LANES), F32)],
        name="route",
        compiler_params=_cparams(("arbitrary",)),
    )(logits)
    expert = oi[:, 0:TOP_K]
    rank = oi[:, TOP_K:2 * TOP_K]
    weights = ow[:, 0:TOP_K]
    counts = cnt[0, :N_EXPERTS].astype(jnp.int32)
    return expert, rank, weights, counts


def _dispatch(expert, rank, counts):
    n_tok = expert.shape[0]
    n_assign = n_tok * TOP_K
    cap = n_assign + N_EXPERTS * MOE_BLK
    n_blocks = cap // MOE_BLK + MOE_RING - 2
    e_flat = expert.reshape(-1)
    padded = ((counts + MOE_BLK - 1) // MOE_BLK) * MOE_BLK
    pad_ends = jnp.cumsum(padded)
    pad_starts = pad_ends - padded
    dest = (pad_starts[e_flat] + rank.reshape(-1)).astype(jnp.int32)
    order = jnp.argsort(e_flat, stable=True)
    tok_src = jnp.concatenate([(order // TOP_K).astype(jnp.int32), jnp.zeros((MOE_BLK,), jnp.int32)])
    starts = jnp.cumsum(counts) - counts
    blk_start = jnp.arange(n_blocks, dtype=jnp.int32) * MOE_BLK
    blk_expert = jnp.minimum(jnp.sum((pad_ends[None, :] <= blk_start[:, None]).astype(jnp.int32), axis=1),
                             N_EXPERTS - 1).astype(jnp.int32)
    blk_src = jnp.clip(blk_start - (pad_starts - starts)[blk_expert], 0, n_assign).astype(jnp.int32)
    n_active = (pad_ends[-1] // MOE_BLK).astype(jnp.int32)
    prev = jnp.concatenate([jnp.full((1,), -1, jnp.int32), blk_expert[:-1]])
    blk_first = (blk_expert != prev).astype(jnp.int32)
    run_end = pad_ends[blk_expert] // MOE_BLK
    blk_next = jnp.where(run_end < n_active, blk_expert[jnp.minimum(run_end, n_blocks - 1)], -1).astype(jnp.int32)
    blk_wslot = ((jnp.cumsum((counts > 0).astype(jnp.int32)) - 1)[blk_expert] % 2).astype(jnp.int32)
    return (tok_src, blk_src, blk_expert, blk_first, blk_next, blk_wslot, n_active.reshape(1),
            dest.reshape(n_tok, TOP_K))


def kernel(x, c, ada_w, ada_b, norm1_w, norm2_w, w_in, conv_q, conv_k, igate_b, fgate_b, rel_bias,
           mlstm_norm_w, w_branch_attn, w_branch_mlstm, w_out, router_coarse_w, router_coarse_b,
           router_fine_w, router_fine_b, w_gate, w_up, w_down, final_norm_w):
    b, s, d = x.shape
    depth = ada_w.shape[0]
    n = b * s
    nc = s // CHUNK
    a_width = A_HEADS * A_HEAD_DIM
    m_width = M_HEADS * M_HEAD_DIM
    main_cols = 3 * a_width + 4 * m_width + 2 * d
    ga_col0 = 3 * a_width + 4 * m_width
    gm_col0 = ga_col0 + d

    mod = _ada_mod(c, ada_w, ada_b)
    x2 = x.reshape(n, d)

    w_in_t = jnp.swapaxes(w_in, 1, 2)
    w_ba16 = w_branch_attn.astype(BF16)
    w_bm16 = w_branch_mlstm.astype(BF16)
    w_out16 = w_out.astype(BF16)

    def gate_spec(l):
        return pl.BlockSpec((None, LANES, d), lambda i, *_: (l, main_cols // LANES, 0))

    def gate_bias(l):
        return jnp.zeros((1, LANES), F32).at[0, :M_HEADS].set(igate_b[l]).at[0, M_HEADS:2 * M_HEADS].set(fgate_b[l])

    h, gates = _norm_mod(x2, norm1_w[0], mod[0], w_in_t, gate_spec(0), 2 * M_HEADS, gate_bias(0), seq=s,
                         shift_row=0, scale_row=1, precise=False, out_dtype=BF16)
    out = None
    for l in range(depth):
        w_r = (jnp.zeros((LANES, d), F32).at[:N_GROUPS].set(router_coarse_w[l].T)
               .at[N_GROUPS:N_GROUPS + N_EXPERTS].set(router_fine_w[l].T))
        b_r = (jnp.zeros((1, LANES), F32).at[0, :N_GROUPS].set(router_coarse_b[l])
               .at[0, N_GROUPS:N_GROUPS + N_EXPERTS].set(router_fine_b[l]))

        p_all = _proj(h, w_in_t, l, main_cols)
        y_attn = _attention(p_all, _attn_bias(rel_bias[l]), b, s)
        gates_t = gates[:, :2 * M_HEADS].reshape(b, nc, CHUNK, 2 * M_HEADS).transpose(0, 1, 3, 2)
        h_m = _mlstm(p_all, gates_t, conv_q, conv_k, mlstm_norm_w[l], l, b, s)
        x2, h2, logits = _mixer_tail(y_attn, h_m, p_all, w_ba16, w_bm16, w_out16, l, x2, mod[l], norm2_w[l],
                                     w_r, b_r, N_GROUPS + N_EXPERTS, s, ga_col0, gm_col0)
        expert, rank, weights, counts = _route(logits)
        tok_src, blk_src, blk_expert, blk_first, blk_next, blk_wslot, n_active, pos = _dispatch(expert, rank, counts)
        yb = _moe_experts(h2, tok_src, blk_src, blk_expert, blk_first, blk_next, blk_wslot, n_active,
                          w_gate, w_up, w_down, l)
        if l + 1 < depth:
            x2, h, gates = _combine(yb, pos, weights, x2, mod[l], s, norm1_w[l + 1], mod[l + 1], w_in_t,
                                    gate_spec(l + 1), gate_bias(l + 1), 2 * M_HEADS, last=False)
        else:
            out = _combine(yb, pos, weights, x2, mod[l], s, final_norm_w, mod[l], w_in_t, gate_spec(l),
                           gate_bias(l), 2 * M_HEADS, last=True)

    return out.reshape(b, s, d)
```
